```python
import math
import jax, jax.numpy as jnp
from jax import lax
import numpy as np

D_MODEL = 1024
BATCH = 16
SEQ = 2048
DEPTH = 1
DEC_BATCH = 8
DEC_SEQ = 64
PAST_LEN = 4096

CHUNK = 64
Q_BLOCK = 128
EPS = 1e-6
NEG_INF = -1e30
SSM_WIDTH = D_MODEL // 2
SSM_GROUP = 16
SSM_GROUPS = SSM_WIDTH // SSM_GROUP
SSM_STATE = 64
DT_MIN = 0.001
DT_MAX = 0.1
N_HEADS = 8
HEAD_DIM = 64
ATTN_WIDTH = N_HEADS * HEAD_DIM
IDX_HEADS = 8
IDX_DIM = 32
IDX_WEIGHT_SCALE = (IDX_HEADS * IDX_DIM) ** -0.5
TOPK_MAX = 256
N_GROUPS = 4
EXPERTS_PER_GROUP = 4
N_EXPERTS = N_GROUPS * EXPERTS_PER_GROUP
EXPERT_DIM = 256
TOP_K_IN_GROUP = 2
IN_SPLITS = (SSM_WIDTH, ATTN_WIDTH, HEAD_DIM, HEAD_DIM, IDX_HEADS * IDX_DIM, IDX_DIM, IDX_HEADS, D_MODEL, D_MODEL)
IN_WIDTH = sum(IN_SPLITS)
IN_OFFSETS = tuple(int(o) for o in np.cumsum(IN_SPLITS)[:-1])

kernel_name = 'chunk_causal_s5_dsa_hmoe_step'


def rms_norm(x, gain=None):
    x32 = x.astype(jnp.float32)
    y = x32 * lax.rsqrt(jnp.mean(x32 * x32, axis=-1, keepdims=True) + EPS)
    if gain is not None:
        y = y * gain.astype(jnp.float32)
    return y.astype(x.dtype)


def _complex_affine_combine(e1, e2):
    a1r, a1i, b1r, b1i = e1
    a2r, a2i, b2r, b2i = e2
    return (a1r * a2r - a1i * a2i,
            a1r * a2i + a1i * a2r,
            a2r * b1r - a2i * b1i + b2r,
            a2r * b1i + a2i * b1r + b2i)


def s5_scan(u, a_re, a_im, log_dt, b_re, b_im, c_re, c_im, d_skip, h0_re, h0_im):
    bsz, t_len, _ = u.shape
    f32 = jnp.float32
    ug = u.astype(f32).reshape(bsz, t_len, SSM_GROUPS, SSM_GROUP)
    ar = a_re.astype(f32)
    ai = a_im.astype(f32)
    dt = jnp.exp(log_dt.astype(f32))[:, None]
    decay = jnp.exp(ar * dt)
    lam_re = decay * jnp.cos(ai * dt)
    lam_im = decay * jnp.sin(ai * dt)
    den = ar * ar + ai * ai
    num_re = lam_re - 1.0
    coef_re = (num_re * ar + lam_im * ai) / den
    coef_im = (lam_im * ar - num_re * ai) / den
    br = b_re.astype(f32)
    bi = b_im.astype(f32)
    bb_re = coef_re[..., None] * br - coef_im[..., None] * bi
    bb_im = coef_re[..., None] * bi + coef_im[..., None] * br
    x_re = jnp.einsum('gpj,btgj->btgp', bb_re, ug)
    x_im = jnp.einsum('gpj,btgj->btgp', bb_im, ug)
    if h0_re is not None:
        hr0 = h0_re.astype(f32)
        hi0 = h0_im.astype(f32)
        x_re = x_re.at[:, 0].add(lam_re * hr0 - lam_im * hi0)
        x_im = x_im.at[:, 0].add(lam_re * hi0 + lam_im * hr0)
    lr = jnp.broadcast_to(lam_re, x_re.shape)
    li = jnp.broadcast_to(lam_im, x_im.shape)
    _, _, h_re, h_im = lax.associative_scan(_complex_affine_combine, (lr, li, x_re, x_im), axis=1)
    y = (jnp.einsum('gjp,btgp->btgj', c_re.astype(f32), h_re)
         - jnp.einsum('gjp,btgp->btgj', c_im.astype(f32), h_im)
         + d_skip.astype(f32).reshape(SSM_GROUPS, SSM_GROUP) * ug)
    return y.reshape(bsz, t_len, SSM_WIDTH).astype(u.dtype), h_re[:, -1], h_im[:, -1]


def dsa_block(q, qi, wi, q_pos, k, v, ki, k_pos, top_k):
    f32 = jnp.float32
    idx_dot = jnp.einsum('bthi,bsi->bths', qi, ki)
    score = jnp.einsum('bth,bths->bts', wi.astype(f32), jax.nn.relu(idx_dot).astype(f32))
    q_chunk = q_pos // CHUNK
    admissible = (k_pos[None, :] // CHUNK) <= q_chunk[:, None]
    score = jnp.where(admissible[None], score, NEG_INF)
    _, sel = lax.top_k(score, top_k)
    gather = jax.vmap(lambda rows, ids: rows[ids])
    k_sel = gather(k, sel)
    v_sel = gather(v, sel)
    valid = (k_pos[sel] // CHUNK) <= q_chunk[None, :, None]
    logits = jnp.einsum('bthd,btjd->bthj', q, k_sel).astype(f32) * HEAD_DIM ** -0.5
    logits = jnp.where(valid[:, :, None, :], logits, NEG_INF)
    probs = jax.nn.softmax(logits, axis=-1).astype(v.dtype)
    return jnp.einsum('bthj,btjd->bthd', probs, v_sel)


def dsa_prompt(q, k, v, qi, ki, wi):
    bsz, seq_len = k.shape[0], k.shape[1]
    top_k = min(TOPK_MAX, seq_len // 4)
    pos = jnp.arange(seq_len, dtype=jnp.int32)
    n_blk = seq_len // Q_BLOCK

    def to_blocks(t):
        return jnp.swapaxes(t.reshape((bsz, n_blk, Q_BLOCK) + t.shape[2:]), 0, 1)

    def body(args):
        qb, qib, wib, pb = args
        return dsa_block(qb, qib, wib, pb, k, v, ki, pos, top_k)

    out = lax.map(body, (to_blocks(q), to_blocks(qi), to_blocks(wi), pos.reshape(n_blk, Q_BLOCK)))
    return jnp.swapaxes(out, 0, 1).reshape(bsz, seq_len, N_HEADS, HEAD_DIM)


def dsa_step(q, k, v, qi, ki, wi, past_k, past_v, past_ki):
    past_len = past_k.shape[1]
    t_len = q.shape[1]
    k_all = jnp.concatenate([past_k, k], axis=1)
    v_all = jnp.concatenate([past_v, v], axis=1)
    ki_all = jnp.concatenate([past_ki, ki], axis=1)
    total = past_len + t_len
    top_k = min(TOPK_MAX, total // 4)
    k_pos = jnp.arange(total, dtype=jnp.int32)
    q_pos = past_len + jnp.arange(t_len, dtype=jnp.int32)
    return dsa_block(q, qi, wi, q_pos, k_all, v_all, ki_all, k_pos, top_k)


def hier_moe(h, w_rg, b_rg, w_re, b_re, w_gate, w_up, w_down):
    shp = h.shape
    f32 = jnp.float32
    t = h.reshape(-1, D_MODEL)
    g_logits = (t @ w_rg + b_rg).astype(f32)
    g_prob = jax.nn.softmax(g_logits, axis=-1)
    g_idx = jnp.argmax(g_logits, axis=-1)
    g_w = jnp.take_along_axis(g_prob, g_idx[:, None], axis=-1)
    e_logits = (t @ w_re + b_re).astype(f32).reshape(-1, N_GROUPS, EXPERTS_PER_GROUP)
    e_in = jnp.take_along_axis(e_logits, g_idx[:, None, None], axis=1)[:, 0]
    top_v, top_i = lax.top_k(e_in, TOP_K_IN_GROUP)
    top_w = jax.nn.softmax(top_v, axis=-1) * g_w
    eid = g_idx[:, None] * EXPERTS_PER_GROUP + top_i
    combine = jnp.sum(jax.nn.one_hot(eid, N_EXPERTS, dtype=f32) * top_w[..., None], axis=1)
    hid = jax.nn.silu(jnp.einsum('nd,edf->nef', t, w_gate)) * jnp.einsum('nd,edf->nef', t, w_up)
    out = jnp.einsum('nef,efd->nd', hid * combine[..., None].astype(hid.dtype), w_down)
    return out.reshape(shp)


def hybrid_layer(x, c, lw, past):
    (w_ada, b_ada, w_in, a_re, a_im, log_dt, sb_re, sb_im, sc_re, sc_im, d_skip, w_glu,
     q_gain, k_gain, kidx_gain, w_attn_out, w_out,
     w_rg, b_rg, w_re, b_re, w_gate, w_up, w_down) = lw
    bsz, t_len, _ = x.shape
    mod = jax.nn.silu(c) @ w_ada + b_ada
    sh1, sc1, g1, sh2, sc2, g2 = jnp.split(mod[:, None, :], 6, axis=-1)
    h = rms_norm(x) * (1.0 + sc1) + sh1
    proj = h @ w_in
    u, q, k, v, qi, ki, wi, gate_a, gate_b = jnp.split(proj, IN_OFFSETS, axis=-1)
    q = rms_norm(q.reshape(bsz, t_len, N_HEADS, HEAD_DIM), q_gain)
    k = rms_norm(k, k_gain)
    qi = qi.reshape(bsz, t_len, IDX_HEADS, IDX_DIM)
    ki = rms_norm(ki, kidx_gain)
    wi = wi * IDX_WEIGHT_SCALE
    if past is None:
        y_ssm, s_re, s_im = s5_scan(u, a_re, a_im, log_dt, sb_re, sb_im, sc_re, sc_im, d_skip, None, None)
        attn = dsa_prompt(q, k, v, qi, ki, wi)
    else:
        past_k, past_v, past_ki, h0_re, h0_im = past
        y_ssm, s_re, s_im = s5_scan(u, a_re, a_im, log_dt, sb_re, sb_im, sc_re, sc_im, d_skip, h0_re, h0_im)
        attn = dsa_step(q, k, v, qi, ki, wi, past_k, past_v, past_ki)
    z = jax.nn.gelu(y_ssm)
    glu_val, glu_gate = jnp.split(z @ w_glu, 2, axis=-1)
    out_a = glu_val * jax.nn.sigmoid(glu_gate)
    out_b = attn.reshape(bsz, t_len, ATTN_WIDTH) @ w_attn_out
    merged = jax.nn.sigmoid(gate_a) * out_a + jax.nn.sigmoid(gate_b) * out_b
    x = x + g1 * (merged @ w_out)
    h2 = rms_norm(x) * (1.0 + sc2) + sh2
    x = x + g2 * hier_moe(h2, w_rg, b_rg, w_re, b_re, w_gate, w_up, w_down)
    return x, k, v, ki, s_re, s_im


def setup_inputs(seed: int = 0) -> dict:
    key = jax.random.key(seed)
    ks = jax.random.split(key, 40)
    f32 = jnp.float32

    def nrm(i, shape, scale):
        return jax.random.normal(ks[i], shape, f32) * scale

    G, P, J = SSM_GROUPS, SSM_STATE, SSM_GROUP
    return {
        'x_prompt': nrm(0, (BATCH, SEQ, D_MODEL), 1.0),
        'x_sample': nrm(1, (DEC_BATCH, DEC_SEQ, D_MODEL), 1.0),
        'cache_k': nrm(2, (DEPTH, DEC_BATCH, PAST_LEN, HEAD_DIM), 1.0),
        'cache_v': nrm(3, (DEPTH, DEC_BATCH, PAST_LEN, HEAD_DIM), 1.0),
        'cache_kidx': nrm(4, (DEPTH, DEC_BATCH, PAST_LEN, IDX_DIM), 1.0),
        'state_ssm_re': nrm(5, (DEPTH, DEC_BATCH, G, P), 0.3),
        'state_ssm_im': nrm(6, (DEPTH, DEC_BATCH, G, P), 0.3),
        'c_prompt': nrm(7, (BATCH, D_MODEL), 1.0),
        'c_sample': nrm(8, (DEC_BATCH, D_MODEL), 1.0),
        'w_ada': nrm(9, (DEPTH, D_MODEL, 6 * D_MODEL), D_MODEL ** -0.5),
        'b_ada': nrm(10, (DEPTH, 6 * D_MODEL), 0.02),
        'w_in': nrm(11, (DEPTH, D_MODEL, IN_WIDTH), D_MODEL ** -0.5),
        'ssm_a_re': -0.5 + nrm(12, (DEPTH, G, P), 0.01),
        'ssm_a_im': jnp.pi * jnp.arange(P, dtype=f32) + nrm(13, (DEPTH, G, P), 0.01),
        'ssm_log_dt': jax.random.uniform(ks[14], (DEPTH, G), f32, math.log(DT_MIN), math.log(DT_MAX)),
        'ssm_b_re': nrm(15, (DEPTH, G, P, J), (2 * J) ** -0.5),
        'ssm_b_im': nrm(16, (DEPTH, G, P, J), (2 * J) ** -0.5),
        'ssm_c_re': nrm(17, (DEPTH, G, J, P), P ** -0.5),
        'ssm_c_im': nrm(18, (DEPTH, G, J, P), P ** -0.5),
        'ssm_d': nrm(19, (DEPTH, SSM_WIDTH), 0.5),
        'w_glu': nrm(20, (DEPTH, SSM_WIDTH, 2 * D_MODEL), SSM_WIDTH ** -0.5),
        'q_gain': 1.0 + nrm(21, (DEPTH, HEAD_DIM), 0.02),
        'k_gain': 1.0 + nrm(22, (DEPTH, HEAD_DIM), 0.02),
        'kidx_gain': 1.0 + nrm(23, (DEPTH, IDX_DIM), 0.02),
        'w_attn_out': nrm(24, (DEPTH, ATTN_WIDTH, D_MODEL), ATTN_WIDTH ** -0.5),
        'w_out': nrm(25, (DEPTH, D_MODEL, D_MODEL), D_MODEL ** -0.5),
        'w_route_group': nrm(26, (DEPTH, D_MODEL, N_GROUPS), D_MODEL ** -0.5),
        'b_route_group': nrm(27, (DEPTH, N_GROUPS), 0.01),
        'w_route_expert': nrm(28, (DEPTH, D_MODEL, N_EXPERTS), D_MODEL ** -0.5),
        'b_route_expert': nrm(29, (DEPTH, N_EXPERTS), 0.01),
        'w_gate': nrm(30, (DEPTH, N_EXPERTS, D_MODEL, EXPERT_DIM), D_MODEL ** -0.5),
        'w_up': nrm(31, (DEPTH, N_EXPERTS, D_MODEL, EXPERT_DIM), D_MODEL ** -0.5),
        'w_down': nrm(32, (DEPTH, N_EXPERTS, EXPERT_DIM, D_MODEL), EXPERT_DIM ** -0.5),
    }


def reference(x_prompt, x_sample, cache_k, cache_v, cache_kidx, state_ssm_re, state_ssm_im,
              c_prompt, c_sample, w_ada, b_ada, w_in, ssm_a_re, ssm_a_im, ssm_log_dt,
              ssm_b_re, ssm_b_im, ssm_c_re, ssm_c_im, ssm_d, w_glu, q_gain, k_gain, kidx_gain,
              w_attn_out, w_out, w_route_group, b_route_group, w_route_expert, b_route_expert,
              w_gate, w_up, w_down):
    xp = x_prompt
    xs = x_sample
    kp_l, vp_l, kip_l, srp_l, sip_l = [], [], [], [], []
    ks_l, vs_l, kis_l, srs_l, sis_l = [], [], [], [], []
    for l in range(DEPTH):
        lw = (w_ada[l], b_ada[l], w_in[l], ssm_a_re[l], ssm_a_im[l], ssm_log_dt[l],
              ssm_b_re[l], ssm_b_im[l], ssm_c_re[l], ssm_c_im[l], ssm_d[l], w_glu[l],
              q_gain[l], k_gain[l], kidx_gain[l], w_attn_out[l], w_out[l],
              w_route_group[l], b_route_group[l], w_route_expert[l], b_route_expert[l],
              w_gate[l], w_up[l], w_down[l])
        xp, kp, vp, kip, srp, sip = hybrid_layer(xp, c_prompt, lw, None)
        past = (cache_k[l], cache_v[l], cache_kidx[l], state_ssm_re[l], state_ssm_im[l])
        xs, kss, vss, kiss, srs, sis = hybrid_layer(xs, c_sample, lw, past)
        kp_l.append(kp); vp_l.append(vp); kip_l.append(kip); srp_l.append(srp); sip_l.append(sip)
        ks_l.append(kss); vs_l.append(vss); kis_l.append(kiss); srs_l.append(srs); sis_l.append(sis)
    return (xp, xs,
            jnp.stack(kp_l), jnp.stack(vp_l), jnp.stack(kip_l), jnp.stack(srp_l), jnp.stack(sip_l),
            jnp.stack(ks_l), jnp.stack(vs_l), jnp.stack(kis_l), jnp.stack(srs_l), jnp.stack(sis_l))
```

```python
import functools

import jax
import jax.numpy as jnp
import numpy as np
from jax import lax
from jax.experimental import pallas as pl
from jax.experimental.pallas import tpu as pltpu

F32 = jnp.float32
BF16 = jnp.bfloat16
I32 = jnp.int32

D_MODEL = 1024
CHUNK_SHIFT = 6
EPS = 1e-6
NEG_INF = -1e30
SSM_WIDTH = 512
SSM_GROUP = 16
SSM_GROUPS = 32
SSM_STATE = 64
SSM_LANES = SSM_GROUPS * SSM_STATE
N_HEADS = 8
HEAD_DIM = 64
IDX_HEADS = 8
IDX_DIM = 32
IDX_WEIGHT_SCALE = (IDX_HEADS * IDX_DIM) ** -0.5
TOPK_MAX = 256
N_GROUPS = 4
EXPERTS_PER_GROUP = 4
N_EXPERTS = 16
EXPERT_DIM = 256
LANE = 128
INT_MIN = -(2 ** 31)
VMEM_LIMIT = 56 * 1024 * 1024

C_U = 0
C_Q = C_U + SSM_WIDTH
C_K = C_Q + N_HEADS * LANE
C_V = C_K + LANE
C_QI = C_V + LANE
C_KI = C_QI + IDX_HEADS * IDX_DIM
C_WI = C_KI + LANE
C_GA = C_WI + LANE
C_GB = C_GA + D_MODEL
C_END = C_GB + D_MODEL


def _dot(a, b):
    return jnp.dot(a, b, preferred_element_type=F32)


def _dot_nt(a, b):
    return lax.dot_general(a, b, (((1,), (1,)), ((), ())), preferred_element_type=F32)


def _split_bf16(x):
    hi = x.astype(BF16)
    lo = (x - hi.astype(F32)).astype(BF16)
    return hi, lo


def _dot_hilo(a, b):
    ah, al = _split_bf16(a)
    bh, bl = _split_bf16(b)
    return _dot(ah, bh) + _dot(ah, bl) + _dot(al, bh)


def _mod_spec(tm, t_len):
    if tm >= t_len:
        return pl.BlockSpec((tm // t_len, 1, 6 * D_MODEL), lambda i, *_: (i, 0, 0))
    per = t_len // tm
    return pl.BlockSpec((1, 1, 6 * D_MODEL), lambda i, *_: (i // per, 0, 0))


def _modulate(x, mod, scale_seg, shift_seg):
    nb = mod.shape[0]
    tm = x.shape[0]
    sc = mod[:, :, scale_seg * D_MODEL:(scale_seg + 1) * D_MODEL]
    sh = mod[:, :, shift_seg * D_MODEL:(shift_seg + 1) * D_MODEL]
    x3 = x.reshape(nb, tm // nb, D_MODEL)
    return (x3 * (1.0 + sc) + sh).reshape(tm, D_MODEL)


def _gate_mul(x, mod, seg):
    nb = mod.shape[0]
    tm = x.shape[0]
    g = mod[:, :, seg * D_MODEL:(seg + 1) * D_MODEL]
    return (x.reshape(nb, tm // nb, D_MODEL) * g).reshape(tm, D_MODEL)


def _rms(x, n):
    return x * lax.rsqrt(jnp.sum(x * x, axis=-1, keepdims=True) * (1.0 / n) + EPS)


def _adaln_body(c_ref, w_ref, b_ref, o_ref):
    c = c_ref[...]
    s = c * jax.nn.sigmoid(c)
    o_ref[...] = _dot_hilo(s, w_ref[...]) + b_ref[...]


def _adaln(c, w, b):
    nb = c.shape[0]
    n = w.shape[1]
    bn = 512
    return pl.pallas_call(
        _adaln_body,
        grid=(n // bn,),
        in_specs=[pl.BlockSpec((nb, D_MODEL), lambda j: (0, 0)),
                  pl.BlockSpec((D_MODEL, bn), lambda j: (0, j)),
                  pl.BlockSpec((1, bn), lambda j: (0, j))],
        out_specs=pl.BlockSpec((nb, bn), lambda j: (0, j)),
        out_shape=jax.ShapeDtypeStruct((nb, n), F32),
        name="adaln",
    )(c, w, b)


def _proj_body(x_ref, mod_ref, w_ref, qg_ref, kg_ref, ig_ref,
               u_ref, q_ref, k_ref, v_ref, qih_ref, qil_ref, ki_ref, wi_ref, ga_ref, gb_ref):
    x = x_ref[...]
    hn = _rms(x, D_MODEL)
    hb = _modulate(hn, mod_ref[...], 1, 0).astype(BF16)

    u_ref[...] = _dot(hb, w_ref[:, C_U:C_Q])

    q = _dot(hb, w_ref[:, C_Q:C_K])
    heads = [_rms(q[:, LANE * h:LANE * (h + 1)], HEAD_DIM) for h in range(N_HEADS)]
    q_ref[...] = (jnp.concatenate(heads, axis=-1) * qg_ref[...]).astype(BF16)

    k = _dot(hb, w_ref[:, C_K:C_V])[:, :HEAD_DIM]
    k_ref[...] = _rms(k, HEAD_DIM) * kg_ref[...]
    v_ref[...] = _dot(hb, w_ref[:, C_V:C_QI])[:, :HEAD_DIM]

    qi = _dot(hb, w_ref[:, C_QI:C_KI])
    qi_hi, qi_lo = _split_bf16(qi)
    qih_ref[...] = qi_hi
    qil_ref[...] = qi_lo

    ki = _dot(hb, w_ref[:, C_KI:C_WI])[:, :IDX_DIM]
    ki_ref[...] = _rms(ki, IDX_DIM) * ig_ref[...]
    wi_ref[...] = _dot(hb, w_ref[:, C_WI:C_GA])[:, :IDX_HEADS] * IDX_WEIGHT_SCALE

    ga_ref[...] = jax.nn.sigmoid(_dot(hb, w_ref[:, C_GA:C_GB]))
    gb_ref[...] = jax.nn.sigmoid(_dot(hb, w_ref[:, C_GB:C_END]))


def _proj(xf, mod3, w_pad, qg, kg, ig, t_len, tm):
    n = xf.shape[0]
    row = lambda w: pl.BlockSpec((tm, w), lambda i: (i, 0))
    const = lambda s: pl.BlockSpec(s, lambda i: (0, 0))
    widths = [(SSM_WIDTH, F32), (N_HEADS * LANE, BF16), (HEAD_DIM, F32), (HEAD_DIM, F32),
              (IDX_HEADS * IDX_DIM, BF16), (IDX_HEADS * IDX_DIM, BF16), (IDX_DIM, F32),
              (IDX_HEADS, F32), (D_MODEL, F32), (D_MODEL, F32)]
    return pl.pallas_call(
        _proj_body,
        grid=(n // tm,),
        in_specs=[row(D_MODEL), _mod_spec(tm, t_len), const((D_MODEL, C_END)),
                  const((1, N_HEADS * LANE)), const((1, HEAD_DIM)), const((1, IDX_DIM))],
        out_specs=[row(w) for w, _ in widths],
        out_shape=[jax.ShapeDtypeStruct((n, w), dt) for w, dt in widths],
        compiler_params=pltpu.CompilerParams(dimension_semantics=("parallel",),
                                             vmem_limit_bytes=VMEM_LIMIT),
        name="proj",
    )(xf, mod3, w_pad, qg, kg, ig)


def _s5_body(u_ref, h0r_ref, h0i_ref, bre_ref, bim_ref, cre_ref, cim_ref, lre_ref, lim_ref, d_ref,
             y_ref, sr_ref, si_ref, xr, xi, hr, hi, *, nb, tc, lane_chunk):
    @pl.when(pl.program_id(0) == 0)
    def _():
        hr[...] = h0r_ref[...]
        hi[...] = h0i_ref[...]

    u = u_ref[...]
    ub = u.astype(BF16)
    xr[...] = _dot(ub, bre_ref[...])
    xi[...] = _dot(ub, bim_ref[...])

    for lc in range(SSM_LANES // lane_chunk):
        cols = slice(lc * lane_chunk, (lc + 1) * lane_chunk)

        def step(t, carry, cols=cols):
            a, b = carry
            rows = pl.ds(pl.multiple_of(t * nb, nb), nb)
            lr = lre_ref[:, cols]
            li = lim_ref[:, cols]
            na = lr * a - li * b + xr[rows, cols]
            nb_ = lr * b + li * a + xi[rows, cols]
            xr[rows, cols] = na
            xi[rows, cols] = nb_
            return na, nb_

        a, b = lax.fori_loop(0, tc, step, (hr[:, cols], hi[:, cols]), unroll=2)
        hr[:, cols] = a
        hi[:, cols] = b

    y = _dot(xr[...].astype(BF16), cre_ref[...]) + _dot(xi[...].astype(BF16), cim_ref[...])
    y_ref[...] = y + d_ref[...] * u
    sr_ref[...] = hr[...]
    si_ref[...] = hi[...]


def _s5(u_tm, h0r, h0i, ssm, nb, t_len, tc):
    bre, bim, cre, cim, lre, lim, dsk = ssm
    rows = tc * nb
    const = lambda s: pl.BlockSpec(s, lambda c: (0, 0))
    return pl.pallas_call(
        functools.partial(_s5_body, nb=nb, tc=tc, lane_chunk=512),
        grid=(t_len // tc,),
        in_specs=[pl.BlockSpec((rows, SSM_WIDTH), lambda c: (c, 0)),
                  const((nb, SSM_LANES)), const((nb, SSM_LANES)),
                  const((SSM_WIDTH, SSM_LANES)), const((SSM_WIDTH, SSM_LANES)),
                  const((SSM_LANES, SSM_WIDTH)), const((SSM_LANES, SSM_WIDTH)),
                  const((nb, SSM_LANES)), const((nb, SSM_LANES)), const((1, SSM_WIDTH))],
        out_specs=[pl.BlockSpec((rows, SSM_WIDTH), lambda c: (c, 0)),
                   const((nb, SSM_LANES)), const((nb, SSM_LANES))],
        out_shape=[jax.ShapeDtypeStruct((t_len * nb, SSM_WIDTH), F32),
                   jax.ShapeDtypeStruct((nb, SSM_LANES), F32),
                   jax.ShapeDtypeStruct((nb, SSM_LANES), F32)],
        scratch_shapes=[pltpu.VMEM((rows, SSM_LANES), F32), pltpu.VMEM((rows, SSM_LANES), F32),
                        pltpu.VMEM((nb, SSM_LANES), F32), pltpu.VMEM((nb, SSM_LANES), F32)],
        compiler_params=pltpu.CompilerParams(dimension_semantics=("arbitrary",),
                                             vmem_limit_bytes=VMEM_LIMIT),
        name="s5",
    )(u_tm, h0r, h0i, bre, bim, cre, cim, lre[:nb], lim[:nb], dsk)


def _dsa_body(q_ref, qc_ref, wt_ref, kc_ref, k_ref, vt_ref, wo_ref, o_ref,
              key_ref, bias_ref, lg_ref, at_ref, *, tq, kb, s_real, s_pad, q_off, q_wrap, topk):
    j = pl.program_id(1)
    lane_q = lax.broadcasted_iota(I32, (1, tq), 1)
    qpos = q_off + j * tq + (lane_q & (q_wrap - 1))
    qchunk = qpos >> CHUNK_SHIFT
    last_end = (((q_off + j * tq + jnp.minimum(tq, q_wrap) - 1) >> CHUNK_SHIFT) + 1) << CHUNK_SHIFT
    nkb = jnp.minimum(s_pad // kb, lax.div(last_end + (kb - 1), kb))

    def blk(b):
        return pl.ds(pl.multiple_of(b * kb, kb), kb)

    def kpos_of(b):
        return b * kb + lax.broadcasted_iota(I32, (kb, tq), 0)

    qc = qc_ref[0]
    wt = wt_ref[0]

    def score_blk(b, carry):
        kc = kc_ref[0, blk(b), :]
        sc = jnp.zeros((kb, tq), F32)
        for h in range(IDX_HEADS):
            d = _dot_nt(kc, qc[:, LANE * h:LANE * (h + 1)])
            sc = sc + wt[h:h + 1, :] * jnp.maximum(d, 0.0)
        kpos = kpos_of(b)
        adm = ((kpos >> CHUNK_SHIFT) <= qchunk) & (kpos < s_real)
        bits = lax.bitcast_convert_type(sc, I32)
        key = bits ^ ((bits >> 31) & 0x7FFFFFFF)
        key = jnp.where(sc == 0.0, 0, key)
        key_ref[blk(b), :] = jnp.where(adm, key, INT_MIN)
        return carry

    lax.fori_loop(0, nkb, score_blk, 0)

    def count(pred):
        def body(b, acc):
            return acc + jnp.sum(jnp.where(pred(key_ref[blk(b), :], b), 1.0, 0.0),
                                 axis=0, keepdims=True)
        return lax.fori_loop(0, nkb, body, jnp.zeros((1, tq), F32))

    def bit_body(i, prefix):
        cand = prefix | (jnp.int32(1) << (31 - i))
        cand_s = cand ^ INT_MIN
        cnt = count(lambda kk, b: kk >= cand_s)
        return jnp.where(cnt >= topk, cand, prefix)

    prefix = lax.fori_loop(0, 32, bit_body, jnp.zeros((1, tq), I32))
    thr = jnp.maximum(prefix ^ INT_MIN, INT_MIN + 1)

    n_gt = count(lambda kk, b: kk > thr)
    n_eq = count(lambda kk, b: kk == thr)
    need = topk - n_gt
    any_over = jnp.max(n_eq - need) > 0.0

    idx_bits = int(s_pad).bit_length()

    def tie_search():
        def tbody(i, lim):
            cand = lim | (jnp.int32(1) << (idx_bits - 1 - i))
            n = count(lambda kk, b: (kk == thr) & (kpos_of(b) < cand))
            return jnp.where(n <= need, cand, lim)
        return lax.fori_loop(0, idx_bits, tbody, jnp.zeros((1, tq), I32))

    lim = lax.cond(any_over, tie_search, lambda: jnp.full((1, tq), 2 ** idx_bits - 1, I32))

    def bias_blk(b, carry):
        kk = key_ref[blk(b), :]
        sel = (kk > thr) | ((kk == thr) & (kpos_of(b) < lim))
        bias_ref[blk(b), :] = jnp.where(sel, 0.0, NEG_INF)
        return carry

    lax.fori_loop(0, nkb, bias_blk, 0)

    for h in range(N_HEADS):
        qh = q_ref[0, :, LANE * h:LANE * (h + 1)]

        def logits_blk(b, m, qh=qh):
            l = _dot_nt(k_ref[0, blk(b), :], qh) + bias_ref[blk(b), :]
            lg_ref[blk(b), :] = l
            return jnp.maximum(m, jnp.max(l, axis=0, keepdims=True))

        m = lax.fori_loop(0, nkb, logits_blk, jnp.full((1, tq), NEG_INF, F32))

        def pv_blk(b, carry, m=m):
            ssum, acc = carry
            p = jnp.exp(lg_ref[blk(b), :] - m)
            ssum = ssum + jnp.sum(p, axis=0, keepdims=True)
            acc = acc + _dot(vt_ref[0, b], p.astype(BF16))
            return ssum, acc

        ssum, acc = lax.fori_loop(0, nkb, pv_blk,
                                  (jnp.zeros((1, tq), F32), jnp.zeros((HEAD_DIM, tq), F32)))
        at_ref[HEAD_DIM * h:HEAD_DIM * (h + 1), :] = acc / ssum

    attn = at_ref[...].T.astype(BF16)
    o_ref[0] = _dot(attn, wo_ref[...])


def _dsa(q, qc, wt, kc, kp, vt, wo, *, tq, kb, s_real, q_off, q_wrap):
    nb, t_len, _ = q.shape
    s_pad = kc.shape[1]
    topk = min(TOPK_MAX, s_real // 4)
    qspec = lambda w: pl.BlockSpec((1, tq, w), lambda b, j: (b, j, 0))
    return pl.pallas_call(
        functools.partial(_dsa_body, tq=tq, kb=kb, s_real=s_real, s_pad=s_pad, q_off=q_off,
                          q_wrap=q_wrap, topk=float(topk)),
        grid=(nb, t_len // tq),
        in_specs=[qspec(N_HEADS * LANE), qspec(IDX_HEADS * LANE),
                  pl.BlockSpec((1, IDX_HEADS, tq), lambda b, j: (b, 0, j)),
                  pl.BlockSpec((1, s_pad, LANE), lambda b, j: (b, 0, 0)),
                  pl.BlockSpec((1, s_pad, LANE), lambda b, j: (b, 0, 0)),
                  pl.BlockSpec((1, s_pad // kb, HEAD_DIM, kb), lambda b, j: (b, 0, 0, 0)),
                  pl.BlockSpec((N_HEADS * HEAD_DIM, D_MODEL), lambda b, j: (0, 0))],
        out_specs=qspec(D_MODEL),
        out_shape=jax.ShapeDtypeStruct((nb, t_len, D_MODEL), F32),
        scratch_shapes=[pltpu.VMEM((s_pad, tq), I32), pltpu.VMEM((s_pad, tq), F32),
                        pltpu.VMEM((s_pad, tq), F32), pltpu.VMEM((N_HEADS * HEAD_DIM, tq), F32)],
        compiler_params=pltpu.CompilerParams(dimension_semantics=("parallel", "parallel"),
                                             vmem_limit_bytes=VMEM_LIMIT),
        name="dsa",
    )(q, qc, wt, kc, kp, vt, wo)


def _merge_body(x_ref, y_ref, ga_ref, gb_ref, ob_ref, mod_ref, wglu_ref, wout_ref, o_ref):
    z = jax.nn.gelu(y_ref[...], approximate=True).astype(BF16)
    glu = _dot(z, wglu_ref[...])
    out_a = glu[:, :D_MODEL] * jax.nn.sigmoid(glu[:, D_MODEL:])
    merged = ga_ref[...] * out_a + gb_ref[...] * ob_ref[...]
    upd = _dot(merged.astype(BF16), wout_ref[...])
    o_ref[...] = x_ref[...] + _gate_mul(upd, mod_ref[...], 2)


def _merge(xf, y, ga, gb, ob, mod3, wglu, wout, t_len, tm):
    n = xf.shape[0]
    row = lambda w: pl.BlockSpec((tm, w), lambda i: (i, 0))
    const = lambda s: pl.BlockSpec(s, lambda i: (0, 0))
    return pl.pallas_call(
        _merge_body,
        grid=(n // tm,),
        in_specs=[row(D_MODEL), row(SSM_WIDTH), row(D_MODEL), row(D_MODEL), row(D_MODEL),
                  _mod_spec(tm, t_len), const((SSM_WIDTH, 2 * D_MODEL)), const((D_MODEL, D_MODEL))],
        out_specs=row(D_MODEL),
        out_shape=jax.ShapeDtypeStruct((n, D_MODEL), F32),
        compiler_params=pltpu.CompilerParams(dimension_semantics=("parallel",),
                                             vmem_limit_bytes=VMEM_LIMIT),
        name="merge",
    )(xf, y, ga, gb, ob, mod3, wglu, wout)


def _moe_body(x_ref, mod_ref, wr_ref, br_ref, wgu_ref, wd_ref, o_ref, hb_ref, comb_ref, acc_ref):
    e = pl.program_id(1)
    tm = x_ref.shape[0]
    lane = lax.broadcasted_iota(I32, (tm, LANE), 1).astype(F32)

    @pl.when(e == 0)
    def _():
        h2 = _modulate(_rms(x_ref[...], D_MODEL), mod_ref[...], 4, 3)
        hb_ref[...] = h2.astype(BF16)
        lg = _dot_hilo(h2, wr_ref[...]) + br_ref[...]
        neg = -jnp.inf
        gl = jnp.where(lane < N_GROUPS, lg, neg)
        gmax = jnp.max(gl, axis=-1, keepdims=True)
        g_idx = jnp.min(jnp.where(gl == gmax, lane, float(LANE)), axis=-1, keepdims=True)
        g_w = 1.0 / jnp.sum(jnp.where(lane < N_GROUPS, jnp.exp(lg - gmax), 0.0),
                            axis=-1, keepdims=True)
        lo = N_GROUPS + EXPERTS_PER_GROUP * g_idx
        el = jnp.where((lane >= lo) & (lane < lo + EXPERTS_PER_GROUP), lg, neg)
        v1 = jnp.max(el, axis=-1, keepdims=True)
        i1 = jnp.min(jnp.where(el == v1, lane, float(LANE)), axis=-1, keepdims=True)
        el2 = jnp.where(lane == i1, neg, el)
        v2 = jnp.max(el2, axis=-1, keepdims=True)
        i2 = jnp.min(jnp.where(el2 == v2, lane, float(LANE)), axis=-1, keepdims=True)
        e21 = jnp.exp(v2 - v1)
        w1 = g_w / (1.0 + e21)
        w2 = g_w * e21 / (1.0 + e21)
        comb_ref[...] = jnp.where(lane == i1, w1, 0.0) + jnp.where(lane == i2, w2, 0.0)
        acc_ref[...] = jnp.zeros_like(acc_ref)

    gu = _dot(hb_ref[...], wgu_ref[0])
    g = gu[:, :EXPERT_DIM]
    hid = g * jax.nn.sigmoid(g) * gu[:, EXPERT_DIM:]
    col = jnp.sum(jnp.where(lane == (e + N_GROUPS).astype(F32), comb_ref[...], 0.0),
                  axis=-1, keepdims=True)
    acc_ref[...] += _dot((hid * col).astype(BF16), wd_ref[0])

    @pl.when(e == N_EXPERTS - 1)
    def _():
        o_ref[...] = x_ref[...] + _gate_mul(acc_ref[...], mod_ref[...], 5)


def _moe(x1, mod3, wr, br, wgu, wd, t_len, tm):
    n = x1.shape[0]
    return pl.pallas_call(
        _moe_body,
        grid=(n // tm, N_EXPERTS),
        in_specs=[pl.BlockSpec((tm, D_MODEL), lambda i, e: (i, 0)),
                  _mod_spec(tm, t_len),
                  pl.BlockSpec((D_MODEL, LANE), lambda i, e: (0, 0)),
                  pl.BlockSpec((1, LANE), lambda i, e: (0, 0)),
                  pl.BlockSpec((1, D_MODEL, 2 * EXPERT_DIM), lambda i, e: (e, 0, 0)),
                  pl.BlockSpec((1, EXPERT_DIM, D_MODEL), lambda i, e: (e, 0, 0))],
        out_specs=pl.BlockSpec((tm, D_MODEL), lambda i, e: (i, 0)),
        out_shape=jax.ShapeDtypeStruct((n, D_MODEL), F32),
        scratch_shapes=[pltpu.VMEM((tm, D_MODEL), BF16), pltpu.VMEM((tm, LANE), F32),
                        pltpu.VMEM((tm, D_MODEL), F32)],
        compiler_params=pltpu.CompilerParams(dimension_semantics=("parallel", "arbitrary"),
                                             vmem_limit_bytes=VMEM_LIMIT),
        name="moe",
    )(x1, mod3, wr, br, wgu, wd)


def _prep_params(w_in, ssm_a_re, ssm_a_im, ssm_log_dt, ssm_b_re, ssm_b_im, ssm_c_re, ssm_c_im, ssm_d,
                 w_glu, q_gain, k_gain, kidx_gain, w_attn_out, w_out, w_rg, b_rg, w_re, b_re,
                 w_gate, w_up, w_down, max_batch):
    offs = np.cumsum([0, SSM_WIDTH, N_HEADS * HEAD_DIM, HEAD_DIM, HEAD_DIM, IDX_HEADS * IDX_DIM,
                      IDX_DIM, IDX_HEADS, D_MODEL, D_MODEL])
    seg = lambda i: w_in[:, offs[i]:offs[i + 1]]
    padto = lambda a, w: jnp.pad(a, ((0, 0), (0, w - a.shape[1])))
    wq = jnp.pad(seg(1).reshape(D_MODEL, N_HEADS, HEAD_DIM), ((0, 0), (0, 0), (0, LANE - HEAD_DIM)))
    w_pad = jnp.concatenate(
        [seg(0), wq.reshape(D_MODEL, N_HEADS * LANE), padto(seg(2), LANE), padto(seg(3), LANE),
         seg(4), padto(seg(5), LANE), padto(seg(6), LANE), seg(7), seg(8)], axis=1).astype(BF16)
    qg = jnp.tile(jnp.pad(q_gain * HEAD_DIM ** -0.5, (0, LANE - HEAD_DIM)), N_HEADS)[None, :]

    dt = jnp.exp(ssm_log_dt)[:, None]
    decay = jnp.exp(ssm_a_re * dt)
    lam_re = decay * jnp.cos(ssm_a_im * dt)
    lam_im = decay * jnp.sin(ssm_a_im * dt)
    den = ssm_a_re * ssm_a_re + ssm_a_im * ssm_a_im
    num_re = lam_re - 1.0
    coef_re = (num_re * ssm_a_re + lam_im * ssm_a_im) / den
    coef_im = (lam_im * ssm_a_re - num_re * ssm_a_im) / den
    bb_re = coef_re[..., None] * ssm_b_re - coef_im[..., None] * ssm_b_im
    bb_im = coef_re[..., None] * ssm_b_im + coef_im[..., None] * ssm_b_re
    eye = jnp.eye(SSM_GROUPS, dtype=F32)

    def in_map(bb):
        return jnp.einsum('gpj,gh->gjhp', bb, eye).reshape(SSM_WIDTH, SSM_LANES).astype(BF16)

    def out_map(c):
        return jnp.einsum('gjp,gh->gphj', c, eye).reshape(SSM_LANES, SSM_WIDTH).astype(BF16)

    bcast = lambda a: jnp.broadcast_to(a.reshape(1, SSM_LANES), (max_batch, SSM_LANES))
    ssm = (in_map(bb_re), in_map(bb_im), out_map(ssm_c_re), out_map(-ssm_c_im),
           bcast(lam_re), bcast(lam_im), ssm_d[None, :])

    wr = jnp.pad(jnp.concatenate([w_rg, w_re], axis=1), ((0, 0), (0, LANE - N_GROUPS - N_EXPERTS)))
    br = jnp.pad(jnp.concatenate([b_rg, b_re]), (0, LANE - N_GROUPS - N_EXPERTS))[None, :]
    wgu = jnp.concatenate([w_gate, w_up], axis=-1).astype(BF16)
    return dict(w_pad=w_pad, qg=qg, kg=k_gain[None, :], ig=kidx_gain[None, :], ssm=ssm,
                wglu=w_glu.astype(BF16), wo=w_attn_out.astype(BF16), wout=w_out.astype(BF16),
                wr=wr, br=br, wgu=wgu, wd=w_down.astype(BF16))


def _layer(x, mod3, p, past, *, tm_tok, tm_moe, tc, tq, kb):
    nb, t_len, _ = x.shape
    n = nb * t_len
    xf = x.reshape(n, D_MODEL)
    u, q, k, v, qih, qil, ki, wi, ga, gb = _proj(xf, mod3, p["w_pad"], p["qg"], p["kg"], p["ig"],
                                                 t_len, tm_tok)
    k3 = k.reshape(nb, t_len, HEAD_DIM)
    v3 = v.reshape(nb, t_len, HEAD_DIM)
    ki3 = ki.reshape(nb, t_len, IDX_DIM)

    u_tm = u.reshape(nb, t_len, SSM_WIDTH).transpose(1, 0, 2).reshape(n, SSM_WIDTH)
    if past is None:
        h0r = jnp.zeros((nb, SSM_LANES), F32)
        h0i = h0r
        k_all, v_all, ki_all, q_off = k3, v3, ki3, 0
    else:
        past_k, past_v, past_ki, h0_re, h0_im = past
        h0r = h0_re.reshape(nb, SSM_LANES)
        h0i = h0_im.reshape(nb, SSM_LANES)
        k_all = jnp.concatenate([past_k, k3], axis=1)
        v_all = jnp.concatenate([past_v, v3], axis=1)
        ki_all = jnp.concatenate([past_ki, ki3], axis=1)
        q_off = past_k.shape[1]
    y_tm, s_re, s_im = _s5(u_tm, h0r, h0i, p["ssm"], nb, t_len, tc)
    y = y_tm.reshape(t_len, nb, SSM_WIDTH).transpose(1, 0, 2).reshape(n, SSM_WIDTH)

    s_real = k_all.shape[1]
    s_pad = -(-s_real // kb) * kb
    pad_s = lambda a: jnp.pad(a, ((0, 0), (0, s_pad - s_real), (0, 0)))
    kp = jnp.pad(pad_s(k_all).astype(BF16), ((0, 0), (0, 0), (0, LANE - HEAD_DIM)))
    vt = pad_s(v_all).astype(BF16).reshape(nb, s_pad // kb, kb, HEAD_DIM).transpose(0, 1, 3, 2)
    ki_hi, ki_lo = _split_bf16(pad_s(ki_all))
    kc = jnp.concatenate([ki_hi, ki_lo, ki_hi, ki_lo], axis=-1)
    qih3 = qih.reshape(n, IDX_HEADS, IDX_DIM)
    qil3 = qil.reshape(n, IDX_HEADS, IDX_DIM)
    qc = jnp.concatenate([qih3, qih3, qil3, qil3], axis=-1).reshape(nb, t_len, IDX_HEADS * LANE)
    wt = wi.reshape(nb, t_len, IDX_HEADS).transpose(0, 2, 1)
    q3 = q.reshape(nb, t_len, N_HEADS * LANE)
    reps = tq // t_len if t_len < tq else 1
    if reps > 1:
        q3 = jnp.tile(q3, (1, reps, 1))
        qc = jnp.tile(qc, (1, reps, 1))
        wt = jnp.tile(wt, (1, 1, reps))
    ob = _dsa(q3, qc, wt, kc, kp, vt, p["wo"], tq=tq, kb=kb, s_real=s_real, q_off=q_off,
              q_wrap=t_len if reps > 1 else tq)
    ob = ob[:, :t_len].reshape(n, D_MODEL)

    x1 = _merge(xf, y, ga, gb, ob, mod3, p["wglu"], p["wout"], t_len, tm_tok)
    x2 = _moe(x1, mod3, p["wr"], p["br"], p["wgu"], p["wd"], t_len, tm_moe)
    return (x2.reshape(nb, t_len, D_MODEL), k3, v3, ki3,
            s_re.reshape(nb, SSM_GROUPS, SSM_STATE), s_im.reshape(nb, SSM_GROUPS, SSM_STATE))


def kernel(x_prompt, x_sample, cache_k, cache_v, cache_kidx, state_ssm_re, state_ssm_im, c_prompt, c_sample, w_ada, b_ada, w_in, ssm_a_re, ssm_a_im, ssm_log_dt, ssm_b_re, ssm_b_im, ssm_c_re, ssm_c_im, ssm_d, w_glu, q_gain, k_gain, kidx_gain, w_attn_out, w_out, w_route_group, b_route_group, w_route_expert, b_route_expert, w_gate, w_up, w_down):
    depth = w_ada.shape[0]
    nbp = x_prompt.shape[0]
    nbs = x_sample.shape[0]
    xp, xs = x_prompt, x_sample
    outs = [[] for _ in range(10)]
    for l in range(depth):
        p = _prep_params(w_in[l], ssm_a_re[l], ssm_a_im[l], ssm_log_dt[l], ssm_b_re[l], ssm_b_im[l],
                         ssm_c_re[l], ssm_c_im[l], ssm_d[l], w_glu[l], q_gain[l], k_gain[l],
                         kidx_gain[l], w_attn_out[l], w_out[l], w_route_group[l], b_route_group[l],
                         w_route_expert[l], b_route_expert[l], w_gate[l], w_up[l], w_down[l],
                         max(nbp, nbs))
        mod = _adaln(jnp.concatenate([c_prompt, c_sample], axis=0), w_ada[l], b_ada[l][None, :])
        mod3 = mod[:, None, :]
        xp, kp, vp, kip, srp, sip = _layer(xp, mod3[:nbp], p, None,
                                           tm_tok=512, tm_moe=1024, tc=32, tq=128, kb=256)
        past = (cache_k[l], cache_v[l], cache_kidx[l], state_ssm_re[l], state_ssm_im[l])
        xs, ks, vs, kis, srs, sis = _layer(xs, mod3[nbp:], p, past,
                                           tm_tok=512, tm_moe=512, tc=64, tq=128, kb=384)
        for lst, val in zip(outs, (kp, vp, kip, srp, sip, ks, vs, kis, srs, sis)):
            lst.append(val)
    return (xp, xs) + tuple(jnp.stack(o) for o in outs)
```

```python
import functools

import jax
import jax.numpy as jnp
import numpy as np
from jax import lax
from jax.experimental import pallas as pl
from jax.experimental.pallas import tpu as pltpu

F32 = jnp.float32
BF16 = jnp.bfloat16
I32 = jnp.int32

D_MODEL = 1024
CHUNK_SHIFT = 6
EPS = 1e-6
NEG_INF = -1e30
SSM_WIDTH = 512
SSM_GROUP = 16
SSM_GROUPS = 32
SSM_STATE = 64
SSM_LANES = SSM_GROUPS * SSM_STATE
N_HEADS = 8
HEAD_DIM = 64
IDX_HEADS = 8
IDX_DIM = 32
IDX_WEIGHT_SCALE = (IDX_HEADS * IDX_DIM) ** -0.5
TOPK_MAX = 256
N_GROUPS = 4
EXPERTS_PER_GROUP = 4
N_EXPERTS = 16
EXPERT_DIM = 256
LANE = 128
INT_MIN = -(2 ** 31)
VMEM_LIMIT = 56 * 1024 * 1024

C_U = 0
C_Q = C_U + SSM_WIDTH
C_K = C_Q + N_HEADS * LANE
C_V = C_K + LANE
C_QI = C_V + LANE
C_KI = C_QI + IDX_HEADS * IDX_DIM
C_WI = C_KI + LANE
C_GA = C_WI + LANE
C_GB = C_GA + D_MODEL
C_END = C_GB + D_MODEL


def _dot(a, b):
    return jnp.dot(a, b, preferred_element_type=F32)


def _dot_nt(a, b):
    return lax.dot_general(a, b, (((1,), (1,)), ((), ())), preferred_element_type=F32)


def _split_bf16(x):
    hi = x.astype(BF16)
    lo = (x - hi.astype(F32)).astype(BF16)
    return hi, lo


def _dot_hilo(a, b):
    ah, al = _split_bf16(a)
    bh, bl = _split_bf16(b)
    return _dot(ah, bh) + _dot(ah, bl) + _dot(al, bh)


def _mod_spec(tm, t_len):
    if tm >= t_len:
        return pl.BlockSpec((tm // t_len, 1, 6 * D_MODEL), lambda i, *_: (i, 0, 0))
    per = t_len // tm
    return pl.BlockSpec((1, 1, 6 * D_MODEL), lambda i, *_: (i // per, 0, 0))


def _modulate(x, mod, scale_seg, shift_seg):
    nb = mod.shape[0]
    tm = x.shape[0]
    sc = mod[:, :, scale_seg * D_MODEL:(scale_seg + 1) * D_MODEL]
    sh = mod[:, :, shift_seg * D_MODEL:(shift_seg + 1) * D_MODEL]
    x3 = x.reshape(nb, tm // nb, D_MODEL)
    return (x3 * (1.0 + sc) + sh).reshape(tm, D_MODEL)


def _gate_mul(x, mod, seg):
    nb = mod.shape[0]
    tm = x.shape[0]
    g = mod[:, :, seg * D_MODEL:(seg + 1) * D_MODEL]
    return (x.reshape(nb, tm // nb, D_MODEL) * g).reshape(tm, D_MODEL)


def _rms(x, n):
    return x * lax.rsqrt(jnp.sum(x * x, axis=-1, keepdims=True) * (1.0 / n) + EPS)


def _adaln_body(c_ref, w_ref, b_ref, o_ref):
    c = c_ref[...]
    s = c * jax.nn.sigmoid(c)
    o_ref[...] = _dot_hilo(s, w_ref[...]) + b_ref[...]


def _adaln(c, w, b):
    nb = c.shape[0]
    n = w.shape[1]
    bn = 512
    return pl.pallas_call(
        _adaln_body,
        grid=(n // bn,),
        in_specs=[pl.BlockSpec((nb, D_MODEL), lambda j: (0, 0)),
                  pl.BlockSpec((D_MODEL, bn), lambda j: (0, j)),
                  pl.BlockSpec((1, bn), lambda j: (0, j))],
        out_specs=pl.BlockSpec((nb, bn), lambda j: (0, j)),
        out_shape=jax.ShapeDtypeStruct((nb, n), F32),
        name="adaln",
    )(c, w, b)


def _proj_body(x_ref, mod_ref, w_ref, qg_ref, kg_ref, ig_ref,
               u_ref, q_ref, k_ref, v_ref, qih_ref, qil_ref, ki_ref, wi_ref, ga_ref, gb_ref):
    x = x_ref[...]
    hn = _rms(x, D_MODEL)
    hb = _modulate(hn, mod_ref[...], 1, 0).astype(BF16)

    u_ref[...] = _dot(hb, w_ref[:, C_U:C_Q])

    q = _dot(hb, w_ref[:, C_Q:C_K])
    heads = [_rms(q[:, LANE * h:LANE * (h + 1)], HEAD_DIM) for h in range(N_HEADS)]
    q_ref[...] = (jnp.concatenate(heads, axis=-1) * qg_ref[...]).astype(BF16)

    k = _dot(hb, w_ref[:, C_K:C_V])[:, :HEAD_DIM]
    k_ref[...] = _rms(k, HEAD_DIM) * kg_ref[...]
    v_ref[...] = _dot(hb, w_ref[:, C_V:C_QI])[:, :HEAD_DIM]

    qi = _dot(hb, w_ref[:, C_QI:C_KI])
    qi_hi, qi_lo = _split_bf16(qi)
    qih_ref[...] = qi_hi
    qil_ref[...] = qi_lo

    ki = _dot(hb, w_ref[:, C_KI:C_WI])[:, :IDX_DIM]
    ki_ref[...] = _rms(ki, IDX_DIM) * ig_ref[...]
    wi_ref[...] = _dot(hb, w_ref[:, C_WI:C_GA])[:, :IDX_HEADS] * IDX_WEIGHT_SCALE

    ga_ref[...] = jax.nn.sigmoid(_dot(hb, w_ref[:, C_GA:C_GB]))
    gb_ref[...] = jax.nn.sigmoid(_dot(hb, w_ref[:, C_GB:C_END]))


def _proj(xf, mod3, w_pad, qg, kg, ig, t_len, tm):
    n = xf.shape[0]
    row = lambda w: pl.BlockSpec((tm, w), lambda i: (i, 0))
    const = lambda s: pl.BlockSpec(s, lambda i: (0, 0))
    widths = [(SSM_WIDTH, F32), (N_HEADS * LANE, BF16), (HEAD_DIM, F32), (HEAD_DIM, F32),
              (IDX_HEADS * IDX_DIM, BF16), (IDX_HEADS * IDX_DIM, BF16), (IDX_DIM, F32),
              (IDX_HEADS, F32), (D_MODEL, F32), (D_MODEL, F32)]
    return pl.pallas_call(
        _proj_body,
        grid=(n // tm,),
        in_specs=[row(D_MODEL), _mod_spec(tm, t_len), const((D_MODEL, C_END)),
                  const((1, N_HEADS * LANE)), const((1, HEAD_DIM)), const((1, IDX_DIM))],
        out_specs=[row(w) for w, _ in widths],
        out_shape=[jax.ShapeDtypeStruct((n, w), dt) for w, dt in widths],
        compiler_params=pltpu.CompilerParams(dimension_semantics=("parallel",),
                                             vmem_limit_bytes=VMEM_LIMIT),
        name="proj",
    )(xf, mod3, w_pad, qg, kg, ig)


def _s5_body(u_ref, h0r_ref, h0i_ref, bre_ref, bim_ref, cre_ref, cim_ref, lre_ref, lim_ref, d_ref,
             y_ref, sr_ref, si_ref, xr, xi, hr, hi, *, nb, tc, lane_chunk):
    @pl.when(pl.program_id(0) == 0)
    def _():
        hr[...] = h0r_ref[...]
        hi[...] = h0i_ref[...]

    u = u_ref[...]
    ub = u.astype(BF16)
    xr[...] = _dot(ub, bre_ref[...])
    xi[...] = _dot(ub, bim_ref[...])

    for lc in range(SSM_LANES // lane_chunk):
        cols = slice(lc * lane_chunk, (lc + 1) * lane_chunk)

        def step(t, carry, cols=cols):
            a, b = carry
            rows = pl.ds(pl.multiple_of(t * nb, nb), nb)
            lr = lre_ref[:, cols]
            li = lim_ref[:, cols]
            na = lr * a - li * b + xr[rows, cols]
            nb_ = lr * b + li * a + xi[rows, cols]
            xr[rows, cols] = na
            xi[rows, cols] = nb_
            return na, nb_

        a, b = lax.fori_loop(0, tc, step, (hr[:, cols], hi[:, cols]), unroll=2)
        hr[:, cols] = a
        hi[:, cols] = b

    y = _dot(xr[...].astype(BF16), cre_ref[...]) + _dot(xi[...].astype(BF16), cim_ref[...])
    y_ref[...] = y + d_ref[...] * u
    sr_ref[...] = hr[...]
    si_ref[...] = hi[...]


def _s5(u_tm, h0r, h0i, ssm, nb, t_len, tc):
    bre, bim, cre, cim, lre, lim, dsk = ssm
    rows = tc * nb
    const = lambda s: pl.BlockSpec(s, lambda c: (0, 0))
    return pl.pallas_call(
        functools.partial(_s5_body, nb=nb, tc=tc, lane_chunk=512),
        grid=(t_len // tc,),
        in_specs=[pl.BlockSpec((rows, SSM_WIDTH), lambda c: (c, 0)),
                  const((nb, SSM_LANES)), const((nb, SSM_LANES)),
                  const((SSM_WIDTH, SSM_LANES)), const((SSM_WIDTH, SSM_LANES)),
                  const((SSM_LANES, SSM_WIDTH)), const((SSM_LANES, SSM_WIDTH)),
                  const((nb, SSM_LANES)), const((nb, SSM_LANES)), const((1, SSM_WIDTH))],
        out_specs=[pl.BlockSpec((rows, SSM_WIDTH), lambda c: (c, 0)),
                   const((nb, SSM_LANES)), const((nb, SSM_LANES))],
        out_shape=[jax.ShapeDtypeStruct((t_len * nb, SSM_WIDTH), F32),
                   jax.ShapeDtypeStruct((nb, SSM_LANES), F32),
                   jax.ShapeDtypeStruct((nb, SSM_LANES), F32)],
        scratch_shapes=[pltpu.VMEM((rows, SSM_LANES), F32), pltpu.VMEM((rows, SSM_LANES), F32),
                        pltpu.VMEM((nb, SSM_LANES), F32), pltpu.VMEM((nb, SSM_LANES), F32)],
        compiler_params=pltpu.CompilerParams(dimension_semantics=("arbitrary",),
                                             vmem_limit_bytes=VMEM_LIMIT),
        name="s5",
    )(u_tm, h0r, h0i, bre, bim, cre, cim, lre[:nb], lim[:nb], dsk)


def _dsa_body(q_ref, qc_ref, wt_ref, kc_ref, k_ref, vt_ref, wo_ref, o_ref,
              sc_ref, lg_ref, acc_ref, at_ref, *, tq, kb, cpb, s_real, s_pad, q_off, q_wrap, topk):
    j = pl.program_id(1)
    hq = N_HEADS * tq
    lane_q = lax.broadcasted_iota(I32, (1, tq), 1)
    qpos = q_off + j * tq + (lane_q & (q_wrap - 1))
    qchunk = qpos >> CHUNK_SHIFT
    last_end = (((q_off + j * tq + min(tq, q_wrap) - 1) >> CHUNK_SHIFT) + 1) << CHUNK_SHIFT
    nkb = jnp.minimum(s_pad // kb, lax.div(last_end + (kb - 1), kb))

    def blk(b):
        return pl.ds(pl.multiple_of(b * kb, kb), kb)

    def kpos_of(b):
        return b * kb + lax.broadcasted_iota(I32, (kb, tq), 0)

    def admissible(b):
        kpos = kpos_of(b)
        return ((kpos >> CHUNK_SHIFT) <= qchunk) & (kpos < s_real)

    def fold8(x, op):
        return op(x.reshape(kb // 8, 8, x.shape[-1]), axis=0)

    heads = lambda x: [x[:, h * tq:(h + 1) * tq] for h in range(N_HEADS)]

    qcs = qc_ref[0].reshape(hq, LANE)
    wt = wt_ref[0]

    def score_blk(b, carry):
        d = heads(_dot_nt(kc_ref[0, blk(b), :], qcs))
        sc = jnp.zeros((kb, tq), F32)
        for h in range(IDX_HEADS):
            sc = sc + wt[h:h + 1, :] * jnp.maximum(d[h], 0.0)
        sc_ref[blk(b), :] = jnp.where(admissible(b), sc, NEG_INF)
        return carry

    lax.fori_loop(0, nkb, score_blk, 0)

    cb = cpb * kb
    ncb = lax.div(nkb + (cpb - 1), cpb)

    def pad_blk(b, carry):
        sc_ref[blk(b), :] = jnp.full((kb, tq), NEG_INF, F32)
        return carry

    lax.fori_loop(nkb, ncb * cpb, pad_blk, 0)

    def as_float(u):
        s = u ^ INT_MIN
        return lax.bitcast_convert_type(s ^ ((s >> 31) & 0x7FFFFFFF), F32)

    def count(pred):
        def body(c, acc):
            x = sc_ref[pl.ds(pl.multiple_of(c * cb, cb), cb), :]
            kpos = lambda: c * cb + lax.broadcasted_iota(I32, (cb, tq), 0)
            ones = jnp.where(pred(x, kpos), 1.0, 0.0)
            return acc + jnp.sum(ones.reshape(cb // 32, 32, tq), axis=0)
        acc = lax.fori_loop(0, ncb, body, jnp.zeros((32, tq), F32))
        return jnp.sum(acc, axis=0, keepdims=True)

    def bit_body(i, prefix):
        cand = prefix | (jnp.int32(1) << (31 - i))
        cf = as_float(cand)
        cnt = count(lambda x, kpos: x >= cf)
        return jnp.where(cnt >= topk, cand, prefix)

    thr = as_float(lax.fori_loop(0, 32, bit_body, jnp.zeros((1, tq), I32)))

    n_gt = count(lambda x, kpos: x > thr)
    n_eq = count(lambda x, kpos: x == thr)
    need = topk - n_gt
    any_over = jnp.max(n_eq - need) > 0.0
    idx_bits = int(s_pad).bit_length()

    def tie_search():
        def tbody(i, lim):
            cand = lim | (jnp.int32(1) << (idx_bits - 1 - i))
            n = count(lambda x, kpos: (x == thr) & (kpos() < cand))
            return jnp.where(n <= need, cand, lim)
        return lax.fori_loop(0, idx_bits, tbody, jnp.zeros((1, tq), I32))

    lim = lax.cond(any_over, tie_search, lambda: jnp.full((1, tq), 2 ** idx_bits - 1, I32))

    qs = q_ref[0].reshape(hq, LANE)

    def logits_blk(b, m):
        x = sc_ref[blk(b), :]
        sel = ((x > thr) | ((x == thr) & (kpos_of(b) < lim))) & admissible(b)
        bias = jnp.where(sel, 0.0, NEG_INF)
        l = heads(_dot_nt(k_ref[0, blk(b), :], qs))
        tops = []
        for h in range(N_HEADS):
            lh = l[h] + bias
            lg_ref[blk(b), h * tq:(h + 1) * tq] = lh
            tops.append(fold8(lh, jnp.max))
        return jnp.maximum(m, jnp.concatenate(tops, axis=1))

    m8 = lax.fori_loop(0, nkb, logits_blk, jnp.full((8, hq), NEG_INF, F32))
    m = jnp.max(m8, axis=0, keepdims=True)

    acc_ref[...] = jnp.zeros_like(acc_ref)

    def pv_blk(b, ssum):
        p = jnp.exp(lg_ref[blk(b), :] - m)
        acc_ref[...] += _dot(vt_ref[0, b], p.astype(BF16))
        return ssum + fold8(p, jnp.sum)

    s8 = lax.fori_loop(0, nkb, pv_blk, jnp.zeros((8, hq), F32))
    out_t = acc_ref[...] / jnp.sum(s8, axis=0, keepdims=True)
    for h in range(N_HEADS):
        at_ref[HEAD_DIM * h:HEAD_DIM * (h + 1), :] = out_t[:, h * tq:(h + 1) * tq]
    o_ref[0] = _dot(at_ref[...].T.astype(BF16), wo_ref[...])


def _dsa(q, qc, wt, kc, kp, vt, wo, *, tq, kb, cpb, s_real, q_off, q_wrap):
    nb, _, t_len, _ = q.shape
    s_pad = kc.shape[1]
    topk = min(TOPK_MAX, s_real // 4)
    qspec = pl.BlockSpec((1, N_HEADS, tq, LANE), lambda b, j: (b, 0, j, 0))
    return pl.pallas_call(
        functools.partial(_dsa_body, tq=tq, kb=kb, cpb=cpb, s_real=s_real, s_pad=s_pad, q_off=q_off,
                          q_wrap=q_wrap, topk=float(topk)),
        grid=(nb, t_len // tq),
        in_specs=[qspec, qspec,
                  pl.BlockSpec((1, IDX_HEADS, tq), lambda b, j: (b, 0, j)),
                  pl.BlockSpec((1, s_pad, LANE), lambda b, j: (b, 0, 0)),
                  pl.BlockSpec((1, s_pad, LANE), lambda b, j: (b, 0, 0)),
                  pl.BlockSpec((1, s_pad // kb, HEAD_DIM, kb), lambda b, j: (b, 0, 0, 0)),
                  pl.BlockSpec((N_HEADS * HEAD_DIM, D_MODEL), lambda b, j: (0, 0))],
        out_specs=pl.BlockSpec((1, tq, D_MODEL), lambda b, j: (b, j, 0)),
        out_shape=jax.ShapeDtypeStruct((nb, t_len, D_MODEL), F32),
        scratch_shapes=[pltpu.VMEM((s_pad, tq), F32), pltpu.VMEM((s_pad, N_HEADS * tq), F32),
                        pltpu.VMEM((HEAD_DIM, N_HEADS * tq), F32),
                        pltpu.VMEM((N_HEADS * HEAD_DIM, tq), F32)],
        compiler_params=pltpu.CompilerParams(dimension_semantics=("parallel", "parallel"),
                                             vmem_limit_bytes=VMEM_LIMIT),
        name="dsa",
    )(q, qc, wt, kc, kp, vt, wo)


def _merge_body(x_ref, y_ref, ga_ref, gb_ref, ob_ref, mod_ref, wglu_ref, wout_ref, o_ref):
    z = jax.nn.gelu(y_ref[...], approximate=True).astype(BF16)
    glu = _dot(z, wglu_ref[...])
    out_a = glu[:, :D_MODEL] * jax.nn.sigmoid(glu[:, D_MODEL:])
    merged = ga_ref[...] * out_a + gb_ref[...] * ob_ref[...]
    upd = _dot(merged.astype(BF16), wout_ref[...])
    o_ref[...] = x_ref[...] + _gate_mul(upd, mod_ref[...], 2)


def _merge(xf, y, ga, gb, ob, mod3, wglu, wout, t_len, tm):
    n = xf.shape[0]
    row = lambda w: pl.BlockSpec((tm, w), lambda i: (i, 0))
    const = lambda s: pl.BlockSpec(s, lambda i: (0, 0))
    return pl.pallas_call(
        _merge_body,
        grid=(n // tm,),
        in_specs=[row(D_MODEL), row(SSM_WIDTH), row(D_MODEL), row(D_MODEL), row(D_MODEL),
                  _mod_spec(tm, t_len), const((SSM_WIDTH, 2 * D_MODEL)), const((D_MODEL, D_MODEL))],
        out_specs=row(D_MODEL),
        out_shape=jax.ShapeDtypeStruct((n, D_MODEL), F32),
        compiler_params=pltpu.CompilerParams(dimension_semantics=("parallel",),
                                             vmem_limit_bytes=VMEM_LIMIT),
        name="merge",
    )(xf, y, ga, gb, ob, mod3, wglu, wout)


def _moe_body(x_ref, mod_ref, wr_ref, br_ref, wgu_ref, wd_ref, o_ref, hb_ref, comb_ref, acc_ref):
    e = pl.program_id(1)
    tm = x_ref.shape[0]
    lane = lax.broadcasted_iota(I32, (tm, LANE), 1).astype(F32)

    @pl.when(e == 0)
    def _():
        h2 = _modulate(_rms(x_ref[...], D_MODEL), mod_ref[...], 4, 3)
        hb_ref[...] = h2.astype(BF16)
        lg = _dot_hilo(h2, wr_ref[...]) + br_ref[...]
        neg = -jnp.inf
        gl = jnp.where(lane < N_GROUPS, lg, neg)
        gmax = jnp.max(gl, axis=-1, keepdims=True)
        g_idx = jnp.min(jnp.where(gl == gmax, lane, float(LANE)), axis=-1, keepdims=True)
        g_w = 1.0 / jnp.sum(jnp.where(lane < N_GROUPS, jnp.exp(lg - gmax), 0.0),
                            axis=-1, keepdims=True)
        lo = N_GROUPS + EXPERTS_PER_GROUP * g_idx
        el = jnp.where((lane >= lo) & (lane < lo + EXPERTS_PER_GROUP), lg, neg)
        v1 = jnp.max(el, axis=-1, keepdims=True)
        i1 = jnp.min(jnp.where(el == v1, lane, float(LANE)), axis=-1, keepdims=True)
        el2 = jnp.where(lane == i1, neg, el)
        v2 = jnp.max(el2, axis=-1, keepdims=True)
        i2 = jnp.min(jnp.where(el2 == v2, lane, float(LANE)), axis=-1, keepdims=True)
        e21 = jnp.exp(v2 - v1)
        w1 = g_w / (1.0 + e21)
        w2 = g_w * e21 / (1.0 + e21)
        comb_ref[...] = jnp.where(lane == i1, w1, 0.0) + jnp.where(lane == i2, w2, 0.0)
        acc_ref[...] = jnp.zeros_like(acc_ref)

    gu = _dot(hb_ref[...], wgu_ref[0])
    g = gu[:, :EXPERT_DIM]
    hid = g * jax.nn.sigmoid(g) * gu[:, EXPERT_DIM:]
    col = jnp.sum(jnp.where(lane == (e + N_GROUPS).astype(F32), comb_ref[...], 0.0),
                  axis=-1, keepdims=True)
    acc_ref[...] += _dot((hid * col).astype(BF16), wd_ref[0])

    @pl.when(e == N_EXPERTS - 1)
    def _():
        o_ref[...] = x_ref[...] + _gate_mul(acc_ref[...], mod_ref[...], 5)


def _moe(x1, mod3, wr, br, wgu, wd, t_len, tm):
    n = x1.shape[0]
    return pl.pallas_call(
        _moe_body,
        grid=(n // tm, N_EXPERTS),
        in_specs=[pl.BlockSpec((tm, D_MODEL), lambda i, e: (i, 0)),
                  _mod_spec(tm, t_len),
                  pl.BlockSpec((D_MODEL, LANE), lambda i, e: (0, 0)),
                  pl.BlockSpec((1, LANE), lambda i, e: (0, 0)),
                  pl.BlockSpec((1, D_MODEL, 2 * EXPERT_DIM), lambda i, e: (e, 0, 0)),
                  pl.BlockSpec((1, EXPERT_DIM, D_MODEL), lambda i, e: (e, 0, 0))],
        out_specs=pl.BlockSpec((tm, D_MODEL), lambda i, e: (i, 0)),
        out_shape=jax.ShapeDtypeStruct((n, D_MODEL), F32),
        scratch_shapes=[pltpu.VMEM((tm, D_MODEL), BF16), pltpu.VMEM((tm, LANE), F32),
                        pltpu.VMEM((tm, D_MODEL), F32)],
        compiler_params=pltpu.CompilerParams(dimension_semantics=("parallel", "arbitrary"),
                                             vmem_limit_bytes=VMEM_LIMIT),
        name="moe",
    )(x1, mod3, wr, br, wgu, wd)


def _prep_params(w_in, ssm_a_re, ssm_a_im, ssm_log_dt, ssm_b_re, ssm_b_im, ssm_c_re, ssm_c_im, ssm_d,
                 w_glu, q_gain, k_gain, kidx_gain, w_attn_out, w_out, w_rg, b_rg, w_re, b_re,
                 w_gate, w_up, w_down, max_batch):
    offs = np.cumsum([0, SSM_WIDTH, N_HEADS * HEAD_DIM, HEAD_DIM, HEAD_DIM, IDX_HEADS * IDX_DIM,
                      IDX_DIM, IDX_HEADS, D_MODEL, D_MODEL])
    seg = lambda i: w_in[:, offs[i]:offs[i + 1]]
    padto = lambda a, w: jnp.pad(a, ((0, 0), (0, w - a.shape[1])))
    wq = jnp.pad(seg(1).reshape(D_MODEL, N_HEADS, HEAD_DIM), ((0, 0), (0, 0), (0, LANE - HEAD_DIM)))
    w_pad = jnp.concatenate(
        [seg(0), wq.reshape(D_MODEL, N_HEADS * LANE), padto(seg(2), LANE), padto(seg(3), LANE),
         seg(4), padto(seg(5), LANE), padto(seg(6), LANE), seg(7), seg(8)], axis=1).astype(BF16)
    qg = jnp.tile(jnp.pad(q_gain * HEAD_DIM ** -0.5, (0, LANE - HEAD_DIM)), N_HEADS)[None, :]

    dt = jnp.exp(ssm_log_dt)[:, None]
    decay = jnp.exp(ssm_a_re * dt)
    lam_re = decay * jnp.cos(ssm_a_im * dt)
    lam_im = decay * jnp.sin(ssm_a_im * dt)
    den = ssm_a_re * ssm_a_re + ssm_a_im * ssm_a_im
    num_re = lam_re - 1.0
    coef_re = (num_re * ssm_a_re + lam_im * ssm_a_im) / den
    coef_im = (lam_im * ssm_a_re - num_re * ssm_a_im) / den
    bb_re = coef_re[..., None] * ssm_b_re - coef_im[..., None] * ssm_b_im
    bb_im = coef_re[..., None] * ssm_b_im + coef_im[..., None] * ssm_b_re
    eye = jnp.eye(SSM_GROUPS, dtype=F32)

    def in_map(bb):
        return jnp.einsum('gpj,gh->gjhp', bb, eye).reshape(SSM_WIDTH, SSM_LANES).astype(BF16)

    def out_map(c):
        return jnp.einsum('gjp,gh->gphj', c, eye).reshape(SSM_LANES, SSM_WIDTH).astype(BF16)

    bcast = lambda a: jnp.broadcast_to(a.reshape(1, SSM_LANES), (max_batch, SSM_LANES))
    ssm = (in_map(bb_re), in_map(bb_im), out_map(ssm_c_re), out_map(-ssm_c_im),
           bcast(lam_re), bcast(lam_im), ssm_d[None, :])

    wr = jnp.pad(jnp.concatenate([w_rg, w_re], axis=1), ((0, 0), (0, LANE - N_GROUPS - N_EXPERTS)))
    br = jnp.pad(jnp.concatenate([b_rg, b_re]), (0, LANE - N_GROUPS - N_EXPERTS))[None, :]
    wgu = jnp.concatenate([w_gate, w_up], axis=-1).astype(BF16)
    return dict(w_pad=w_pad, qg=qg, kg=k_gain[None, :], ig=kidx_gain[None, :], ssm=ssm,
                wglu=w_glu.astype(BF16), wo=w_attn_out.astype(BF16), wout=w_out.astype(BF16),
                wr=wr, br=br, wgu=wgu, wd=w_down.astype(BF16))


def _layer(x, mod3, p, past, *, tm_tok, tm_moe, tc, tq, kb, cpb):
    nb, t_len, _ = x.shape
    n = nb * t_len
    xf = x.reshape(n, D_MODEL)
    u, q, k, v, qih, qil, ki, wi, ga, gb = _proj(xf, mod3, p["w_pad"], p["qg"], p["kg"], p["ig"],
                                                 t_len, tm_tok)
    k3 = k.reshape(nb, t_len, HEAD_DIM)
    v3 = v.reshape(nb, t_len, HEAD_DIM)
    ki3 = ki.reshape(nb, t_len, IDX_DIM)

    u_tm = u.reshape(nb, t_len, SSM_WIDTH).transpose(1, 0, 2).reshape(n, SSM_WIDTH)
    if past is None:
        h0r = jnp.zeros((nb, SSM_LANES), F32)
        h0i = h0r
        k_all, v_all, ki_all, q_off = k3, v3, ki3, 0
    else:
        past_k, past_v, past_ki, h0_re, h0_im = past
        h0r = h0_re.reshape(nb, SSM_LANES)
        h0i = h0_im.reshape(nb, SSM_LANES)
        k_all = jnp.concatenate([past_k, k3], axis=1)
        v_all = jnp.concatenate([past_v, v3], axis=1)
        ki_all = jnp.concatenate([past_ki, ki3], axis=1)
        q_off = past_k.shape[1]
    y_tm, s_re, s_im = _s5(u_tm, h0r, h0i, p["ssm"], nb, t_len, tc)
    y = y_tm.reshape(t_len, nb, SSM_WIDTH).transpose(1, 0, 2).reshape(n, SSM_WIDTH)

    s_real = k_all.shape[1]
    s_pad = -(-s_real // kb) * kb
    pad_s = lambda a: jnp.pad(a, ((0, 0), (0, s_pad - s_real), (0, 0)))
    kp = jnp.pad(pad_s(k_all).astype(BF16), ((0, 0), (0, 0), (0, LANE - HEAD_DIM)))
    vt = pad_s(v_all).astype(BF16).reshape(nb, s_pad // kb, kb, HEAD_DIM).transpose(0, 1, 3, 2)
    ki_hi, ki_lo = _split_bf16(pad_s(ki_all))
    kc = jnp.concatenate([ki_hi, ki_lo, ki_hi, ki_lo], axis=-1)
    qih3 = qih.reshape(n, IDX_HEADS, IDX_DIM)
    qil3 = qil.reshape(n, IDX_HEADS, IDX_DIM)
    qc = jnp.concatenate([qih3, qih3, qil3, qil3], axis=-1)
    qc = qc.reshape(nb, t_len, IDX_HEADS, LANE).transpose(0, 2, 1, 3)
    wt = wi.reshape(nb, t_len, IDX_HEADS).transpose(0, 2, 1)
    q3 = q.reshape(nb, t_len, N_HEADS, LANE).transpose(0, 2, 1, 3)
    reps = tq // t_len if t_len < tq else 1
    if reps > 1:
        q3 = jnp.tile(q3, (1, 1, reps, 1))
        qc = jnp.tile(qc, (1, 1, reps, 1))
        wt = jnp.tile(wt, (1, 1, reps))
    ob = _dsa(q3, qc, wt, kc, kp, vt, p["wo"], tq=tq, kb=kb, cpb=cpb, s_real=s_real, q_off=q_off,
              q_wrap=t_len if reps > 1 else tq)
    ob = ob[:, :t_len].reshape(n, D_MODEL)

    x1 = _merge(xf, y, ga, gb, ob, mod3, p["wglu"], p["wout"], t_len, tm_tok)
    x2 = _moe(x1, mod3, p["wr"], p["br"], p["wgu"], p["wd"], t_len, tm_moe)
    return (x2.reshape(nb, t_len, D_MODEL), k3, v3, ki3,
            s_re.reshape(nb, SSM_GROUPS, SSM_STATE), s_im.reshape(nb, SSM_GROUPS, SSM_STATE))


def kernel(x_prompt, x_sample, cache_k, cache_v, cache_kidx, state_ssm_re, state_ssm_im, c_prompt, c_sample, w_ada, b_ada, w_in, ssm_a_re, ssm_a_im, ssm_log_dt, ssm_b_re, ssm_b_im, ssm_c_re, ssm_c_im, ssm_d, w_glu, q_gain, k_gain, kidx_gain, w_attn_out, w_out, w_route_group, b_route_group, w_route_expert, b_route_expert, w_gate, w_up, w_down):
    depth = w_ada.shape[0]
    nbp = x_prompt.shape[0]
    nbs = x_sample.shape[0]
    xp, xs = x_prompt, x_sample
    outs = [[] for _ in range(10)]
    for l in range(depth):
        p = _prep_params(w_in[l], ssm_a_re[l], ssm_a_im[l], ssm_log_dt[l], ssm_b_re[l], ssm_b_im[l],
                         ssm_c_re[l], ssm_c_im[l], ssm_d[l], w_glu[l], q_gain[l], k_gain[l],
                         kidx_gain[l], w_attn_out[l], w_out[l], w_route_group[l], b_route_group[l],
                         w_route_expert[l], b_route_expert[l], w_gate[l], w_up[l], w_down[l],
                         max(nbp, nbs))
        mod = _adaln(jnp.concatenate([c_prompt, c_sample], axis=0), w_ada[l], b_ada[l][None, :])
        mod3 = mod[:, None, :]
        xp, kp, vp, kip, srp, sip = _layer(xp, mod3[:nbp], p, None,
                                           tm_tok=512, tm_moe=1024, tc=32, tq=128, kb=256, cpb=2)
        past = (cache_k[l], cache_v[l], cache_kidx[l], state_ssm_re[l], state_ssm_im[l])
        xs, ks, vs, kis, srs, sis = _layer(xs, mod3[nbp:], p, past,
                                           tm_tok=512, tm_moe=512, tc=64, tq=128, kb=384, cpb=1)
        for lst, val in zip(outs, (kp, vp, kip, srp, sip, ks, vs, kis, srs, sis)):
            lst.append(val)
    return (xp, xs) + tuple(jnp.stack(o) for o in outs)
```

```python
import functools

import jax
import jax.numpy as jnp
import numpy as np
from jax import lax
from jax.experimental import pallas as pl
from jax.experimental.pallas import tpu as pltpu

F32 = jnp.float32
BF16 = jnp.bfloat16
I32 = jnp.int32

D_MODEL = 1024
CHUNK_SHIFT = 6
EPS = 1e-6
NEG_INF = -1e30
SSM_WIDTH = 512
SSM_GROUP = 16
SSM_GROUPS = 32
SSM_STATE = 64
SSM_LANES = SSM_GROUPS * SSM_STATE
N_HEADS = 8
HEAD_DIM = 64
IDX_HEADS = 8
IDX_DIM = 32
IDX_WEIGHT_SCALE = (IDX_HEADS * IDX_DIM) ** -0.5
TOPK_MAX = 256
N_GROUPS = 4
EXPERTS_PER_GROUP = 4
N_EXPERTS = 16
EXPERT_DIM = 256
LANE = 128
INT_MIN = -(2 ** 31)
VMEM_LIMIT = 56 * 1024 * 1024

C_U = 0
C_Q = C_U + SSM_WIDTH
C_K = C_Q + N_HEADS * LANE
C_V = C_K + LANE
C_QI = C_V + LANE
C_KI = C_QI + IDX_HEADS * LANE
C_WI = C_KI + LANE
C_GA = C_WI + LANE
C_GB = C_GA + D_MODEL
C_END = C_GB + D_MODEL


def _dot(a, b):
    return jnp.dot(a, b, preferred_element_type=F32)


def _dot_nt(a, b):
    return lax.dot_general(a, b, (((1,), (1,)), ((), ())), preferred_element_type=F32)


def _split_bf16(x):
    hi = x.astype(BF16)
    lo = (x - hi.astype(F32)).astype(BF16)
    return hi, lo


def _dot_hilo(a, b):
    ah, al = _split_bf16(a)
    bh, bl = _split_bf16(b)
    return _dot(ah, bh) + _dot(ah, bl) + _dot(al, bh)


def _tiling(tm, t_len):
    return (t_len // tm, 1) if tm < t_len else (1, tm // t_len)


def _mod_spec(tm, t_len):
    per, nbt = _tiling(tm, t_len)
    return pl.BlockSpec((nbt, 1, 6 * D_MODEL), lambda i, *_: (i // per, 0, 0))


def _tm_spec(tm, t_len):
    per, nbt = _tiling(tm, t_len)
    return pl.BlockSpec((tm // nbt, nbt * SSM_WIDTH), lambda i, *_: (i % per, i // per))


def _modulate(x, mod, scale_seg, shift_seg):
    nb = mod.shape[0]
    tm = x.shape[0]
    sc = mod[:, :, scale_seg * D_MODEL:(scale_seg + 1) * D_MODEL]
    sh = mod[:, :, shift_seg * D_MODEL:(shift_seg + 1) * D_MODEL]
    x3 = x.reshape(nb, tm // nb, D_MODEL)
    return (x3 * (1.0 + sc) + sh).reshape(tm, D_MODEL)


def _gate_mul(x, mod, seg):
    nb = mod.shape[0]
    tm = x.shape[0]
    g = mod[:, :, seg * D_MODEL:(seg + 1) * D_MODEL]
    return (x.reshape(nb, tm // nb, D_MODEL) * g).reshape(tm, D_MODEL)


def _rms(x, n):
    return x * lax.rsqrt(jnp.sum(x * x, axis=-1, keepdims=True) * (1.0 / n) + EPS)


def _adaln_body(c_ref, w_ref, b_ref, o_ref):
    c = c_ref[...]
    s = c * jax.nn.sigmoid(c)
    o_ref[...] = _dot_hilo(s, w_ref[...]) + b_ref[...]


def _adaln(c, w, b):
    nb = c.shape[0]
    n = w.shape[1]
    bn = 512
    return pl.pallas_call(
        _adaln_body,
        grid=(n // bn,),
        in_specs=[pl.BlockSpec((nb, D_MODEL), lambda j: (0, 0)),
                  pl.BlockSpec((D_MODEL, bn), lambda j: (0, j)),
                  pl.BlockSpec((1, bn), lambda j: (0, j))],
        out_specs=pl.BlockSpec((nb, bn), lambda j: (0, j)),
        out_shape=jax.ShapeDtypeStruct((nb, n), F32),
        name="adaln",
    )(c, w, b)


def _proj_body(x_ref, mod_ref, w_ref, qg_ref, kg_ref, ig_ref,
               u_ref, q_ref, qc_ref, k_ref, v_ref, ki_ref, kp_ref, kc_ref, vt_ref, wt_ref,
               ga_ref, gb_ref, *, nbt, reps, kbv):
    tm = x_ref.shape[0]
    tt = tm // nbt
    rows = lambda b: slice(tt * b, tt * (b + 1))
    hsl = lambda h: slice(LANE * h, LANE * (h + 1))
    lane = lax.broadcasted_iota(I32, (tm, LANE), 1)
    hb = _modulate(_rms(x_ref[...], D_MODEL), mod_ref[...], 1, 0).astype(BF16)

    u = _dot(hb, w_ref[:, C_U:C_Q])
    for b in range(nbt):
        u_ref[:, SSM_WIDTH * b:SSM_WIDTH * (b + 1)] = u[rows(b), :]

    def put_heads(ref, pieces):
        for b in range(nbt):
            for h in range(N_HEADS):
                for r in range(reps):
                    ref[b, h, tt * r:tt * (r + 1), :] = pieces[h][rows(b), :]

    q = _dot(hb, w_ref[:, C_Q:C_K])
    put_heads(q_ref, [(_rms(q[:, hsl(h)], HEAD_DIM) * qg_ref[...]).astype(BF16)
                      for h in range(N_HEADS)])

    qi_hi, qi_lo = _split_bf16(_dot(hb, w_ref[:, C_QI:C_KI]))
    put_heads(qc_ref, [jnp.where(lane < 2 * IDX_DIM, qi_hi[:, hsl(h)], qi_lo[:, hsl(h)])
                       for h in range(IDX_HEADS)])

    kn = _rms(_dot(hb, w_ref[:, C_K:C_V]), HEAD_DIM) * kg_ref[...]
    k_ref[...] = kn[:, :HEAD_DIM]
    kp_ref[...] = kn.astype(BF16)

    vfull = _dot(hb, w_ref[:, C_V:C_QI])
    v_ref[...] = vfull[:, :HEAD_DIM]
    vt = vfull.T[:HEAD_DIM].astype(BF16)
    for b in range(nbt):
        for c in range(tt // kbv):
            vt_ref[b, c] = vt[:, tt * b + kbv * c:tt * b + kbv * (c + 1)]

    kin = _rms(_dot(hb, w_ref[:, C_KI:C_WI]), LANE) * ig_ref[...]
    ki_ref[...] = kin[:, :IDX_DIM]
    ki_hi, ki_lo = _split_bf16(kin)
    kc_ref[...] = jnp.where(((lane >> 5) & 1) == 1, ki_lo, ki_hi)

    wt = (_dot(hb, w_ref[:, C_WI:C_GA]) * IDX_WEIGHT_SCALE).T[:IDX_HEADS]
    for b in range(nbt):
        for r in range(reps):
            wt_ref[b, :, tt * r:tt * (r + 1)] = wt[:, rows(b)]

    ga_ref[...] = jax.nn.sigmoid(_dot(hb, w_ref[:, C_GA:C_GB]))
    gb_ref[...] = jax.nn.sigmoid(_dot(hb, w_ref[:, C_GB:C_END]))


def _proj(xf, mod3, w_pad, qg, kg, ig, nb, t_len, tm, reps, kbv):
    n = xf.shape[0]
    per, nbt = _tiling(tm, t_len)
    tt = tm // nbt
    row = lambda w: pl.BlockSpec((tm, w), lambda i: (i, 0))
    const = lambda s: pl.BlockSpec(s, lambda i: (0, 0))
    head_spec = pl.BlockSpec((nbt, N_HEADS, tt * reps, LANE), lambda i: (i // per, 0, i % per, 0))
    head_shape = jax.ShapeDtypeStruct((nb, N_HEADS, t_len * reps, LANE), BF16)
    outs = [
        (_tm_spec(tm, t_len), jax.ShapeDtypeStruct((t_len, nb * SSM_WIDTH), F32)),
        (head_spec, head_shape), (head_spec, head_shape),
        (row(HEAD_DIM), jax.ShapeDtypeStruct((n, HEAD_DIM), F32)),
        (row(HEAD_DIM), jax.ShapeDtypeStruct((n, HEAD_DIM), F32)),
        (row(IDX_DIM), jax.ShapeDtypeStruct((n, IDX_DIM), F32)),
        (row(LANE), jax.ShapeDtypeStruct((n, LANE), BF16)),
        (row(LANE), jax.ShapeDtypeStruct((n, LANE), BF16)),
        (pl.BlockSpec((nbt, tt // kbv, HEAD_DIM, kbv), lambda i: (i // per, i % per, 0, 0)),
         jax.ShapeDtypeStruct((nb, t_len // kbv, HEAD_DIM, kbv), BF16)),
        (pl.BlockSpec((nbt, IDX_HEADS, tt * reps), lambda i: (i // per, 0, i % per)),
         jax.ShapeDtypeStruct((nb, IDX_HEADS, t_len * reps), F32)),
        (row(D_MODEL), jax.ShapeDtypeStruct((n, D_MODEL), F32)),
        (row(D_MODEL), jax.ShapeDtypeStruct((n, D_MODEL), F32)),
    ]
    return pl.pallas_call(
        functools.partial(_proj_body, nbt=nbt, reps=reps, kbv=kbv),
        grid=(n // tm,),
        in_specs=[row(D_MODEL), _mod_spec(tm, t_len), const((D_MODEL, C_END)),
                  const((1, LANE)), const((1, LANE)), const((1, LANE))],
        out_specs=[s for s, _ in outs],
        out_shape=[s for _, s in outs],
        compiler_params=pltpu.CompilerParams(dimension_semantics=("parallel",),
                                             vmem_limit_bytes=VMEM_LIMIT),
        name="proj",
    )(xf, mod3, w_pad, qg, kg, ig)


def _s5_body(u_ref, h0r_ref, h0i_ref, bre_ref, bim_ref, cre_ref, cim_ref, lre_ref, lim_ref, d_ref,
             y_ref, sr_ref, si_ref, xr, xi, hr, hi, *, nb, tc, lane_chunk):
    @pl.when(pl.program_id(0) == 0)
    def _():
        hr[...] = h0r_ref[...]
        hi[...] = h0i_ref[...]

    u = u_ref[...]
    ub = u.astype(BF16)
    xr[...] = _dot(ub, bre_ref[...])
    xi[...] = _dot(ub, bim_ref[...])

    for lc in range(SSM_LANES // lane_chunk):
        cols = slice(lc * lane_chunk, (lc + 1) * lane_chunk)

        def step(t, carry, cols=cols):
            a, b = carry
            rows = pl.ds(pl.multiple_of(t * nb, nb), nb)
            lr = lre_ref[:, cols]
            li = lim_ref[:, cols]
            na = lr * a - li * b + xr[rows, cols]
            nb_ = lr * b + li * a + xi[rows, cols]
            xr[rows, cols] = na
            xi[rows, cols] = nb_
            return na, nb_

        a, b = lax.fori_loop(0, tc, step, (hr[:, cols], hi[:, cols]), unroll=2)
        hr[:, cols] = a
        hi[:, cols] = b

    y = _dot(xr[...].astype(BF16), cre_ref[...]) + _dot(xi[...].astype(BF16), cim_ref[...])
    y_ref[...] = y + d_ref[...] * u
    sr_ref[...] = hr[...]
    si_ref[...] = hi[...]


def _s5(u_tm, h0r, h0i, ssm, nb, t_len, tc):
    bre, bim, cre, cim, lre, lim, dsk = ssm
    rows = tc * nb
    const = lambda s: pl.BlockSpec(s, lambda c: (0, 0))
    return pl.pallas_call(
        functools.partial(_s5_body, nb=nb, tc=tc, lane_chunk=512),
        grid=(t_len // tc,),
        in_specs=[pl.BlockSpec((rows, SSM_WIDTH), lambda c: (c, 0)),
                  const((nb, SSM_LANES)), const((nb, SSM_LANES)),
                  const((SSM_WIDTH, SSM_LANES)), const((SSM_WIDTH, SSM_LANES)),
                  const((SSM_LANES, SSM_WIDTH)), const((SSM_LANES, SSM_WIDTH)),
                  const((nb, SSM_LANES)), const((nb, SSM_LANES)), const((1, SSM_WIDTH))],
        out_specs=[pl.BlockSpec((rows, SSM_WIDTH), lambda c: (c, 0)),
                   const((nb, SSM_LANES)), const((nb, SSM_LANES))],
        out_shape=[jax.ShapeDtypeStruct((t_len * nb, SSM_WIDTH), F32),
                   jax.ShapeDtypeStruct((nb, SSM_LANES), F32),
                   jax.ShapeDtypeStruct((nb, SSM_LANES), F32)],
        scratch_shapes=[pltpu.VMEM((rows, SSM_LANES), F32), pltpu.VMEM((rows, SSM_LANES), F32),
                        pltpu.VMEM((nb, SSM_LANES), F32), pltpu.VMEM((nb, SSM_LANES), F32)],
        compiler_params=pltpu.CompilerParams(dimension_semantics=("arbitrary",),
                                             vmem_limit_bytes=VMEM_LIMIT),
        name="s5",
    )(u_tm, h0r, h0i, bre, bim, cre, cim, lre[:nb], lim[:nb], dsk)


def _dsa_body(q_ref, qc_ref, wt_ref, kc_ref, k_ref, vt_ref, wo_ref, o_ref,
              sc_ref, lg_ref, acc_ref, at_ref, st0_ref, st1_ref,
              *, tq, kb, s_real, s_pad, q_off, q_wrap, topk):
    j = pl.program_id(1)
    hq = N_HEADS * tq
    n_all = s_pad // kb
    cb = 2 * kb
    lane_q = lax.broadcasted_iota(I32, (1, tq), 1)
    qpos = q_off + j * tq + (lane_q & (q_wrap - 1))
    qchunk = qpos >> CHUNK_SHIFT
    last_end = (((q_off + j * tq + min(tq, q_wrap) - 1) >> CHUNK_SHIFT) + 1) << CHUNK_SHIFT
    npair = jnp.minimum(n_all // 2, lax.div(last_end + (cb - 1), cb))

    def blk(b):
        return pl.ds(pl.multiple_of(b * kb, kb), kb)

    def pair(i):
        return pl.ds(pl.multiple_of(i * cb, cb), cb)

    def admissible(b):
        kpos = b * kb + lax.broadcasted_iota(I32, (kb, tq), 0)
        return ((kpos >> CHUNK_SHIFT) <= qchunk) & (kpos < s_real), kpos

    def fold8(x, op):
        return op(x.reshape(x.shape[0] // 8, 8, x.shape[-1]), axis=0)

    hsl = lambda h: slice(h * tq, (h + 1) * tq)

    def staged(mm, consume, carry):
        st0_ref[...] = mm(0)

        def body(i, c):
            b0 = 2 * i
            st1_ref[...] = mm(b0 + 1)
            c = consume(b0, st0_ref, c)
            st0_ref[...] = mm(jnp.minimum(b0 + 2, n_all - 1))
            return consume(b0 + 1, st1_ref, c)

        return lax.fori_loop(0, npair, body, carry)

    qcs = qc_ref[0].reshape(hq, LANE)
    wt = wt_ref[0]

    rc = 64

    def sub(b, r):
        return pl.ds(pl.multiple_of(b * kb + r * rc, rc), rc)

    def admissible_rows(b, r):
        kpos = b * kb + r * rc + lax.broadcasted_iota(I32, (rc, tq), 0)
        return ((kpos >> CHUNK_SHIFT) <= qchunk) & (kpos < s_real), kpos

    def score_consume(b, st, carry):
        for r in range(kb // rc):
            sc = jnp.zeros((rc, tq), F32)
            for h in range(IDX_HEADS):
                sc = sc + wt[h:h + 1, :] * jnp.maximum(st[r * rc:(r + 1) * rc, hsl(h)], 0.0)
            sc_ref[sub(b, r), :] = jnp.where(admissible_rows(b, r)[0], sc, NEG_INF)
        return carry

    staged(lambda b: _dot_nt(kc_ref[0, blk(b), :], qcs), score_consume, 0)

    def as_float(u):
        s = u ^ INT_MIN
        return lax.bitcast_convert_type(s ^ ((s >> 31) & 0x7FFFFFFF), F32)

    def count(pred):
        def body(i, acc):
            kpos = lambda: i * cb + lax.broadcasted_iota(I32, (cb, tq), 0)
            ones = jnp.where(pred(sc_ref[pair(i), :], kpos), 1.0, 0.0)
            return acc + jnp.sum(ones.reshape(cb // 32, 32, tq), axis=0)
        acc = lax.fori_loop(0, npair, body, jnp.zeros((32, tq), F32))
        return jnp.sum(acc, axis=0, keepdims=True)

    def bit_body(i, prefix):
        cand = prefix | (jnp.int32(1) << (31 - i))
        cf = as_float(cand)
        cnt = count(lambda x, kpos: x >= cf)
        return jnp.where(cnt >= topk, cand, prefix)

    thr = as_float(lax.fori_loop(0, 32, bit_body, jnp.zeros((1, tq), I32)))

    n_gt = count(lambda x, kpos: x > thr)
    n_eq = count(lambda x, kpos: x == thr)
    need = topk - n_gt
    any_over = jnp.max(n_eq - need) > 0.0
    idx_bits = int(s_pad).bit_length()

    def tie_search():
        def tbody(i, lim):
            cand = lim | (jnp.int32(1) << (idx_bits - 1 - i))
            n = count(lambda x, kpos: (x == thr) & (kpos() < cand))
            return jnp.where(n <= need, cand, lim)
        return lax.fori_loop(0, idx_bits, tbody, jnp.zeros((1, tq), I32))

    lim = lax.cond(any_over, tie_search, lambda: jnp.full((1, tq), 2 ** idx_bits - 1, I32))

    qs = q_ref[0].reshape(hq, LANE)

    def logits_consume(b, st, m):
        for r in range(kb // rc):
            x = sc_ref[sub(b, r), :]
            adm, kpos = admissible_rows(b, r)
            sel = ((x > thr) | ((x == thr) & (kpos < lim))) & adm
            bias = jnp.where(sel, 0.0, NEG_INF)
            tops = []
            for h in range(N_HEADS):
                lh = st[r * rc:(r + 1) * rc, hsl(h)] + bias
                lg_ref[sub(b, r), hsl(h)] = lh
                tops.append(fold8(lh, jnp.max))
            m = jnp.maximum(m, jnp.concatenate(tops, axis=1))
        return m

    m8 = staged(lambda b: _dot_nt(k_ref[0, blk(b), :], qs), logits_consume,
                jnp.full((8, hq), NEG_INF, F32))
    m = jnp.max(m8, axis=0, keepdims=True)

    acc_ref[...] = jnp.zeros_like(acc_ref)

    def pv_pair(i, ssum):
        p = jnp.exp(lg_ref[pair(i), :] - m)
        acc_ref[...] += _dot(vt_ref[0, i], p.astype(BF16))
        return ssum + fold8(p, jnp.sum)

    s8 = lax.fori_loop(0, npair, pv_pair, jnp.zeros((8, hq), F32))
    out_t = acc_ref[...] / jnp.sum(s8, axis=0, keepdims=True)
    for h in range(N_HEADS):
        at_ref[HEAD_DIM * h:HEAD_DIM * (h + 1), :] = out_t[:, hsl(h)]
    o_ref[0] = _dot(at_ref[...].T.astype(BF16), wo_ref[...])


def _dsa(q, qc, wt, kc, kp, vt, wo, *, tq, kb, s_real, q_off, q_wrap):
    nb, _, t_len, _ = q.shape
    s_pad = kc.shape[1]
    cb = 2 * kb
    topk = min(TOPK_MAX, s_real // 4)
    hq = N_HEADS * tq
    qspec = pl.BlockSpec((1, N_HEADS, tq, LANE), lambda b, j: (b, 0, j, 0))
    return pl.pallas_call(
        functools.partial(_dsa_body, tq=tq, kb=kb, s_real=s_real, s_pad=s_pad, q_off=q_off,
                          q_wrap=q_wrap, topk=float(topk)),
        grid=(nb, t_len // tq),
        in_specs=[qspec, qspec,
                  pl.BlockSpec((1, IDX_HEADS, tq), lambda b, j: (b, 0, j)),
                  pl.BlockSpec((1, s_pad, LANE), lambda b, j: (b, 0, 0)),
                  pl.BlockSpec((1, s_pad, LANE), lambda b, j: (b, 0, 0)),
                  pl.BlockSpec((1, s_pad // cb, HEAD_DIM, cb), lambda b, j: (b, 0, 0, 0)),
                  pl.BlockSpec((N_HEADS * HEAD_DIM, D_MODEL), lambda b, j: (0, 0))],
        out_specs=pl.BlockSpec((1, tq, D_MODEL), lambda b, j: (b, j, 0)),
        out_shape=jax.ShapeDtypeStruct((nb, t_len, D_MODEL), F32),
        scratch_shapes=[pltpu.VMEM((s_pad, tq), F32), pltpu.VMEM((s_pad, hq), F32),
                        pltpu.VMEM((HEAD_DIM, hq), F32), pltpu.VMEM((N_HEADS * HEAD_DIM, tq), F32),
                        pltpu.VMEM((kb, hq), F32), pltpu.VMEM((kb, hq), F32)],
        compiler_params=pltpu.CompilerParams(dimension_semantics=("parallel", "parallel"),
                                             vmem_limit_bytes=VMEM_LIMIT),
        name="dsa",
    )(q, qc, wt, kc, kp, vt, wo)


def _merge_body(x_ref, y_ref, ga_ref, gb_ref, ob_ref, mod_ref, wglu_ref, wout_ref, o_ref, *, nbt):
    y = y_ref[...]
    if nbt > 1:
        y = jnp.concatenate([y[:, SSM_WIDTH * b:SSM_WIDTH * (b + 1)] for b in range(nbt)], axis=0)
    z = jax.nn.gelu(y, approximate=True).astype(BF16)
    glu = _dot(z, wglu_ref[...])
    out_a = glu[:, :D_MODEL] * jax.nn.sigmoid(glu[:, D_MODEL:])
    merged = ga_ref[...] * out_a + gb_ref[...] * ob_ref[...]
    upd = _dot(merged.astype(BF16), wout_ref[...])
    o_ref[...] = x_ref[...] + _gate_mul(upd, mod_ref[...], 2)


def _merge(xf, y_tm, ga, gb, ob, mod3, wglu, wout, t_len, tm):
    n = xf.shape[0]
    row = lambda w: pl.BlockSpec((tm, w), lambda i: (i, 0))
    const = lambda s: pl.BlockSpec(s, lambda i: (0, 0))
    return pl.pallas_call(
        functools.partial(_merge_body, nbt=_tiling(tm, t_len)[1]),
        grid=(n // tm,),
        in_specs=[row(D_MODEL), _tm_spec(tm, t_len), row(D_MODEL), row(D_MODEL), row(D_MODEL),
                  _mod_spec(tm, t_len), const((SSM_WIDTH, 2 * D_MODEL)), const((D_MODEL, D_MODEL))],
        out_specs=row(D_MODEL),
        out_shape=jax.ShapeDtypeStruct((n, D_MODEL), F32),
        compiler_params=pltpu.CompilerParams(dimension_semantics=("parallel",),
                                             vmem_limit_bytes=VMEM_LIMIT),
        name="merge",
    )(xf, y_tm, ga, gb, ob, mod3, wglu, wout)


def _moe_body(x_ref, mod_ref, wr_ref, br_ref, wgu_ref, wd_ref, o_ref, hb_ref, comb_ref, acc_ref):
    e = pl.program_id(1)
    tm = x_ref.shape[0]
    lane = lax.broadcasted_iota(I32, (tm, LANE), 1).astype(F32)

    @pl.when(e == 0)
    def _():
        h2 = _modulate(_rms(x_ref[...], D_MODEL), mod_ref[...], 4, 3)
        hb_ref[...] = h2.astype(BF16)
        lg = _dot_hilo(h2, wr_ref[...]) + br_ref[...]
        neg = -jnp.inf
        gl = jnp.where(lane < N_GROUPS, lg, neg)
        gmax = jnp.max(gl, axis=-1, keepdims=True)
        g_idx = jnp.min(jnp.where(gl == gmax, lane, float(LANE)), axis=-1, keepdims=True)
        g_w = 1.0 / jnp.sum(jnp.where(lane < N_GROUPS, jnp.exp(lg - gmax), 0.0),
                            axis=-1, keepdims=True)
        lo = N_GROUPS + EXPERTS_PER_GROUP * g_idx
        el = jnp.where((lane >= lo) & (lane < lo + EXPERTS_PER_GROUP), lg, neg)
        v1 = jnp.max(el, axis=-1, keepdims=True)
        i1 = jnp.min(jnp.where(el == v1, lane, float(LANE)), axis=-1, keepdims=True)
        el2 = jnp.where(lane == i1, neg, el)
        v2 = jnp.max(el2, axis=-1, keepdims=True)
        i2 = jnp.min(jnp.where(el2 == v2, lane, float(LANE)), axis=-1, keepdims=True)
        e21 = jnp.exp(v2 - v1)
        w1 = g_w / (1.0 + e21)
        w2 = g_w * e21 / (1.0 + e21)
        comb_ref[...] = jnp.where(lane == i1, w1, 0.0) + jnp.where(lane == i2, w2, 0.0)
        acc_ref[...] = jnp.zeros_like(acc_ref)

    gu = _dot(hb_ref[...], wgu_ref[0])
    g = gu[:, :EXPERT_DIM]
    hid = g * jax.nn.sigmoid(g) * gu[:, EXPERT_DIM:]
    col = jnp.sum(jnp.where(lane == (e + N_GROUPS).astype(F32), comb_ref[...], 0.0),
                  axis=-1, keepdims=True)
    acc_ref[...] += _dot((hid * col).astype(BF16), wd_ref[0])

    @pl.when(e == N_EXPERTS - 1)
    def _():
        o_ref[...] = x_ref[...] + _gate_mul(acc_ref[...], mod_ref[...], 5)


def _moe(x1, mod3, wr, br, wgu, wd, t_len, tm):
    n = x1.shape[0]
    return pl.pallas_call(
        _moe_body,
        grid=(n // tm, N_EXPERTS),
        in_specs=[pl.BlockSpec((tm, D_MODEL), lambda i, e: (i, 0)),
                  _mod_spec(tm, t_len),
                  pl.BlockSpec((D_MODEL, LANE), lambda i, e: (0, 0)),
                  pl.BlockSpec((1, LANE), lambda i, e: (0, 0)),
                  pl.BlockSpec((1, D_MODEL, 2 * EXPERT_DIM), lambda i, e: (e, 0, 0)),
                  pl.BlockSpec((1, EXPERT_DIM, D_MODEL), lambda i, e: (e, 0, 0))],
        out_specs=pl.BlockSpec((tm, D_MODEL), lambda i, e: (i, 0)),
        out_shape=jax.ShapeDtypeStruct((n, D_MODEL), F32),
        scratch_shapes=[pltpu.VMEM((tm, D_MODEL), BF16), pltpu.VMEM((tm, LANE), F32),
                        pltpu.VMEM((tm, D_MODEL), F32)],
        compiler_params=pltpu.CompilerParams(dimension_semantics=("parallel", "arbitrary"),
                                             vmem_limit_bytes=VMEM_LIMIT),
        name="moe",
    )(x1, mod3, wr, br, wgu, wd)


def _prep_params(w_in, ssm_a_re, ssm_a_im, ssm_log_dt, ssm_b_re, ssm_b_im, ssm_c_re, ssm_c_im, ssm_d,
                 w_glu, q_gain, k_gain, kidx_gain, w_attn_out, w_out, w_rg, b_rg, w_re, b_re,
                 w_gate, w_up, w_down, max_batch):
    offs = np.cumsum([0, SSM_WIDTH, N_HEADS * HEAD_DIM, HEAD_DIM, HEAD_DIM, IDX_HEADS * IDX_DIM,
                      IDX_DIM, IDX_HEADS, D_MODEL, D_MODEL])
    seg = lambda i: w_in[:, offs[i]:offs[i + 1]]
    padto = lambda a, w: jnp.pad(a, ((0, 0), (0, w - a.shape[-1])))
    wq = jnp.pad(seg(1).reshape(D_MODEL, N_HEADS, HEAD_DIM), ((0, 0), (0, 0), (0, LANE - HEAD_DIM)))
    wqi = jnp.tile(seg(4).reshape(D_MODEL, IDX_HEADS, IDX_DIM), (1, 1, LANE // IDX_DIM))
    w_pad = jnp.concatenate(
        [seg(0), wq.reshape(D_MODEL, N_HEADS * LANE), padto(seg(2), LANE), padto(seg(3), LANE),
         wqi.reshape(D_MODEL, IDX_HEADS * LANE), jnp.tile(seg(5), (1, LANE // IDX_DIM)),
         padto(seg(6), LANE), seg(7), seg(8)], axis=1).astype(BF16)
    qg = padto((q_gain * HEAD_DIM ** -0.5)[None, :], LANE)
    kg = padto(k_gain[None, :], LANE)
    ig = jnp.tile(kidx_gain[None, :], (1, LANE // IDX_DIM))

    dt = jnp.exp(ssm_log_dt)[:, None]
    decay = jnp.exp(ssm_a_re * dt)
    lam_re = decay * jnp.cos(ssm_a_im * dt)
    lam_im = decay * jnp.sin(ssm_a_im * dt)
    den = ssm_a_re * ssm_a_re + ssm_a_im * ssm_a_im
    num_re = lam_re - 1.0
    coef_re = (num_re * ssm_a_re + lam_im * ssm_a_im) / den
    coef_im = (lam_im * ssm_a_re - num_re * ssm_a_im) / den
    bb_re = coef_re[..., None] * ssm_b_re - coef_im[..., None] * ssm_b_im
    bb_im = coef_re[..., None] * ssm_b_im + coef_im[..., None] * ssm_b_re
    eye = jnp.eye(SSM_GROUPS, dtype=F32)

    def in_map(bb):
        return jnp.einsum('gpj,gh->gjhp', bb, eye).reshape(SSM_WIDTH, SSM_LANES).astype(BF16)

    def out_map(c):
        return jnp.einsum('gjp,gh->gphj', c, eye).reshape(SSM_LANES, SSM_WIDTH).astype(BF16)

    bcast = lambda a: jnp.broadcast_to(a.reshape(1, SSM_LANES), (max_batch, SSM_LANES))
    ssm = (in_map(bb_re), in_map(bb_im), out_map(ssm_c_re), out_map(-ssm_c_im),
           bcast(lam_re), bcast(lam_im), ssm_d[None, :])

    wr = jnp.pad(jnp.concatenate([w_rg, w_re], axis=1), ((0, 0), (0, LANE - N_GROUPS - N_EXPERTS)))
    br = jnp.pad(jnp.concatenate([b_rg, b_re]), (0, LANE - N_GROUPS - N_EXPERTS))[None, :]
    wgu = jnp.concatenate([w_gate, w_up], axis=-1).astype(BF16)
    return dict(w_pad=w_pad, qg=qg, kg=kg, ig=ig, ssm=ssm,
                wglu=w_glu.astype(BF16), wo=w_attn_out.astype(BF16), wout=w_out.astype(BF16),
                wr=wr, br=br, wgu=wgu, wd=w_down.astype(BF16))


def _layer(x, mod3, p, past, *, tm_tok, tm_moe, tc, tq, kb):
    nb, t_len, _ = x.shape
    n = nb * t_len
    cb = 2 * kb
    reps = max(tq // t_len, 1)
    kbv = min(cb, t_len)
    xf = x.reshape(n, D_MODEL)
    u_tm, q, qc, k, v, ki, kp, kc, vt, wt, ga, gb = _proj(
        xf, mod3, p["w_pad"], p["qg"], p["kg"], p["ig"], nb, t_len, tm_tok, reps, kbv)
    k3 = k.reshape(nb, t_len, HEAD_DIM)
    v3 = v.reshape(nb, t_len, HEAD_DIM)
    ki3 = ki.reshape(nb, t_len, IDX_DIM)
    kp = kp.reshape(nb, t_len, LANE)
    kc = kc.reshape(nb, t_len, LANE)

    if past is None:
        h0r = jnp.zeros((nb, SSM_LANES), F32)
        h0i = h0r
        s_real, q_off = t_len, 0
    else:
        past_k, past_v, past_ki, h0_re, h0_im = past
        h0r = h0_re.reshape(nb, SSM_LANES)
        h0i = h0_im.reshape(nb, SSM_LANES)
        q_off = past_k.shape[1]
        s_real = q_off + t_len
        s_pad = -(-s_real // cb) * cb
        fill = lambda a: jnp.pad(a, ((0, 0), (0, s_pad - s_real), (0, 0)))
        pk = jnp.pad(past_k.astype(BF16), ((0, 0), (0, 0), (0, LANE - HEAD_DIM)))
        kp = fill(jnp.concatenate([pk, kp], axis=1))
        pi_hi, pi_lo = _split_bf16(past_ki)
        kc = fill(jnp.concatenate([jnp.concatenate([pi_hi, pi_lo, pi_hi, pi_lo], axis=-1), kc], axis=1))
        vt_all = jnp.concatenate([past_v.astype(BF16).transpose(0, 2, 1),
                                  vt.transpose(0, 2, 1, 3).reshape(nb, HEAD_DIM, t_len)], axis=2)
        vt_all = jnp.pad(vt_all, ((0, 0), (0, 0), (0, s_pad - s_real)))
        vt = vt_all.reshape(nb, HEAD_DIM, s_pad // cb, cb).transpose(0, 2, 1, 3)

    y_tm, s_re, s_im = _s5(u_tm.reshape(n, SSM_WIDTH), h0r, h0i, p["ssm"], nb, t_len, tc)
    ob = _dsa(q, qc, wt, kc, kp, vt, p["wo"], tq=tq, kb=kb, s_real=s_real, q_off=q_off,
              q_wrap=t_len if reps > 1 else tq)
    ob = ob[:, :t_len].reshape(n, D_MODEL)
    x1 = _merge(xf, y_tm.reshape(t_len, nb * SSM_WIDTH), ga, gb, ob, mod3, p["wglu"], p["wout"],
                t_len, tm_tok)
    x2 = _moe(x1, mod3, p["wr"], p["br"], p["wgu"], p["wd"], t_len, tm_moe)
    return (x2.reshape(nb, t_len, D_MODEL), k3, v3, ki3,
            s_re.reshape(nb, SSM_GROUPS, SSM_STATE), s_im.reshape(nb, SSM_GROUPS, SSM_STATE))


def kernel(x_prompt, x_sample, cache_k, cache_v, cache_kidx, state_ssm_re, state_ssm_im, c_prompt, c_sample, w_ada, b_ada, w_in, ssm_a_re, ssm_a_im, ssm_log_dt, ssm_b_re, ssm_b_im, ssm_c_re, ssm_c_im, ssm_d, w_glu, q_gain, k_gain, kidx_gain, w_attn_out, w_out, w_route_group, b_route_group, w_route_expert, b_route_expert, w_gate, w_up, w_down):
    depth = w_ada.shape[0]
    nbp = x_prompt.shape[0]
    nbs = x_sample.shape[0]
    xp, xs = x_prompt, x_sample
    outs = [[] for _ in range(10)]
    for l in range(depth):
        p = _prep_params(w_in[l], ssm_a_re[l], ssm_a_im[l], ssm_log_dt[l], ssm_b_re[l], ssm_b_im[l],
                         ssm_c_re[l], ssm_c_im[l], ssm_d[l], w_glu[l], q_gain[l], k_gain[l],
                         kidx_gain[l], w_attn_out[l], w_out[l], w_route_group[l], b_route_group[l],
                         w_route_expert[l], b_route_expert[l], w_gate[l], w_up[l], w_down[l],
                         max(nbp, nbs))
        mod = _adaln(jnp.concatenate([c_prompt, c_sample], axis=0), w_ada[l], b_ada[l][None, :])
        mod3 = mod[:, None, :]
        xp, kp, vp, kip, srp, sip = _layer(xp, mod3[:nbp], p, None,
                                           tm_tok=512, tm_moe=1024, tc=32, tq=128, kb=256)
        past = (cache_k[l], cache_v[l], cache_kidx[l], state_ssm_re[l], state_ssm_im[l])
        xs, ks, vs, kis, srs, sis = _layer(xs, mod3[nbp:], p, past,
                                           tm_tok=512, tm_moe=512, tc=64, tq=128, kb=384)
        for lst, val in zip(outs, (kp, vp, kip, srp, sip, ks, vs, kis, srs, sis)):
            lst.append(val)
    return (xp, xs) + tuple(jnp.stack(o) for o in outs)
```

```python
import functools

import jax
import jax.numpy as jnp
import numpy as np
from jax import lax
from jax.experimental import pallas as pl
from jax.experimental.pallas import tpu as pltpu

F32 = jnp.float32
BF16 = jnp.bfloat16
I32 = jnp.int32

D_MODEL = 1024
CHUNK_SHIFT = 6
EPS = 1e-6
NEG_INF = -1e30
SSM_WIDTH = 512
SSM_GROUP = 16
SSM_GROUPS = 32
SSM_STATE = 64
SSM_LANES = SSM_GROUPS * SSM_STATE
S5_EXPAND = SSM_STATE // SSM_GROUP
S5_IN_BLOCK = 256
N_HEADS = 8
HEAD_DIM = 64
IDX_HEADS = 8
IDX_DIM = 32
IDX_WEIGHT_SCALE = (IDX_HEADS * IDX_DIM) ** -0.5
TOPK_MAX = 256
N_GROUPS = 4
EXPERTS_PER_GROUP = 4
N_EXPERTS = 16
EXPERT_DIM = 256
LANE = 128
INT_MIN = -(2 ** 31)
VMEM_LIMIT = 56 * 1024 * 1024

C_U = 0
C_Q = C_U + SSM_WIDTH
C_K = C_Q + N_HEADS * LANE
C_V = C_K + LANE
C_QI = C_V + LANE
C_KI = C_QI + IDX_HEADS * LANE
C_WI = C_KI + LANE
C_GA = C_WI + LANE
C_GB = C_GA + D_MODEL
C_END = C_GB + D_MODEL


def _dot(a, b):
    return jnp.dot(a, b, preferred_element_type=F32)


def _dot_nt(a, b):
    return lax.dot_general(a, b, (((1,), (1,)), ((), ())), preferred_element_type=F32)


def _split_bf16(x):
    hi = x.astype(BF16)
    lo = (x - hi.astype(F32)).astype(BF16)
    return hi, lo


def _dot_hilo(a, b):
    ah, al = _split_bf16(a)
    bh, bl = _split_bf16(b)
    return _dot(ah, bh) + _dot(ah, bl) + _dot(al, bh)


def _tiling(tm, t_len):
    return (t_len // tm, 1) if tm < t_len else (1, tm // t_len)


def _mod_spec(tm, t_len):
    per, nbt = _tiling(tm, t_len)
    return pl.BlockSpec((nbt, 1, 6 * D_MODEL), lambda i, *_: (i // per, 0, 0))


def _tm_spec(tm, t_len):
    per, nbt = _tiling(tm, t_len)
    return pl.BlockSpec((tm // nbt, nbt * SSM_WIDTH), lambda i, *_: (i % per, i // per))


def _modulate(x, mod, scale_seg, shift_seg):
    nb = mod.shape[0]
    tm = x.shape[0]
    sc = mod[:, :, scale_seg * D_MODEL:(scale_seg + 1) * D_MODEL]
    sh = mod[:, :, shift_seg * D_MODEL:(shift_seg + 1) * D_MODEL]
    x3 = x.reshape(nb, tm // nb, D_MODEL)
    return (x3 * (1.0 + sc) + sh).reshape(tm, D_MODEL)


def _gate_mul(x, mod, seg):
    nb = mod.shape[0]
    tm = x.shape[0]
    g = mod[:, :, seg * D_MODEL:(seg + 1) * D_MODEL]
    return (x.reshape(nb, tm // nb, D_MODEL) * g).reshape(tm, D_MODEL)


def _rms(x, n):
    return x * lax.rsqrt(jnp.sum(x * x, axis=-1, keepdims=True) * (1.0 / n) + EPS)


def _adaln_body(c_ref, w_ref, b_ref, o_ref):
    c = c_ref[...]
    s = c * jax.nn.sigmoid(c)
    o_ref[...] = _dot_hilo(s, w_ref[...]) + b_ref[...]


def _adaln(c, w, b):
    nb = c.shape[0]
    n = w.shape[1]
    bn = 512
    return pl.pallas_call(
        _adaln_body,
        grid=(n // bn,),
        in_specs=[pl.BlockSpec((nb, D_MODEL), lambda j: (0, 0)),
                  pl.BlockSpec((D_MODEL, bn), lambda j: (0, j)),
                  pl.BlockSpec((1, bn), lambda j: (0, j))],
        out_specs=pl.BlockSpec((nb, bn), lambda j: (0, j)),
        out_shape=jax.ShapeDtypeStruct((nb, n), F32),
        name="adaln",
    )(c, w, b)


def _proj_body(x_ref, mod_ref, w_ref, qg_ref, kg_ref, ig_ref,
               u_ref, q_ref, qc_ref, k_ref, v_ref, ki_ref, kp_ref, kc_ref, vt_ref, wt_ref,
               ga_ref, gb_ref, *, nbt, reps, kbv):
    tm = x_ref.shape[0]
    tt = tm // nbt
    rows = lambda b: slice(tt * b, tt * (b + 1))
    hsl = lambda h: slice(LANE * h, LANE * (h + 1))
    lane = lax.broadcasted_iota(I32, (tm, LANE), 1)
    hb = _modulate(_rms(x_ref[...], D_MODEL), mod_ref[...], 1, 0).astype(BF16)

    u = _dot(hb, w_ref[:, C_U:C_Q])
    for b in range(nbt):
        u_ref[:, SSM_WIDTH * b:SSM_WIDTH * (b + 1)] = u[rows(b), :]

    def put_heads(ref, pieces):
        for b in range(nbt):
            for h in range(N_HEADS):
                for r in range(reps):
                    ref[b, h, tt * r:tt * (r + 1), :] = pieces[h][rows(b), :]

    q = _dot(hb, w_ref[:, C_Q:C_K])
    put_heads(q_ref, [(_rms(q[:, hsl(h)], HEAD_DIM) * qg_ref[...]).astype(BF16)
                      for h in range(N_HEADS)])

    qi_hi, qi_lo = _split_bf16(_dot(hb, w_ref[:, C_QI:C_KI]))
    put_heads(qc_ref, [jnp.where(lane < 2 * IDX_DIM, qi_hi[:, hsl(h)], qi_lo[:, hsl(h)])
                       for h in range(IDX_HEADS)])

    kn = _rms(_dot(hb, w_ref[:, C_K:C_V]), HEAD_DIM) * kg_ref[...]
    k_ref[...] = kn[:, :HEAD_DIM]
    kp_ref[...] = kn.astype(BF16)

    vfull = _dot(hb, w_ref[:, C_V:C_QI])
    v_ref[...] = vfull[:, :HEAD_DIM]
    vt = vfull.T[:HEAD_DIM].astype(BF16)
    for b in range(nbt):
        for c in range(tt // kbv):
            vt_ref[b, c] = vt[:, tt * b + kbv * c:tt * b + kbv * (c + 1)]

    kin = _rms(_dot(hb, w_ref[:, C_KI:C_WI]), LANE) * ig_ref[...]
    ki_ref[...] = kin[:, :IDX_DIM]
    ki_hi, ki_lo = _split_bf16(kin)
    kc_ref[...] = jnp.where(((lane >> 5) & 1) == 1, ki_lo, ki_hi)

    wt = (_dot(hb, w_ref[:, C_WI:C_GA]) * IDX_WEIGHT_SCALE).T[:IDX_HEADS]
    for b in range(nbt):
        for r in range(reps):
            wt_ref[b, :, tt * r:tt * (r + 1)] = wt[:, rows(b)]

    ga_ref[...] = jax.nn.sigmoid(_dot(hb, w_ref[:, C_GA:C_GB]))
    gb_ref[...] = jax.nn.sigmoid(_dot(hb, w_ref[:, C_GB:C_END]))


def _proj(xf, mod3, w_pad, qg, kg, ig, nb, t_len, tm, reps, kbv):
    n = xf.shape[0]
    per, nbt = _tiling(tm, t_len)
    tt = tm // nbt
    row = lambda w: pl.BlockSpec((tm, w), lambda i: (i, 0))
    const = lambda s: pl.BlockSpec(s, lambda i: (0, 0))
    head_spec = pl.BlockSpec((nbt, N_HEADS, tt * reps, LANE), lambda i: (i // per, 0, i % per, 0))
    head_shape = jax.ShapeDtypeStruct((nb, N_HEADS, t_len * reps, LANE), BF16)
    outs = [
        (_tm_spec(tm, t_len), jax.ShapeDtypeStruct((t_len, nb * SSM_WIDTH), F32)),
        (head_spec, head_shape), (head_spec, head_shape),
        (row(HEAD_DIM), jax.ShapeDtypeStruct((n, HEAD_DIM), F32)),
        (row(HEAD_DIM), jax.ShapeDtypeStruct((n, HEAD_DIM), F32)),
        (row(IDX_DIM), jax.ShapeDtypeStruct((n, IDX_DIM), F32)),
        (row(LANE), jax.ShapeDtypeStruct((n, LANE), BF16)),
        (row(LANE), jax.ShapeDtypeStruct((n, LANE), BF16)),
        (pl.BlockSpec((nbt, tt // kbv, HEAD_DIM, kbv), lambda i: (i // per, i % per, 0, 0)),
         jax.ShapeDtypeStruct((nb, t_len // kbv, HEAD_DIM, kbv), BF16)),
        (pl.BlockSpec((nbt, IDX_HEADS, tt * reps), lambda i: (i // per, 0, i % per)),
         jax.ShapeDtypeStruct((nb, IDX_HEADS, t_len * reps), F32)),
        (row(D_MODEL), jax.ShapeDtypeStruct((n, D_MODEL), F32)),
        (row(D_MODEL), jax.ShapeDtypeStruct((n, D_MODEL), F32)),
    ]
    return pl.pallas_call(
        functools.partial(_proj_body, nbt=nbt, reps=reps, kbv=kbv),
        grid=(n // tm,),
        in_specs=[row(D_MODEL), _mod_spec(tm, t_len), const((D_MODEL, C_END)),
                  const((1, LANE)), const((1, LANE)), const((1, LANE))],
        out_specs=[s for s, _ in outs],
        out_shape=[s for _, s in outs],
        compiler_params=pltpu.CompilerParams(dimension_semantics=("parallel",),
                                             vmem_limit_bytes=VMEM_LIMIT),
        name="proj",
    )(xf, mod3, w_pad, qg, kg, ig)


def _s5_body(u_ref, h0r_ref, h0i_ref, bre_ref, bim_ref, cre_ref, cim_ref, lre_ref, lim_ref, d_ref,
             y_ref, sr_ref, si_ref, xr, xi, hr, hi, *, nb, tc, lane_chunk):
    @pl.when(pl.program_id(0) == 0)
    def _():
        hr[...] = h0r_ref[...]
        hi[...] = h0i_ref[...]

    u = u_ref[...]
    ub = u.astype(BF16)
    for c in range(SSM_WIDTH // S5_IN_BLOCK):
        cin = slice(S5_IN_BLOCK * c, S5_IN_BLOCK * (c + 1))
        cst = slice(S5_IN_BLOCK * S5_EXPAND * c, S5_IN_BLOCK * S5_EXPAND * (c + 1))
        xr[:, cst] = _dot(ub[:, cin], bre_ref[cin, cst])
        xi[:, cst] = _dot(ub[:, cin], bim_ref[cin, cst])

    for lc in range(SSM_LANES // lane_chunk):
        cols = slice(lc * lane_chunk, (lc + 1) * lane_chunk)

        def step(t, carry, cols=cols):
            a, b = carry
            rows = pl.ds(pl.multiple_of(t * nb, nb), nb)
            lr = lre_ref[:, cols]
            li = lim_ref[:, cols]
            na = lr * a - li * b + xr[rows, cols]
            nb_ = lr * b + li * a + xi[rows, cols]
            xr[rows, cols] = na
            xi[rows, cols] = nb_
            return na, nb_

        a, b = lax.fori_loop(0, tc, step, (hr[:, cols], hi[:, cols]), unroll=2)
        hr[:, cols] = a
        hi[:, cols] = b

    for c in range(SSM_WIDTH // LANE):
        cy = slice(LANE * c, LANE * (c + 1))
        cst = slice(LANE * S5_EXPAND * c, LANE * S5_EXPAND * (c + 1))
        y = (_dot(xr[:, cst].astype(BF16), cre_ref[cst, cy])
             + _dot(xi[:, cst].astype(BF16), cim_ref[cst, cy]))
        y_ref[:, cy] = y + d_ref[:, cy] * u[:, cy]
    sr_ref[...] = hr[...]
    si_ref[...] = hi[...]


def _s5(u_tm, h0r, h0i, ssm, nb, t_len, tc):
    bre, bim, cre, cim, lre, lim, dsk = ssm
    rows = tc * nb
    const = lambda s: pl.BlockSpec(s, lambda c: (0, 0))
    return pl.pallas_call(
        functools.partial(_s5_body, nb=nb, tc=tc, lane_chunk=512),
        grid=(t_len // tc,),
        in_specs=[pl.BlockSpec((rows, SSM_WIDTH), lambda c: (c, 0)),
                  const((nb, SSM_LANES)), const((nb, SSM_LANES)),
                  const((SSM_WIDTH, SSM_LANES)), const((SSM_WIDTH, SSM_LANES)),
                  const((SSM_LANES, SSM_WIDTH)), const((SSM_LANES, SSM_WIDTH)),
                  const((nb, SSM_LANES)), const((nb, SSM_LANES)), const((1, SSM_WIDTH))],
        out_specs=[pl.BlockSpec((rows, SSM_WIDTH), lambda c: (c, 0)),
                   const((nb, SSM_LANES)), const((nb, SSM_LANES))],
        out_shape=[jax.ShapeDtypeStruct((t_len * nb, SSM_WIDTH), F32),
                   jax.ShapeDtypeStruct((nb, SSM_LANES), F32),
                   jax.ShapeDtypeStruct((nb, SSM_LANES), F32)],
        scratch_shapes=[pltpu.VMEM((rows, SSM_LANES), F32), pltpu.VMEM((rows, SSM_LANES), F32),
                        pltpu.VMEM((nb, SSM_LANES), F32), pltpu.VMEM((nb, SSM_LANES), F32)],
        compiler_params=pltpu.CompilerParams(dimension_semantics=("arbitrary",),
                                             vmem_limit_bytes=VMEM_LIMIT),
        name="s5",
    )(u_tm, h0r, h0i, bre, bim, cre, cim, lre[:nb], lim[:nb], dsk)


def _dsa_body(q_ref, qc_ref, wt_ref, kc_ref, k_ref, vt_ref, wo_ref, o_ref,
              sc_ref, lg_ref, acc_ref, at_ref, st0_ref, st1_ref, bias_ref,
              *, tq, kb, s_real, s_pad, q_off, q_wrap, topk):
    j = pl.program_id(1)
    hq = N_HEADS * tq
    n_all = s_pad // kb
    cb = 2 * kb
    lane_q = lax.broadcasted_iota(I32, (1, tq), 1)
    qpos = q_off + j * tq + (lane_q & (q_wrap - 1))
    qchunk = qpos >> CHUNK_SHIFT
    last_end = (((q_off + j * tq + min(tq, q_wrap) - 1) >> CHUNK_SHIFT) + 1) << CHUNK_SHIFT
    npair = jnp.minimum(n_all // 2, lax.div(last_end + (cb - 1), cb))

    def blk(b):
        return pl.ds(pl.multiple_of(b * kb, kb), kb)

    def pair(i):
        return pl.ds(pl.multiple_of(i * cb, cb), cb)

    def admissible(b):
        kpos = b * kb + lax.broadcasted_iota(I32, (kb, tq), 0)
        return ((kpos >> CHUNK_SHIFT) <= qchunk) & (kpos < s_real), kpos

    def fold8(x, op):
        return op(x.reshape(x.shape[0] // 8, 8, x.shape[-1]), axis=0)

    hsl = lambda h: slice(h * tq, (h + 1) * tq)

    def staged(mm, consume, carry):
        st0_ref[...] = mm(0)

        def body(i, c):
            b0 = 2 * i
            st1_ref[...] = mm(b0 + 1)
            c = consume(b0, st0_ref, c)
            st0_ref[...] = mm(jnp.minimum(b0 + 2, n_all - 1))
            return consume(b0 + 1, st1_ref, c)

        return lax.fori_loop(0, npair, body, carry)

    qcs = qc_ref[0].reshape(hq, LANE)
    wt = wt_ref[0]

    rc = 64

    def sub(b, r):
        return pl.ds(pl.multiple_of(b * kb + r * rc, rc), rc)

    def admissible_rows(b, r):
        kpos = b * kb + r * rc + lax.broadcasted_iota(I32, (rc, tq), 0)
        return ((kpos >> CHUNK_SHIFT) <= qchunk) & (kpos < s_real), kpos

    def score_consume(b, st, carry):
        for r in range(kb // rc):
            sc = jnp.zeros((rc, tq), F32)
            for h in range(IDX_HEADS):
                sc = sc + wt[h:h + 1, :] * jnp.maximum(st[r * rc:(r + 1) * rc, hsl(h)], 0.0)
            sc_ref[sub(b, r), :] = jnp.where(admissible_rows(b, r)[0], sc, NEG_INF)
        return carry

    staged(lambda b: _dot_nt(kc_ref[0, blk(b), :], qcs), score_consume, 0)

    def as_float(u):
        s = u ^ INT_MIN
        return lax.bitcast_convert_type(s ^ ((s >> 31) & 0x7FFFFFFF), F32)

    def count(pred):
        def body(i, acc):
            for r in range(cb // 32):
                row0 = i * cb + 32 * r
                kpos = lambda row0=row0: row0 + lax.broadcasted_iota(I32, (32, tq), 0)
                x = sc_ref[pl.ds(pl.multiple_of(row0, 32), 32), :]
                acc = acc + jnp.where(pred(x, kpos), 1.0, 0.0)
            return acc
        acc = lax.fori_loop(0, npair, body, jnp.zeros((32, tq), F32))
        return jnp.sum(acc, axis=0, keepdims=True)

    def bit_body(i, prefix):
        cand = prefix | (jnp.int32(1) << (31 - i))
        cf = as_float(cand)
        cnt = count(lambda x, kpos: x >= cf)
        return jnp.where(cnt >= topk, cand, prefix)

    thr = as_float(lax.fori_loop(0, 32, bit_body, jnp.zeros((1, tq), I32)))

    n_gt = count(lambda x, kpos: x > thr)
    n_eq = count(lambda x, kpos: x == thr)
    need = topk - n_gt
    any_over = jnp.max(n_eq - need) > 0.0
    idx_bits = int(s_pad).bit_length()

    def tie_search():
        def tbody(i, lim):
            cand = lim | (jnp.int32(1) << (idx_bits - 1 - i))
            n = count(lambda x, kpos: (x == thr) & (kpos() < cand))
            return jnp.where(n <= need, cand, lim)
        return lax.fori_loop(0, idx_bits, tbody, jnp.zeros((1, tq), I32))

    lim = lax.cond(any_over, tie_search, lambda: jnp.full((1, tq), 2 ** idx_bits - 1, I32))

    qs = q_ref[0].reshape(hq, LANE)

    def logits_pair(i, m):
        for half in range(2):
            b = 2 * i + half
            for r in range(kb // rc):
                x = sc_ref[sub(b, r), :]
                adm, kpos = admissible_rows(b, r)
                sel = ((x > thr) | ((x == thr) & (kpos < lim))) & adm
                bias_ref[half * kb + r * rc:half * kb + (r + 1) * rc, :] = jnp.where(sel, 0.0, NEG_INF)
        tops = [m[:, hsl(h)] for h in range(N_HEADS)]
        for half in range(2):
            b = 2 * i + half
            raw = _dot_nt(k_ref[0, blk(b), :], qs)
            for h in range(N_HEADS):
                lh = raw[:, hsl(h)] + bias_ref[half * kb:(half + 1) * kb, :]
                lg_ref[blk(b), hsl(h)] = lh
                tops[h] = jnp.maximum(tops[h], fold8(lh, jnp.max))
        return jnp.concatenate(tops, axis=1)

    m8 = lax.fori_loop(0, npair, logits_pair, jnp.full((8, hq), NEG_INF, F32))
    m = jnp.max(m8, axis=0, keepdims=True)

    acc_ref[...] = jnp.zeros_like(acc_ref)

    def pv_pair(i, ssum):
        p = jnp.exp2(lg_ref[pair(i), :] - m)
        acc_ref[...] += _dot(vt_ref[0, i], p.astype(BF16))
        return ssum + fold8(p, jnp.sum)

    s8 = lax.fori_loop(0, npair, pv_pair, jnp.zeros((8, hq), F32))
    out_t = acc_ref[...] / jnp.sum(s8, axis=0, keepdims=True)
    for h in range(N_HEADS):
        at_ref[HEAD_DIM * h:HEAD_DIM * (h + 1), :] = out_t[:, hsl(h)]
    o_ref[0] = _dot(at_ref[...].T.astype(BF16), wo_ref[...])


def _dsa(q, qc, wt, kc, kp, vt, wo, *, tq, kb, s_real, q_off, q_wrap):
    nb, _, t_len, _ = q.shape
    s_pad = kc.shape[1]
    cb = 2 * kb
    topk = min(TOPK_MAX, s_real // 4)
    hq = N_HEADS * tq
    qspec = pl.BlockSpec((1, N_HEADS, tq, LANE), lambda b, j: (b, 0, j, 0))
    return pl.pallas_call(
        functools.partial(_dsa_body, tq=tq, kb=kb, s_real=s_real, s_pad=s_pad, q_off=q_off,
                          q_wrap=q_wrap, topk=float(topk)),
        grid=(nb, t_len // tq),
        in_specs=[qspec, qspec,
                  pl.BlockSpec((1, IDX_HEADS, tq), lambda b, j: (b, 0, j)),
                  pl.BlockSpec((1, s_pad, LANE), lambda b, j: (b, 0, 0)),
                  pl.BlockSpec((1, s_pad, LANE), lambda b, j: (b, 0, 0)),
                  pl.BlockSpec((1, s_pad // cb, HEAD_DIM, cb), lambda b, j: (b, 0, 0, 0)),
                  pl.BlockSpec((N_HEADS * HEAD_DIM, D_MODEL), lambda b, j: (0, 0))],
        out_specs=pl.BlockSpec((1, tq, D_MODEL), lambda b, j: (b, j, 0)),
        out_shape=jax.ShapeDtypeStruct((nb, t_len, D_MODEL), F32),
        scratch_shapes=[pltpu.VMEM((s_pad, tq), F32), pltpu.VMEM((s_pad, hq), F32),
                        pltpu.VMEM((HEAD_DIM, hq), F32), pltpu.VMEM((N_HEADS * HEAD_DIM, tq), F32),
                        pltpu.VMEM((kb, hq), F32), pltpu.VMEM((kb, hq), F32),
                        pltpu.VMEM((cb, tq), F32)],
        compiler_params=pltpu.CompilerParams(dimension_semantics=("parallel", "parallel"),
                                             vmem_limit_bytes=VMEM_LIMIT),
        name="dsa",
    )(q, qc, wt, kc, kp, vt, wo)


def _merge_body(x_ref, y_ref, ga_ref, gb_ref, ob_ref, mod_ref, wglu_ref, wout_ref, o_ref, *, nbt):
    y = y_ref[...]
    if nbt > 1:
        y = jnp.concatenate([y[:, SSM_WIDTH * b:SSM_WIDTH * (b + 1)] for b in range(nbt)], axis=0)
    z = jax.nn.gelu(y, approximate=True).astype(BF16)
    glu = _dot(z, wglu_ref[...])
    out_a = glu[:, :D_MODEL] * jax.nn.sigmoid(glu[:, D_MODEL:])
    merged = ga_ref[...] * out_a + gb_ref[...] * ob_ref[...]
    upd = _dot(merged.astype(BF16), wout_ref[...])
    o_ref[...] = x_ref[...] + _gate_mul(upd, mod_ref[...], 2)


def _merge(xf, y_tm, ga, gb, ob, mod3, wglu, wout, t_len, tm):
    n = xf.shape[0]
    row = lambda w: pl.BlockSpec((tm, w), lambda i: (i, 0))
    const = lambda s: pl.BlockSpec(s, lambda i: (0, 0))
    return pl.pallas_call(
        functools.partial(_merge_body, nbt=_tiling(tm, t_len)[1]),
        grid=(n // tm,),
        in_specs=[row(D_MODEL), _tm_spec(tm, t_len), row(D_MODEL), row(D_MODEL), row(D_MODEL),
                  _mod_spec(tm, t_len), const((SSM_WIDTH, 2 * D_MODEL)), const((D_MODEL, D_MODEL))],
        out_specs=row(D_MODEL),
        out_shape=jax.ShapeDtypeStruct((n, D_MODEL), F32),
        compiler_params=pltpu.CompilerParams(dimension_semantics=("parallel",),
                                             vmem_limit_bytes=VMEM_LIMIT),
        name="merge",
    )(xf, y_tm, ga, gb, ob, mod3, wglu, wout)


def _moe_body(x_ref, mod_ref, wr_ref, br_ref, wgu_ref, wd_ref, o_ref, hb_ref, comb_ref, acc_ref):
    e = pl.program_id(1)
    tm = x_ref.shape[0]
    lane = lax.broadcasted_iota(I32, (tm, LANE), 1).astype(F32)

    @pl.when(e == 0)
    def _():
        h2 = _modulate(_rms(x_ref[...], D_MODEL), mod_ref[...], 4, 3)
        hb_ref[...] = h2.astype(BF16)
        lg = _dot_hilo(h2, wr_ref[...]) + br_ref[...]
        neg = -jnp.inf
        gl = jnp.where(lane < N_GROUPS, lg, neg)
        gmax = jnp.max(gl, axis=-1, keepdims=True)
        g_idx = jnp.min(jnp.where(gl == gmax, lane, float(LANE)), axis=-1, keepdims=True)
        g_w = 1.0 / jnp.sum(jnp.where(lane < N_GROUPS, jnp.exp(lg - gmax), 0.0),
                            axis=-1, keepdims=True)
        lo = N_GROUPS + EXPERTS_PER_GROUP * g_idx
        el = jnp.where((lane >= lo) & (lane < lo + EXPERTS_PER_GROUP), lg, neg)
        v1 = jnp.max(el, axis=-1, keepdims=True)
        i1 = jnp.min(jnp.where(el == v1, lane, float(LANE)), axis=-1, keepdims=True)
        el2 = jnp.where(lane == i1, neg, el)
        v2 = jnp.max(el2, axis=-1, keepdims=True)
        i2 = jnp.min(jnp.where(el2 == v2, lane, float(LANE)), axis=-1, keepdims=True)
        e21 = jnp.exp(v2 - v1)
        w1 = g_w / (1.0 + e21)
        w2 = g_w * e21 / (1.0 + e21)
        comb_ref[...] = jnp.where(lane == i1, w1, 0.0) + jnp.where(lane == i2, w2, 0.0)
        acc_ref[...] = jnp.zeros_like(acc_ref)

    hb = hb_ref[...]
    comb = comb_ref[...]
    pw = 2 * EXPERT_DIM
    upd = None
    for p in range(EXPERTS_PER_GROUP // 2):
        gu = _dot(hb, wgu_ref[0, :, 2 * pw * p:2 * pw * (p + 1)])
        gate = gu[:, :pw]
        hid = gate * jax.nn.sigmoid(gate) * gu[:, pw:]
        parts = []
        for k in range(2):
            lane_e = (N_GROUPS + EXPERTS_PER_GROUP * e + 2 * p + k).astype(F32)
            col = jnp.sum(jnp.where(lane == lane_e, comb, 0.0), axis=-1, keepdims=True)
            parts.append(hid[:, EXPERT_DIM * k:EXPERT_DIM * (k + 1)] * col)
        d = _dot(jnp.concatenate(parts, axis=1).astype(BF16), wd_ref[0, pw * p:pw * (p + 1), :])
        upd = d if upd is None else upd + d
    acc_ref[...] += upd

    @pl.when(e == N_GROUPS - 1)
    def _():
        o_ref[...] = x_ref[...] + _gate_mul(acc_ref[...], mod_ref[...], 5)


def _moe(x1, mod3, wr, br, wgu, wd, t_len, tm):
    n = x1.shape[0]
    return pl.pallas_call(
        _moe_body,
        grid=(n // tm, N_GROUPS),
        in_specs=[pl.BlockSpec((tm, D_MODEL), lambda i, e: (i, 0)),
                  _mod_spec(tm, t_len),
                  pl.BlockSpec((D_MODEL, LANE), lambda i, e: (0, 0)),
                  pl.BlockSpec((1, LANE), lambda i, e: (0, 0)),
                  pl.BlockSpec((1, D_MODEL, 2 * EXPERTS_PER_GROUP * EXPERT_DIM),
                               lambda i, e: (e, 0, 0)),
                  pl.BlockSpec((1, EXPERTS_PER_GROUP * EXPERT_DIM, D_MODEL),
                               lambda i, e: (e, 0, 0))],
        out_specs=pl.BlockSpec((tm, D_MODEL), lambda i, e: (i, 0)),
        out_shape=jax.ShapeDtypeStruct((n, D_MODEL), F32),
        scratch_shapes=[pltpu.VMEM((tm, D_MODEL), BF16), pltpu.VMEM((tm, LANE), F32),
                        pltpu.VMEM((tm, D_MODEL), F32)],
        compiler_params=pltpu.CompilerParams(dimension_semantics=("parallel", "arbitrary"),
                                             vmem_limit_bytes=VMEM_LIMIT),
        name="moe",
    )(x1, mod3, wr, br, wgu, wd)


def _prep_params(w_in, ssm_a_re, ssm_a_im, ssm_log_dt, ssm_b_re, ssm_b_im, ssm_c_re, ssm_c_im, ssm_d,
                 w_glu, q_gain, k_gain, kidx_gain, w_attn_out, w_out, w_rg, b_rg, w_re, b_re,
                 w_gate, w_up, w_down, max_batch):
    offs = np.cumsum([0, SSM_WIDTH, N_HEADS * HEAD_DIM, HEAD_DIM, HEAD_DIM, IDX_HEADS * IDX_DIM,
                      IDX_DIM, IDX_HEADS, D_MODEL, D_MODEL])
    seg = lambda i: w_in[:, offs[i]:offs[i + 1]]
    padto = lambda a, w: jnp.pad(a, ((0, 0), (0, w - a.shape[-1])))
    wq = jnp.pad(seg(1).reshape(D_MODEL, N_HEADS, HEAD_DIM), ((0, 0), (0, 0), (0, LANE - HEAD_DIM)))
    wqi = jnp.tile(seg(4).reshape(D_MODEL, IDX_HEADS, IDX_DIM), (1, 1, LANE // IDX_DIM))
    w_pad = jnp.concatenate(
        [seg(0), wq.reshape(D_MODEL, N_HEADS * LANE), padto(seg(2), LANE), padto(seg(3), LANE),
         wqi.reshape(D_MODEL, IDX_HEADS * LANE), jnp.tile(seg(5), (1, LANE // IDX_DIM)),
         padto(seg(6), LANE), seg(7), seg(8)], axis=1).astype(BF16)
    qg = padto((q_gain * (HEAD_DIM ** -0.5 * np.log2(np.e)))[None, :], LANE)
    kg = padto(k_gain[None, :], LANE)
    ig = jnp.tile(kidx_gain[None, :], (1, LANE // IDX_DIM))

    dt = jnp.exp(ssm_log_dt)[:, None]
    decay = jnp.exp(ssm_a_re * dt)
    lam_re = decay * jnp.cos(ssm_a_im * dt)
    lam_im = decay * jnp.sin(ssm_a_im * dt)
    den = ssm_a_re * ssm_a_re + ssm_a_im * ssm_a_im
    num_re = lam_re - 1.0
    coef_re = (num_re * ssm_a_re + lam_im * ssm_a_im) / den
    coef_im = (lam_im * ssm_a_re - num_re * ssm_a_im) / den
    bb_re = coef_re[..., None] * ssm_b_re - coef_im[..., None] * ssm_b_im
    bb_im = coef_re[..., None] * ssm_b_im + coef_im[..., None] * ssm_b_re
    eye = jnp.eye(SSM_GROUPS, dtype=F32)

    def in_map(bb):
        return jnp.einsum('gpj,gh->gjhp', bb, eye).reshape(SSM_WIDTH, SSM_LANES).astype(BF16)

    def out_map(c):
        return jnp.einsum('gjp,gh->gphj', c, eye).reshape(SSM_LANES, SSM_WIDTH).astype(BF16)

    bcast = lambda a: jnp.broadcast_to(a.reshape(1, SSM_LANES), (max_batch, SSM_LANES))
    ssm = (in_map(bb_re), in_map(bb_im), out_map(ssm_c_re), out_map(-ssm_c_im),
           bcast(lam_re), bcast(lam_im), ssm_d[None, :])

    wr = jnp.pad(jnp.concatenate([w_rg, w_re], axis=1), ((0, 0), (0, LANE - N_GROUPS - N_EXPERTS)))
    br = jnp.pad(jnp.concatenate([b_rg, b_re]), (0, LANE - N_GROUPS - N_EXPERTS))[None, :]
    half = EXPERTS_PER_GROUP // 2
    wgu = jnp.stack([w_gate, w_up], axis=1).astype(BF16)
    wgu = wgu.reshape(N_GROUPS, half, 2, 2, D_MODEL, EXPERT_DIM).transpose(0, 4, 1, 3, 2, 5)
    wgu = wgu.reshape(N_GROUPS, D_MODEL, 2 * EXPERTS_PER_GROUP * EXPERT_DIM)
    wd = w_down.astype(BF16).reshape(N_GROUPS, EXPERTS_PER_GROUP * EXPERT_DIM, D_MODEL)
    return dict(w_pad=w_pad, qg=qg, kg=kg, ig=ig, ssm=ssm,
                wglu=w_glu.astype(BF16), wo=w_attn_out.astype(BF16), wout=w_out.astype(BF16),
                wr=wr, br=br, wgu=wgu, wd=wd)


def _layer(x, mod3, p, past, *, tm_tok, tm_moe, tc, tq, kb):
    nb, t_len, _ = x.shape
    n = nb * t_len
    cb = 2 * kb
    reps = max(tq // t_len, 1)
    kbv = min(cb, t_len)
    xf = x.reshape(n, D_MODEL)
    u_tm, q, qc, k, v, ki, kp, kc, vt, wt, ga, gb = _proj(
        xf, mod3, p["w_pad"], p["qg"], p["kg"], p["ig"], nb, t_len, tm_tok, reps, kbv)
    k3 = k.reshape(nb, t_len, HEAD_DIM)
    v3 = v.reshape(nb, t_len, HEAD_DIM)
    ki3 = ki.reshape(nb, t_len, IDX_DIM)
    kp = kp.reshape(nb, t_len, LANE)
    kc = kc.reshape(nb, t_len, LANE)

    if past is None:
        h0r = jnp.zeros((nb, SSM_LANES), F32)
        h0i = h0r
        s_real, q_off = t_len, 0
    else:
        past_k, past_v, past_ki, h0_re, h0_im = past
        h0r = h0_re.reshape(nb, SSM_LANES)
        h0i = h0_im.reshape(nb, SSM_LANES)
        q_off = past_k.shape[1]
        s_real = q_off + t_len
        s_pad = -(-s_real // cb) * cb
        fill = lambda a: jnp.pad(a, ((0, 0), (0, s_pad - s_real), (0, 0)))
        pk = jnp.pad(past_k.astype(BF16), ((0, 0), (0, 0), (0, LANE - HEAD_DIM)))
        kp = fill(jnp.concatenate([pk, kp], axis=1))
        pi_hi, pi_lo = _split_bf16(past_ki)
        kc = fill(jnp.concatenate([jnp.concatenate([pi_hi, pi_lo, pi_hi, pi_lo], axis=-1), kc], axis=1))
        vt_all = jnp.concatenate([past_v.astype(BF16).transpose(0, 2, 1),
                                  vt.transpose(0, 2, 1, 3).reshape(nb, HEAD_DIM, t_len)], axis=2)
        vt_all = jnp.pad(vt_all, ((0, 0), (0, 0), (0, s_pad - s_real)))
        vt = vt_all.reshape(nb, HEAD_DIM, s_pad // cb, cb).transpose(0, 2, 1, 3)

    y_tm, s_re, s_im = _s5(u_tm.reshape(n, SSM_WIDTH), h0r, h0i, p["ssm"], nb, t_len, tc)
    ob = _dsa(q, qc, wt, kc, kp, vt, p["wo"], tq=tq, kb=kb, s_real=s_real, q_off=q_off,
              q_wrap=t_len if reps > 1 else tq)
    ob = ob[:, :t_len].reshape(n, D_MODEL)
    x1 = _merge(xf, y_tm.reshape(t_len, nb * SSM_WIDTH), ga, gb, ob, mod3, p["wglu"], p["wout"],
                t_len, tm_tok)
    x2 = _moe(x1, mod3, p["wr"], p["br"], p["wgu"], p["wd"], t_len, tm_moe)
    return (x2.reshape(nb, t_len, D_MODEL), k3, v3, ki3,
            s_re.reshape(nb, SSM_GROUPS, SSM_STATE), s_im.reshape(nb, SSM_GROUPS, SSM_STATE))


def kernel(x_prompt, x_sample, cache_k, cache_v, cache_kidx, state_ssm_re, state_ssm_im, c_prompt, c_sample, w_ada, b_ada, w_in, ssm_a_re, ssm_a_im, ssm_log_dt, ssm_b_re, ssm_b_im, ssm_c_re, ssm_c_im, ssm_d, w_glu, q_gain, k_gain, kidx_gain, w_attn_out, w_out, w_route_group, b_route_group, w_route_expert, b_route_expert, w_gate, w_up, w_down):
    depth = w_ada.shape[0]
    nbp = x_prompt.shape[0]
    nbs = x_sample.shape[0]
    xp, xs = x_prompt, x_sample
    outs = [[] for _ in range(10)]
    for l in range(depth):
        p = _prep_params(w_in[l], ssm_a_re[l], ssm_a_im[l], ssm_log_dt[l], ssm_b_re[l], ssm_b_im[l],
                         ssm_c_re[l], ssm_c_im[l], ssm_d[l], w_glu[l], q_gain[l], k_gain[l],
                         kidx_gain[l], w_attn_out[l], w_out[l], w_route_group[l], b_route_group[l],
                         w_route_expert[l], b_route_expert[l], w_gate[l], w_up[l], w_down[l],
                         max(nbp, nbs))
        mod = _adaln(jnp.concatenate([c_prompt, c_sample], axis=0), w_ada[l], b_ada[l][None, :])
        mod3 = mod[:, None, :]
        xp, kp, vp, kip, srp, sip = _layer(xp, mod3[:nbp], p, None,
                                           tm_tok=512, tm_moe=1024, tc=32, tq=256, kb=256)
        past = (cache_k[l], cache_v[l], cache_kidx[l], state_ssm_re[l], state_ssm_im[l])
        xs, ks, vs, kis, srs, sis = _layer(xs, mod3[nbp:], p, past,
                                           tm_tok=512, tm_moe=512, tc=64, tq=128, kb=384)
        for lst, val in zip(outs, (kp, vp, kip, srp, sip, ks, vs, kis, srs, sis)):
            lst.append(val)
    return (xp, xs) + tuple(jnp.stack(o) for o in outs)
```

```python
import functools

import jax
import jax.numpy as jnp
import numpy as np
from jax import lax
from jax.experimental import pallas as pl
from jax.experimental.pallas import tpu as pltpu

F32 = jnp.float32
BF16 = jnp.bfloat16
I32 = jnp.int32

D_MODEL = 1024
CHUNK_SHIFT = 6
EPS = 1e-6
NEG_INF = -1e30
SSM_WIDTH = 512
SSM_GROUP = 16
SSM_GROUPS = 32
SSM_STATE = 64
SSM_LANES = SSM_GROUPS * SSM_STATE
S5_EXPAND = SSM_STATE // SSM_GROUP
S5_IN_BLOCK = 256
N_HEADS = 8
HEAD_DIM = 64
IDX_HEADS = 8
IDX_DIM = 32
IDX_WEIGHT_SCALE = (IDX_HEADS * IDX_DIM) ** -0.5
TOPK_MAX = 256
N_GROUPS = 4
EXPERTS_PER_GROUP = 4
N_EXPERTS = 16
EXPERT_DIM = 256
LANE = 128
INT_MIN = -(2 ** 31)
VMEM_LIMIT = 56 * 1024 * 1024

C_U = 0
C_Q = C_U + SSM_WIDTH
C_K = C_Q + N_HEADS * LANE
C_V = C_K + LANE
C_QI = C_V + LANE
C_KI = C_QI + IDX_HEADS * LANE
C_WI = C_KI + LANE
C_GA = C_WI + LANE
C_GB = C_GA + D_MODEL
C_END = C_GB + D_MODEL


def _dot(a, b):
    return jnp.dot(a, b, preferred_element_type=F32)


def _dot_nt(a, b):
    return lax.dot_general(a, b, (((1,), (1,)), ((), ())), preferred_element_type=F32)


def _split_bf16(x):
    hi = x.astype(BF16)
    lo = (x - hi.astype(F32)).astype(BF16)
    return hi, lo


def _dot_hilo(a, b):
    ah, al = _split_bf16(a)
    bh, bl = _split_bf16(b)
    return _dot(ah, bh) + _dot(ah, bl) + _dot(al, bh)


def _tiling(tm, t_len):
    return (t_len // tm, 1) if tm < t_len else (1, tm // t_len)


def _mod_spec(tm, t_len):
    per, nbt = _tiling(tm, t_len)
    return pl.BlockSpec((nbt, 1, 6 * D_MODEL), lambda i, *_: (i // per, 0, 0))


def _tm_spec(tm, t_len):
    per, nbt = _tiling(tm, t_len)
    return pl.BlockSpec((tm // nbt, nbt * SSM_WIDTH), lambda i, *_: (i % per, i // per))


def _modulate(x, mod, scale_seg, shift_seg):
    nb = mod.shape[0]
    tm = x.shape[0]
    sc = mod[:, :, scale_seg * D_MODEL:(scale_seg + 1) * D_MODEL]
    sh = mod[:, :, shift_seg * D_MODEL:(shift_seg + 1) * D_MODEL]
    x3 = x.reshape(nb, tm // nb, D_MODEL)
    return (x3 * (1.0 + sc) + sh).reshape(tm, D_MODEL)


def _gate_mul(x, mod, seg):
    nb = mod.shape[0]
    tm = x.shape[0]
    g = mod[:, :, seg * D_MODEL:(seg + 1) * D_MODEL]
    return (x.reshape(nb, tm // nb, D_MODEL) * g).reshape(tm, D_MODEL)


def _rms(x, n):
    return x * lax.rsqrt(jnp.sum(x * x, axis=-1, keepdims=True) * (1.0 / n) + EPS)


def _adaln_body(c_ref, w_ref, b_ref, o_ref):
    c = c_ref[...]
    s = c * jax.nn.sigmoid(c)
    o_ref[...] = _dot_hilo(s, w_ref[...]) + b_ref[...]


def _adaln(c, w, b):
    nb = c.shape[0]
    n = w.shape[1]
    bn = 512
    return pl.pallas_call(
        _adaln_body,
        grid=(n // bn,),
        in_specs=[pl.BlockSpec((nb, D_MODEL), lambda j: (0, 0)),
                  pl.BlockSpec((D_MODEL, bn), lambda j: (0, j)),
                  pl.BlockSpec((1, bn), lambda j: (0, j))],
        out_specs=pl.BlockSpec((nb, bn), lambda j: (0, j)),
        out_shape=jax.ShapeDtypeStruct((nb, n), F32),
        name="adaln",
    )(c, w, b)


def _proj_body(x_ref, mod_ref, w_ref, qg_ref, kg_ref, ig_ref,
               u_ref, q_ref, qc_ref, k_ref, v_ref, ki_ref, kp_ref, kc_ref, vt_ref, wt_ref,
               ga_ref, gb_ref, *, nbt, reps, kbv):
    tm = x_ref.shape[0]
    tt = tm // nbt
    rows = lambda b: slice(tt * b, tt * (b + 1))
    hsl = lambda h: slice(LANE * h, LANE * (h + 1))
    lane = lax.broadcasted_iota(I32, (tm, LANE), 1)
    hb = _modulate(_rms(x_ref[...], D_MODEL), mod_ref[...], 1, 0).astype(BF16)

    u = _dot(hb, w_ref[:, C_U:C_Q])
    for b in range(nbt):
        u_ref[:, SSM_WIDTH * b:SSM_WIDTH * (b + 1)] = u[rows(b), :].astype(BF16)

    def put_heads(ref, pieces):
        for b in range(nbt):
            for h in range(N_HEADS):
                for r in range(reps):
                    ref[b, h, tt * r:tt * (r + 1), :] = pieces[h][rows(b), :]

    q = _dot(hb, w_ref[:, C_Q:C_K])
    put_heads(q_ref, [(_rms(q[:, hsl(h)], HEAD_DIM) * qg_ref[...]).astype(BF16)
                      for h in range(N_HEADS)])

    qi_hi, qi_lo = _split_bf16(_dot(hb, w_ref[:, C_QI:C_KI]))
    put_heads(qc_ref, [jnp.where(lane < 2 * IDX_DIM, qi_hi[:, hsl(h)], qi_lo[:, hsl(h)])
                       for h in range(IDX_HEADS)])

    kn = _rms(_dot(hb, w_ref[:, C_K:C_V]), HEAD_DIM) * kg_ref[...]
    k_ref[...] = kn[:, :HEAD_DIM]
    kp_ref[...] = kn.astype(BF16)

    vfull = _dot(hb, w_ref[:, C_V:C_QI])
    v_ref[...] = vfull[:, :HEAD_DIM]
    vt = vfull.T[:HEAD_DIM].astype(BF16)
    for b in range(nbt):
        for c in range(tt // kbv):
            vt_ref[b, c] = vt[:, tt * b + kbv * c:tt * b + kbv * (c + 1)]

    kin = _rms(_dot(hb, w_ref[:, C_KI:C_WI]), LANE) * ig_ref[...]
    ki_ref[...] = kin[:, :IDX_DIM]
    ki_hi, ki_lo = _split_bf16(kin)
    kc_ref[...] = jnp.where(((lane >> 5) & 1) == 1, ki_lo, ki_hi)

    wt = (_dot(hb, w_ref[:, C_WI:C_GA]) * IDX_WEIGHT_SCALE).T[:IDX_HEADS]
    for b in range(nbt):
        for r in range(reps):
            wt_ref[b, :, tt * r:tt * (r + 1)] = wt[:, rows(b)]

    ga_ref[...] = jax.nn.sigmoid(_dot(hb, w_ref[:, C_GA:C_GB])).astype(BF16)
    gb_ref[...] = jax.nn.sigmoid(_dot(hb, w_ref[:, C_GB:C_END])).astype(BF16)


def _proj(xf, mod3, w_pad, qg, kg, ig, nb, t_len, tm, reps, kbv):
    n = xf.shape[0]
    per, nbt = _tiling(tm, t_len)
    tt = tm // nbt
    row = lambda w: pl.BlockSpec((tm, w), lambda i: (i, 0))
    const = lambda s: pl.BlockSpec(s, lambda i: (0, 0))
    head_spec = pl.BlockSpec((nbt, N_HEADS, tt * reps, LANE), lambda i: (i // per, 0, i % per, 0))
    head_shape = jax.ShapeDtypeStruct((nb, N_HEADS, t_len * reps, LANE), BF16)
    outs = [
        (_tm_spec(tm, t_len), jax.ShapeDtypeStruct((t_len, nb * SSM_WIDTH), BF16)),
        (head_spec, head_shape), (head_spec, head_shape),
        (row(HEAD_DIM), jax.ShapeDtypeStruct((n, HEAD_DIM), F32)),
        (row(HEAD_DIM), jax.ShapeDtypeStruct((n, HEAD_DIM), F32)),
        (row(IDX_DIM), jax.ShapeDtypeStruct((n, IDX_DIM), F32)),
        (row(LANE), jax.ShapeDtypeStruct((n, LANE), BF16)),
        (row(LANE), jax.ShapeDtypeStruct((n, LANE), BF16)),
        (pl.BlockSpec((nbt, tt // kbv, HEAD_DIM, kbv), lambda i: (i // per, i % per, 0, 0)),
         jax.ShapeDtypeStruct((nb, t_len // kbv, HEAD_DIM, kbv), BF16)),
        (pl.BlockSpec((nbt, IDX_HEADS, tt * reps), lambda i: (i // per, 0, i % per)),
         jax.ShapeDtypeStruct((nb, IDX_HEADS, t_len * reps), F32)),
        (row(D_MODEL), jax.ShapeDtypeStruct((n, D_MODEL), BF16)),
        (row(D_MODEL), jax.ShapeDtypeStruct((n, D_MODEL), BF16)),
    ]
    return pl.pallas_call(
        functools.partial(_proj_body, nbt=nbt, reps=reps, kbv=kbv),
        grid=(n // tm,),
        in_specs=[row(D_MODEL), _mod_spec(tm, t_len), const((D_MODEL, C_END)),
                  const((1, LANE)), const((1, LANE)), const((1, LANE))],
        out_specs=[s for s, _ in outs],
        out_shape=[s for _, s in outs],
        compiler_params=pltpu.CompilerParams(dimension_semantics=("parallel",),
                                             vmem_limit_bytes=VMEM_LIMIT),
        name="proj",
    )(xf, mod3, w_pad, qg, kg, ig)


def _s5_body(u_ref, h0r_ref, h0i_ref, bre_ref, bim_ref, cre_ref, cim_ref, lre_ref, lim_ref, d_ref,
             z_ref, sr_ref, si_ref, xr, xi, hr, hi, *, nb, tc, lane_chunk):
    @pl.when(pl.program_id(0) == 0)
    def _():
        hr[...] = h0r_ref[...]
        hi[...] = h0i_ref[...]

    ub = u_ref[...]
    for c in range(SSM_WIDTH // S5_IN_BLOCK):
        cin = slice(S5_IN_BLOCK * c, S5_IN_BLOCK * (c + 1))
        cst = slice(S5_IN_BLOCK * S5_EXPAND * c, S5_IN_BLOCK * S5_EXPAND * (c + 1))
        xr[:, cst] = _dot(ub[:, cin], bre_ref[cin, cst])
        xi[:, cst] = _dot(ub[:, cin], bim_ref[cin, cst])

    for lc in range(SSM_LANES // lane_chunk):
        cols = slice(lc * lane_chunk, (lc + 1) * lane_chunk)

        def step(t, carry, cols=cols):
            a, b = carry
            rows = pl.ds(pl.multiple_of(t * nb, nb), nb)
            lr = lre_ref[:, cols]
            li = lim_ref[:, cols]
            na = lr * a - li * b + xr[rows, cols]
            nb_ = lr * b + li * a + xi[rows, cols]
            xr[rows, cols] = na
            xi[rows, cols] = nb_
            return na, nb_

        a, b = lax.fori_loop(0, tc, step, (hr[:, cols], hi[:, cols]), unroll=2)
        hr[:, cols] = a
        hi[:, cols] = b

    for c in range(SSM_WIDTH // LANE):
        cy = slice(LANE * c, LANE * (c + 1))
        cst = slice(LANE * S5_EXPAND * c, LANE * S5_EXPAND * (c + 1))
        y = (_dot(xr[:, cst].astype(BF16), cre_ref[cst, cy])
             + _dot(xi[:, cst].astype(BF16), cim_ref[cst, cy]))
        y = y + d_ref[:, cy] * ub[:, cy].astype(F32)
        z_ref[:, cy] = jax.nn.gelu(y, approximate=True).astype(BF16)
    sr_ref[...] = hr[...]
    si_ref[...] = hi[...]


def _s5(u_tm, h0r, h0i, ssm, nb, t_len, tc):
    bre, bim, cre, cim, lre, lim, dsk = ssm
    rows = tc * nb
    const = lambda s: pl.BlockSpec(s, lambda c: (0, 0))
    return pl.pallas_call(
        functools.partial(_s5_body, nb=nb, tc=tc, lane_chunk=512),
        grid=(t_len // tc,),
        in_specs=[pl.BlockSpec((rows, SSM_WIDTH), lambda c: (c, 0)),
                  const((nb, SSM_LANES)), const((nb, SSM_LANES)),
                  const((SSM_WIDTH, SSM_LANES)), const((SSM_WIDTH, SSM_LANES)),
                  const((SSM_LANES, SSM_WIDTH)), const((SSM_LANES, SSM_WIDTH)),
                  const((nb, SSM_LANES)), const((nb, SSM_LANES)), const((1, SSM_WIDTH))],
        out_specs=[pl.BlockSpec((rows, SSM_WIDTH), lambda c: (c, 0)),
                   const((nb, SSM_LANES)), const((nb, SSM_LANES))],
        out_shape=[jax.ShapeDtypeStruct((t_len * nb, SSM_WIDTH), BF16),
                   jax.ShapeDtypeStruct((nb, SSM_LANES), F32),
                   jax.ShapeDtypeStruct((nb, SSM_LANES), F32)],
        scratch_shapes=[pltpu.VMEM((rows, SSM_LANES), F32), pltpu.VMEM((rows, SSM_LANES), F32),
                        pltpu.VMEM((nb, SSM_LANES), F32), pltpu.VMEM((nb, SSM_LANES), F32)],
        compiler_params=pltpu.CompilerParams(dimension_semantics=("arbitrary",),
                                             vmem_limit_bytes=VMEM_LIMIT),
        name="s5",
    )(u_tm, h0r, h0i, bre, bim, cre, cim, lre[:nb], lim[:nb], dsk)


def _dsa_body(q_ref, qc_ref, wt_ref, kc_ref, k_ref, vt_ref, wo_ref, o_ref,
              sc_ref, lg_ref, acc_ref, at_ref, st0_ref, st1_ref, bias_ref,
              *, tq, kb, s_real, s_pad, q_off, q_wrap, topk):
    j = pl.program_id(1)
    hq = N_HEADS * tq
    n_all = s_pad // kb
    cb = 2 * kb
    lane_q = lax.broadcasted_iota(I32, (1, tq), 1)
    qpos = q_off + j * tq + (lane_q & (q_wrap - 1))
    qchunk = qpos >> CHUNK_SHIFT
    last_end = (((q_off + j * tq + min(tq, q_wrap) - 1) >> CHUNK_SHIFT) + 1) << CHUNK_SHIFT
    npair = jnp.minimum(n_all // 2, lax.div(last_end + (cb - 1), cb))

    def blk(b):
        return pl.ds(pl.multiple_of(b * kb, kb), kb)

    def pair(i):
        return pl.ds(pl.multiple_of(i * cb, cb), cb)

    def admissible(b):
        kpos = b * kb + lax.broadcasted_iota(I32, (kb, tq), 0)
        return ((kpos >> CHUNK_SHIFT) <= qchunk) & (kpos < s_real), kpos

    def fold8(x, op):
        return op(x.reshape(x.shape[0] // 8, 8, x.shape[-1]), axis=0)

    hsl = lambda h: slice(h * tq, (h + 1) * tq)

    def staged(mm, consume, carry):
        st0_ref[...] = mm(0)

        def body(i, c):
            b0 = 2 * i
            st1_ref[...] = mm(b0 + 1)
            c = consume(b0, st0_ref, c)
            st0_ref[...] = mm(jnp.minimum(b0 + 2, n_all - 1))
            return consume(b0 + 1, st1_ref, c)

        return lax.fori_loop(0, npair, body, carry)

    qcs = qc_ref[0].reshape(hq, LANE)
    wt = wt_ref[0]

    rc = 64

    def sub(b, r):
        return pl.ds(pl.multiple_of(b * kb + r * rc, rc), rc)

    def admissible_rows(b, r):
        kpos = b * kb + r * rc + lax.broadcasted_iota(I32, (rc, tq), 0)
        return ((kpos >> CHUNK_SHIFT) <= qchunk) & (kpos < s_real), kpos

    def score_consume(b, st, carry):
        for r in range(kb // rc):
            sc = jnp.zeros((rc, tq), F32)
            for h in range(IDX_HEADS):
                sc = sc + wt[h:h + 1, :] * jnp.maximum(st[r * rc:(r + 1) * rc, hsl(h)], 0.0)
            sc_ref[sub(b, r), :] = jnp.where(admissible_rows(b, r)[0], sc, NEG_INF)
        return carry

    staged(lambda b: _dot_nt(kc_ref[0, blk(b), :], qcs), score_consume, 0)

    def as_float(u):
        s = u ^ INT_MIN
        return lax.bitcast_convert_type(s ^ ((s >> 31) & 0x7FFFFFFF), F32)

    def count(pred):
        def body(i, acc):
            for r in range(cb // 32):
                row0 = i * cb + 32 * r
                kpos = lambda row0=row0: row0 + lax.broadcasted_iota(I32, (32, tq), 0)
                x = sc_ref[pl.ds(pl.multiple_of(row0, 32), 32), :]
                acc = acc + jnp.where(pred(x, kpos), 1.0, 0.0)
            return acc
        acc = lax.fori_loop(0, npair, body, jnp.zeros((32, tq), F32))
        return jnp.sum(acc, axis=0, keepdims=True)

    def bit_body(i, prefix):
        cand = prefix | (jnp.int32(1) << (31 - i))
        cf = as_float(cand)
        cnt = count(lambda x, kpos: x >= cf)
        return jnp.where(cnt >= topk, cand, prefix)

    thr = as_float(lax.fori_loop(0, 32, bit_body, jnp.zeros((1, tq), I32)))

    n_gt = count(lambda x, kpos: x > thr)
    n_eq = count(lambda x, kpos: x == thr)
    need = topk - n_gt
    any_over = jnp.max(n_eq - need) > 0.0
    idx_bits = int(s_pad).bit_length()

    def tie_search():
        def tbody(i, lim):
            cand = lim | (jnp.int32(1) << (idx_bits - 1 - i))
            n = count(lambda x, kpos: (x == thr) & (kpos() < cand))
            return jnp.where(n <= need, cand, lim)
        return lax.fori_loop(0, idx_bits, tbody, jnp.zeros((1, tq), I32))

    lim = lax.cond(any_over, tie_search, lambda: jnp.full((1, tq), 2 ** idx_bits - 1, I32))

    qs = q_ref[0].reshape(hq, LANE)

    def logits_pair(i, m):
        for half in range(2):
            b = 2 * i + half
            for r in range(kb // rc):
                x = sc_ref[sub(b, r), :]
                adm, kpos = admissible_rows(b, r)
                sel = ((x > thr) | ((x == thr) & (kpos < lim))) & adm
                bias_ref[half * kb + r * rc:half * kb + (r + 1) * rc, :] = jnp.where(sel, 0.0, NEG_INF)
        tops = [m[:, hsl(h)] for h in range(N_HEADS)]
        for half in range(2):
            b = 2 * i + half
            raw = _dot_nt(k_ref[0, blk(b), :], qs)
            for h in range(N_HEADS):
                lh = raw[:, hsl(h)] + bias_ref[half * kb:(half + 1) * kb, :]
                lg_ref[blk(b), hsl(h)] = lh
                tops[h] = jnp.maximum(tops[h], fold8(lh, jnp.max))
        return jnp.concatenate(tops, axis=1)

    m8 = lax.fori_loop(0, npair, logits_pair, jnp.full((8, hq), NEG_INF, F32))
    m = jnp.max(m8, axis=0, keepdims=True)

    acc_ref[...] = jnp.zeros_like(acc_ref)

    def pv_pair(i, ssum):
        p = jnp.exp2(lg_ref[pair(i), :] - m)
        acc_ref[...] += _dot(vt_ref[0, i], p.astype(BF16))
        return ssum + fold8(p, jnp.sum)

    s8 = lax.fori_loop(0, npair, pv_pair, jnp.zeros((8, hq), F32))
    out_t = acc_ref[...] / jnp.sum(s8, axis=0, keepdims=True)
    for h in range(N_HEADS):
        at_ref[HEAD_DIM * h:HEAD_DIM * (h + 1), :] = out_t[:, hsl(h)]
    o_ref[0] = _dot(at_ref[...].T.astype(BF16), wo_ref[...]).astype(BF16)


def _dsa(q, qc, wt, kc, kp, vt, wo, *, tq, kb, s_real, q_off, q_wrap):
    nb, _, t_len, _ = q.shape
    s_pad = kc.shape[1]
    cb = 2 * kb
    topk = min(TOPK_MAX, s_real // 4)
    hq = N_HEADS * tq
    qspec = pl.BlockSpec((1, N_HEADS, tq, LANE), lambda b, j: (b, 0, j, 0))
    return pl.pallas_call(
        functools.partial(_dsa_body, tq=tq, kb=kb, s_real=s_real, s_pad=s_pad, q_off=q_off,
                          q_wrap=q_wrap, topk=float(topk)),
        grid=(nb, t_len // tq),
        in_specs=[qspec, qspec,
                  pl.BlockSpec((1, IDX_HEADS, tq), lambda b, j: (b, 0, j)),
                  pl.BlockSpec((1, s_pad, LANE), lambda b, j: (b, 0, 0)),
                  pl.BlockSpec((1, s_pad, LANE), lambda b, j: (b, 0, 0)),
                  pl.BlockSpec((1, s_pad // cb, HEAD_DIM, cb), lambda b, j: (b, 0, 0, 0)),
                  pl.BlockSpec((N_HEADS * HEAD_DIM, D_MODEL), lambda b, j: (0, 0))],
        out_specs=pl.BlockSpec((1, tq, D_MODEL), lambda b, j: (b, j, 0)),
        out_shape=jax.ShapeDtypeStruct((nb, t_len, D_MODEL), BF16),
        scratch_shapes=[pltpu.VMEM((s_pad, tq), F32), pltpu.VMEM((s_pad, hq), F32),
                        pltpu.VMEM((HEAD_DIM, hq), F32), pltpu.VMEM((N_HEADS * HEAD_DIM, tq), F32),
                        pltpu.VMEM((kb, hq), F32), pltpu.VMEM((kb, hq), F32),
                        pltpu.VMEM((cb, tq), F32)],
        compiler_params=pltpu.CompilerParams(dimension_semantics=("parallel", "parallel"),
                                             vmem_limit_bytes=VMEM_LIMIT),
        name="dsa",
    )(q, qc, wt, kc, kp, vt, wo)


def _merge_body(x_ref, z_ref, ga_ref, gb_ref, ob_ref, mod_ref, wglu_ref, wout_ref, o_ref, *, nbt):
    z = z_ref[...]
    if nbt > 1:
        z = jnp.concatenate([z[:, SSM_WIDTH * b:SSM_WIDTH * (b + 1)] for b in range(nbt)], axis=0)
    glu = _dot(z, wglu_ref[...])
    out_a = glu[:, :D_MODEL] * jax.nn.sigmoid(glu[:, D_MODEL:])
    merged = ga_ref[...] * out_a + gb_ref[...] * ob_ref[...]
    upd = _dot(merged.astype(BF16), wout_ref[...])
    o_ref[...] = x_ref[...] + _gate_mul(upd, mod_ref[...], 2)


def _merge(xf, y_tm, ga, gb, ob, mod3, wglu, wout, t_len, tm):
    n = xf.shape[0]
    row = lambda w: pl.BlockSpec((tm, w), lambda i: (i, 0))
    const = lambda s: pl.BlockSpec(s, lambda i: (0, 0))
    return pl.pallas_call(
        functools.partial(_merge_body, nbt=_tiling(tm, t_len)[1]),
        grid=(n // tm,),
        in_specs=[row(D_MODEL), _tm_spec(tm, t_len), row(D_MODEL), row(D_MODEL), row(D_MODEL),
                  _mod_spec(tm, t_len), const((SSM_WIDTH, 2 * D_MODEL)), const((D_MODEL, D_MODEL))],
        out_specs=row(D_MODEL),
        out_shape=jax.ShapeDtypeStruct((n, D_MODEL), F32),
        compiler_params=pltpu.CompilerParams(dimension_semantics=("parallel",),
                                             vmem_limit_bytes=VMEM_LIMIT),
        name="merge",
    )(xf, y_tm, ga, gb, ob, mod3, wglu, wout)


def _moe_body(x_ref, mod_ref, wr_ref, br_ref, wgu_ref, wd_ref, o_ref, hb_ref, comb_ref, acc_ref):
    e = pl.program_id(1)
    tm = x_ref.shape[0]
    lane = lax.broadcasted_iota(I32, (tm, LANE), 1).astype(F32)

    @pl.when(e == 0)
    def _():
        h2 = _modulate(_rms(x_ref[...], D_MODEL), mod_ref[...], 4, 3)
        hb_ref[...] = h2.astype(BF16)
        lg = _dot_hilo(h2, wr_ref[...]) + br_ref[...]
        neg = -jnp.inf
        gl = jnp.where(lane < N_GROUPS, lg, neg)
        gmax = jnp.max(gl, axis=-1, keepdims=True)
        g_idx = jnp.min(jnp.where(gl == gmax, lane, float(LANE)), axis=-1, keepdims=True)
        g_w = 1.0 / jnp.sum(jnp.where(lane < N_GROUPS, jnp.exp(lg - gmax), 0.0),
                            axis=-1, keepdims=True)
        lo = N_GROUPS + EXPERTS_PER_GROUP * g_idx
        el = jnp.where((lane >= lo) & (lane < lo + EXPERTS_PER_GROUP), lg, neg)
        v1 = jnp.max(el, axis=-1, keepdims=True)
        i1 = jnp.min(jnp.where(el == v1, lane, float(LANE)), axis=-1, keepdims=True)
        el2 = jnp.where(lane == i1, neg, el)
        v2 = jnp.max(el2, axis=-1, keepdims=True)
        i2 = jnp.min(jnp.where(el2 == v2, lane, float(LANE)), axis=-1, keepdims=True)
        e21 = jnp.exp(v2 - v1)
        w1 = g_w / (1.0 + e21)
        w2 = g_w * e21 / (1.0 + e21)
        comb_ref[...] = jnp.where(lane == i1, w1, 0.0) + jnp.where(lane == i2, w2, 0.0)
        acc_ref[...] = jnp.zeros_like(acc_ref)

    hb = hb_ref[...]
    comb = comb_ref[...]
    pw = 2 * EXPERT_DIM
    upd = None
    for p in range(EXPERTS_PER_GROUP // 2):
        gu = _dot(hb, wgu_ref[0, :, 2 * pw * p:2 * pw * (p + 1)])
        gate = gu[:, :pw]
        hid = gate * jax.nn.sigmoid(gate) * gu[:, pw:]
        parts = []
        for k in range(2):
            lane_e = (N_GROUPS + EXPERTS_PER_GROUP * e + 2 * p + k).astype(F32)
            col = jnp.sum(jnp.where(lane == lane_e, comb, 0.0), axis=-1, keepdims=True)
            parts.append(hid[:, EXPERT_DIM * k:EXPERT_DIM * (k + 1)] * col)
        d = _dot(jnp.concatenate(parts, axis=1).astype(BF16), wd_ref[0, pw * p:pw * (p + 1), :])
        upd = d if upd is None else upd + d
    acc_ref[...] += upd

    @pl.when(e == N_GROUPS - 1)
    def _():
        o_ref[...] = x_ref[...] + _gate_mul(acc_ref[...], mod_ref[...], 5)


def _moe(x1, mod3, wr, br, wgu, wd, t_len, tm):
    n = x1.shape[0]
    return pl.pallas_call(
        _moe_body,
        grid=(n // tm, N_GROUPS),
        in_specs=[pl.BlockSpec((tm, D_MODEL), lambda i, e: (i, 0)),
                  _mod_spec(tm, t_len),
                  pl.BlockSpec((D_MODEL, LANE), lambda i, e: (0, 0)),
                  pl.BlockSpec((1, LANE), lambda i, e: (0, 0)),
                  pl.BlockSpec((1, D_MODEL, 2 * EXPERTS_PER_GROUP * EXPERT_DIM),
                               lambda i, e: (e, 0, 0)),
                  pl.BlockSpec((1, EXPERTS_PER_GROUP * EXPERT_DIM, D_MODEL),
                               lambda i, e: (e, 0, 0))],
        out_specs=pl.BlockSpec((tm, D_MODEL), lambda i, e: (i, 0)),
        out_shape=jax.ShapeDtypeStruct((n, D_MODEL), F32),
        scratch_shapes=[pltpu.VMEM((tm, D_MODEL), BF16), pltpu.VMEM((tm, LANE), F32),
                        pltpu.VMEM((tm, D_MODEL), F32)],
        compiler_params=pltpu.CompilerParams(dimension_semantics=("parallel", "arbitrary"),
                                             vmem_limit_bytes=VMEM_LIMIT),
        name="moe",
    )(x1, mod3, wr, br, wgu, wd)


def _prep_params(w_in, ssm_a_re, ssm_a_im, ssm_log_dt, ssm_b_re, ssm_b_im, ssm_c_re, ssm_c_im, ssm_d,
                 w_glu, q_gain, k_gain, kidx_gain, w_attn_out, w_out, w_rg, b_rg, w_re, b_re,
                 w_gate, w_up, w_down, max_batch):
    offs = np.cumsum([0, SSM_WIDTH, N_HEADS * HEAD_DIM, HEAD_DIM, HEAD_DIM, IDX_HEADS * IDX_DIM,
                      IDX_DIM, IDX_HEADS, D_MODEL, D_MODEL])
    seg = lambda i: w_in[:, offs[i]:offs[i + 1]]
    padto = lambda a, w: jnp.pad(a, ((0, 0), (0, w - a.shape[-1])))
    wq = jnp.pad(seg(1).reshape(D_MODEL, N_HEADS, HEAD_DIM), ((0, 0), (0, 0), (0, LANE - HEAD_DIM)))
    wqi = jnp.tile(seg(4).reshape(D_MODEL, IDX_HEADS, IDX_DIM), (1, 1, LANE // IDX_DIM))
    w_pad = jnp.concatenate(
        [seg(0), wq.reshape(D_MODEL, N_HEADS * LANE), padto(seg(2), LANE), padto(seg(3), LANE),
         wqi.reshape(D_MODEL, IDX_HEADS * LANE), jnp.tile(seg(5), (1, LANE // IDX_DIM)),
         padto(seg(6), LANE), seg(7), seg(8)], axis=1).astype(BF16)
    qg = padto((q_gain * (HEAD_DIM ** -0.5 * np.log2(np.e)))[None, :], LANE)
    kg = padto(k_gain[None, :], LANE)
    ig = jnp.tile(kidx_gain[None, :], (1, LANE // IDX_DIM))

    dt = jnp.exp(ssm_log_dt)[:, None]
    decay = jnp.exp(ssm_a_re * dt)
    lam_re = decay * jnp.cos(ssm_a_im * dt)
    lam_im = decay * jnp.sin(ssm_a_im * dt)
    den = ssm_a_re * ssm_a_re + ssm_a_im * ssm_a_im
    num_re = lam_re - 1.0
    coef_re = (num_re * ssm_a_re + lam_im * ssm_a_im) / den
    coef_im = (lam_im * ssm_a_re - num_re * ssm_a_im) / den
    bb_re = coef_re[..., None] * ssm_b_re - coef_im[..., None] * ssm_b_im
    bb_im = coef_re[..., None] * ssm_b_im + coef_im[..., None] * ssm_b_re
    eye = jnp.eye(SSM_GROUPS, dtype=F32)

    def in_map(bb):
        return jnp.einsum('gpj,gh->gjhp', bb, eye).reshape(SSM_WIDTH, SSM_LANES).astype(BF16)

    def out_map(c):
        return jnp.einsum('gjp,gh->gphj', c, eye).reshape(SSM_LANES, SSM_WIDTH).astype(BF16)

    bcast = lambda a: jnp.broadcast_to(a.reshape(1, SSM_LANES), (max_batch, SSM_LANES))
    ssm = (in_map(bb_re), in_map(bb_im), out_map(ssm_c_re), out_map(-ssm_c_im),
           bcast(lam_re), bcast(lam_im), ssm_d[None, :])

    wr = jnp.pad(jnp.concatenate([w_rg, w_re], axis=1), ((0, 0), (0, LANE - N_GROUPS - N_EXPERTS)))
    br = jnp.pad(jnp.concatenate([b_rg, b_re]), (0, LANE - N_GROUPS - N_EXPERTS))[None, :]
    half = EXPERTS_PER_GROUP // 2
    wgu = jnp.stack([w_gate, w_up], axis=1).astype(BF16)
    wgu = wgu.reshape(N_GROUPS, half, 2, 2, D_MODEL, EXPERT_DIM).transpose(0, 4, 1, 3, 2, 5)
    wgu = wgu.reshape(N_GROUPS, D_MODEL, 2 * EXPERTS_PER_GROUP * EXPERT_DIM)
    wd = w_down.astype(BF16).reshape(N_GROUPS, EXPERTS_PER_GROUP * EXPERT_DIM, D_MODEL)
    return dict(w_pad=w_pad, qg=qg, kg=kg, ig=ig, ssm=ssm,
                wglu=w_glu.astype(BF16), wo=w_attn_out.astype(BF16), wout=w_out.astype(BF16),
                wr=wr, br=br, wgu=wgu, wd=wd)


def _layer(x, mod3, p, past, *, tm_tok, tm_moe, tc, tq, kb):
    nb, t_len, _ = x.shape
    n = nb * t_len
    cb = 2 * kb
    reps = max(tq // t_len, 1)
    kbv = min(cb, t_len)
    xf = x.reshape(n, D_MODEL)
    u_tm, q, qc, k, v, ki, kp, kc, vt, wt, ga, gb = _proj(
        xf, mod3, p["w_pad"], p["qg"], p["kg"], p["ig"], nb, t_len, tm_tok, reps, kbv)
    k3 = k.reshape(nb, t_len, HEAD_DIM)
    v3 = v.reshape(nb, t_len, HEAD_DIM)
    ki3 = ki.reshape(nb, t_len, IDX_DIM)
    kp = kp.reshape(nb, t_len, LANE)
    kc = kc.reshape(nb, t_len, LANE)

    if past is None:
        h0r = jnp.zeros((nb, SSM_LANES), F32)
        h0i = h0r
        s_real, q_off = t_len, 0
    else:
        past_k, past_v, past_ki, h0_re, h0_im = past
        h0r = h0_re.reshape(nb, SSM_LANES)
        h0i = h0_im.reshape(nb, SSM_LANES)
        q_off = past_k.shape[1]
        s_real = q_off + t_len
        s_pad = -(-s_real // cb) * cb
        fill = lambda a: jnp.pad(a, ((0, 0), (0, s_pad - s_real), (0, 0)))
        pk = jnp.pad(past_k.astype(BF16), ((0, 0), (0, 0), (0, LANE - HEAD_DIM)))
        kp = fill(jnp.concatenate([pk, kp], axis=1))
        pi_hi, pi_lo = _split_bf16(past_ki)
        kc = fill(jnp.concatenate([jnp.concatenate([pi_hi, pi_lo, pi_hi, pi_lo], axis=-1), kc], axis=1))
        vt_all = jnp.concatenate([past_v.astype(BF16).transpose(0, 2, 1),
                                  vt.transpose(0, 2, 1, 3).reshape(nb, HEAD_DIM, t_len)], axis=2)
        vt_all = jnp.pad(vt_all, ((0, 0), (0, 0), (0, s_pad - s_real)))
        vt = vt_all.reshape(nb, HEAD_DIM, s_pad // cb, cb).transpose(0, 2, 1, 3)

    y_tm, s_re, s_im = _s5(u_tm.reshape(n, SSM_WIDTH), h0r, h0i, p["ssm"], nb, t_len, tc)
    ob = _dsa(q, qc, wt, kc, kp, vt, p["wo"], tq=tq, kb=kb, s_real=s_real, q_off=q_off,
              q_wrap=t_len if reps > 1 else tq)
    ob = ob[:, :t_len].reshape(n, D_MODEL)
    x1 = _merge(xf, y_tm.reshape(t_len, nb * SSM_WIDTH), ga, gb, ob, mod3, p["wglu"], p["wout"],
                t_len, tm_tok)
    x2 = _moe(x1, mod3, p["wr"], p["br"], p["wgu"], p["wd"], t_len, tm_moe)
    return (x2.reshape(nb, t_len, D_MODEL), k3, v3, ki3,
            s_re.reshape(nb, SSM_GROUPS, SSM_STATE), s_im.reshape(nb, SSM_GROUPS, SSM_STATE))


def kernel(x_prompt, x_sample, cache_k, cache_v, cache_kidx, state_ssm_re, state_ssm_im, c_prompt, c_sample, w_ada, b_ada, w_in, ssm_a_re, ssm_a_im, ssm_log_dt, ssm_b_re, ssm_b_im, ssm_c_re, ssm_c_im, ssm_d, w_glu, q_gain, k_gain, kidx_gain, w_attn_out, w_out, w_route_group, b_route_group, w_route_expert, b_route_expert, w_gate, w_up, w_down):
    depth = w_ada.shape[0]
    nbp = x_prompt.shape[0]
    nbs = x_sample.shape[0]
    xp, xs = x_prompt, x_sample
    outs = [[] for _ in range(10)]
    for l in range(depth):
        p = _prep_params(w_in[l], ssm_a_re[l], ssm_a_im[l], ssm_log_dt[l], ssm_b_re[l], ssm_b_im[l],
                         ssm_c_re[l], ssm_c_im[l], ssm_d[l], w_glu[l], q_gain[l], k_gain[l],
                         kidx_gain[l], w_attn_out[l], w_out[l], w_route_group[l], b_route_group[l],
                         w_route_expert[l], b_route_expert[l], w_gate[l], w_up[l], w_down[l],
                         max(nbp, nbs))
        mod = _adaln(jnp.concatenate([c_prompt, c_sample], axis=0), w_ada[l], b_ada[l][None, :])
        mod3 = mod[:, None, :]
        xp, kp, vp, kip, srp, sip = _layer(xp, mod3[:nbp], p, None,
                                           tm_tok=512, tm_moe=1024, tc=32, tq=256, kb=256)
        past = (cache_k[l], cache_v[l], cache_kidx[l], state_ssm_re[l], state_ssm_im[l])
        xs, ks, vs, kis, srs, sis = _layer(xs, mod3[nbp:], p, past,
                                           tm_tok=512, tm_moe=512, tc=64, tq=128, kb=384)
        for lst, val in zip(outs, (kp, vp, kip, srp, sip, ks, vs, kis, srs, sis)):
            lst.append(val)
    return (xp, xs) + tuple(jnp.stack(o) for o in outs)
```

```python
import functools

import jax
import jax.numpy as jnp
import numpy as np
from jax import lax
from jax.experimental import pallas as pl
from jax.experimental.pallas import tpu as pltpu

F32 = jnp.float32
BF16 = jnp.bfloat16
I32 = jnp.int32

D_MODEL = 1024
CHUNK_SHIFT = 6
EPS = 1e-6
NEG_INF = -1e30
SSM_WIDTH = 512
SSM_GROUP = 16
SSM_GROUPS = 32
SSM_STATE = 64
SSM_LANES = SSM_GROUPS * SSM_STATE
S5_EXPAND = SSM_STATE // SSM_GROUP
S5_IN_BLOCK = 256
N_HEADS = 8
HEAD_DIM = 64
IDX_HEADS = 8
IDX_DIM = 32
IDX_WEIGHT_SCALE = (IDX_HEADS * IDX_DIM) ** -0.5
TOPK_MAX = 256
N_GROUPS = 4
EXPERTS_PER_GROUP = 4
N_EXPERTS = 16
EXPERT_DIM = 256
LANE = 128
INT_MIN = -(2 ** 31)
VMEM_LIMIT = 56 * 1024 * 1024

C_U = 0
C_Q = C_U + SSM_WIDTH
C_K = C_Q + N_HEADS * LANE
C_V = C_K + LANE
C_QI = C_V + LANE
C_KI = C_QI + IDX_HEADS * LANE
C_WI = C_KI + LANE
C_GA = C_WI + LANE
C_GB = C_GA + D_MODEL
C_END = C_GB + D_MODEL


def _dot(a, b):
    return jnp.dot(a, b, preferred_element_type=F32)


def _dot_nt(a, b):
    return lax.dot_general(a, b, (((1,), (1,)), ((), ())), preferred_element_type=F32)


def _split_bf16(x):
    hi = x.astype(BF16)
    lo = (x - hi.astype(F32)).astype(BF16)
    return hi, lo


def _dot_hilo(a, b):
    ah, al = _split_bf16(a)
    bh, bl = _split_bf16(b)
    return _dot(ah, bh) + _dot(ah, bl) + _dot(al, bh)


def _tiling(tm, t_len):
    return (t_len // tm, 1) if tm < t_len else (1, tm // t_len)


def _mod_spec(tm, t_len):
    per, nbt = _tiling(tm, t_len)
    return pl.BlockSpec((nbt, 1, 6 * D_MODEL), lambda i, *_: (i // per, 0, 0))


def _tm_spec(tm, t_len):
    per, nbt = _tiling(tm, t_len)
    return pl.BlockSpec((tm // nbt, nbt * SSM_WIDTH), lambda i, *_: (i % per, i // per))


def _modulate(x, mod, scale_seg, shift_seg):
    nb = mod.shape[0]
    tm = x.shape[0]
    sc = mod[:, :, scale_seg * D_MODEL:(scale_seg + 1) * D_MODEL]
    sh = mod[:, :, shift_seg * D_MODEL:(shift_seg + 1) * D_MODEL]
    x3 = x.reshape(nb, tm // nb, D_MODEL)
    return (x3 * (1.0 + sc) + sh).reshape(tm, D_MODEL)


def _gate_mul(x, mod, seg):
    nb = mod.shape[0]
    tm = x.shape[0]
    g = mod[:, :, seg * D_MODEL:(seg + 1) * D_MODEL]
    return (x.reshape(nb, tm // nb, D_MODEL) * g).reshape(tm, D_MODEL)


def _rms(x, n):
    return x * lax.rsqrt(jnp.sum(x * x, axis=-1, keepdims=True) * (1.0 / n) + EPS)


def _adaln_body(c_ref, w_ref, b_ref, o_ref):
    c = c_ref[...]
    s = c * jax.nn.sigmoid(c)
    o_ref[...] = _dot_hilo(s, w_ref[...]) + b_ref[...]


def _adaln(c, w, b):
    nb = c.shape[0]
    n = w.shape[1]
    bn = 512
    return pl.pallas_call(
        _adaln_body,
        grid=(n // bn,),
        in_specs=[pl.BlockSpec((nb, D_MODEL), lambda j: (0, 0)),
                  pl.BlockSpec((D_MODEL, bn), lambda j: (0, j)),
                  pl.BlockSpec((1, bn), lambda j: (0, j))],
        out_specs=pl.BlockSpec((nb, bn), lambda j: (0, j)),
        out_shape=jax.ShapeDtypeStruct((nb, n), F32),
        name="adaln",
    )(c, w, b)


def _proj_body(x_ref, mod_ref, w_ref, qg_ref, kg_ref, ig_ref,
               u_ref, q_ref, qc_ref, k_ref, v_ref, ki_ref, kp_ref, kc_ref, vt_ref, wt_ref,
               ga_ref, gb_ref, *, nbt, reps, kbv):
    tm = x_ref.shape[0]
    tt = tm // nbt
    rows = lambda b: slice(tt * b, tt * (b + 1))
    hsl = lambda h: slice(LANE * h, LANE * (h + 1))
    lane = lax.broadcasted_iota(I32, (tm, LANE), 1)
    hb = _modulate(_rms(x_ref[...], D_MODEL), mod_ref[...], 1, 0).astype(BF16)

    u = _dot(hb, w_ref[:, C_U:C_Q])
    for b in range(nbt):
        u_ref[:, SSM_WIDTH * b:SSM_WIDTH * (b + 1)] = u[rows(b), :].astype(BF16)

    def put_heads(ref, pieces):
        for b in range(nbt):
            for h in range(N_HEADS):
                for r in range(reps):
                    ref[b, h, tt * r:tt * (r + 1), :] = pieces[h][rows(b), :]

    q = _dot(hb, w_ref[:, C_Q:C_K])
    put_heads(q_ref, [(_rms(q[:, hsl(h)], HEAD_DIM) * qg_ref[...]).astype(BF16)
                      for h in range(N_HEADS)])

    qi_hi, qi_lo = _split_bf16(_dot(hb, w_ref[:, C_QI:C_KI]))
    put_heads(qc_ref, [jnp.where(lane < 2 * IDX_DIM, qi_hi[:, hsl(h)], qi_lo[:, hsl(h)])
                       for h in range(IDX_HEADS)])

    kn = _rms(_dot(hb, w_ref[:, C_K:C_V]), HEAD_DIM) * kg_ref[...]
    k_ref[...] = kn[:, :HEAD_DIM]
    kp_ref[...] = kn.astype(BF16)

    vfull = _dot(hb, w_ref[:, C_V:C_QI])
    v_ref[...] = vfull[:, :HEAD_DIM]
    vt = vfull.T[:HEAD_DIM].astype(BF16)
    for b in range(nbt):
        for c in range(tt // kbv):
            vt_ref[b, c] = vt[:, tt * b + kbv * c:tt * b + kbv * (c + 1)]

    kin = _rms(_dot(hb, w_ref[:, C_KI:C_WI]), LANE) * ig_ref[...]
    ki_ref[...] = kin[:, :IDX_DIM]
    ki_hi, ki_lo = _split_bf16(kin)
    kc_ref[...] = jnp.where(((lane >> 5) & 1) == 1, ki_lo, ki_hi)

    wt = (_dot(hb, w_ref[:, C_WI:C_GA]) * IDX_WEIGHT_SCALE).T[:IDX_HEADS]
    for b in range(nbt):
        for r in range(reps):
            wt_ref[b, :, tt * r:tt * (r + 1)] = wt[:, rows(b)]

    ga_ref[...] = jax.nn.sigmoid(_dot(hb, w_ref[:, C_GA:C_GB])).astype(BF16)
    gb_ref[...] = jax.nn.sigmoid(_dot(hb, w_ref[:, C_GB:C_END])).astype(BF16)


def _proj(xf, mod3, w_pad, qg, kg, ig, nb, t_len, tm, reps, kbv):
    n = xf.shape[0]
    per, nbt = _tiling(tm, t_len)
    tt = tm // nbt
    row = lambda w: pl.BlockSpec((tm, w), lambda i: (i, 0))
    const = lambda s: pl.BlockSpec(s, lambda i: (0, 0))
    head_spec = pl.BlockSpec((nbt, N_HEADS, tt * reps, LANE), lambda i: (i // per, 0, i % per, 0))
    head_shape = jax.ShapeDtypeStruct((nb, N_HEADS, t_len * reps, LANE), BF16)
    outs = [
        (_tm_spec(tm, t_len), jax.ShapeDtypeStruct((t_len, nb * SSM_WIDTH), BF16)),
        (head_spec, head_shape), (head_spec, head_shape),
        (row(HEAD_DIM), jax.ShapeDtypeStruct((n, HEAD_DIM), F32)),
        (row(HEAD_DIM), jax.ShapeDtypeStruct((n, HEAD_DIM), F32)),
        (row(IDX_DIM), jax.ShapeDtypeStruct((n, IDX_DIM), F32)),
        (row(LANE), jax.ShapeDtypeStruct((n, LANE), BF16)),
        (row(LANE), jax.ShapeDtypeStruct((n, LANE), BF16)),
        (pl.BlockSpec((nbt, tt // kbv, HEAD_DIM, kbv), lambda i: (i // per, i % per, 0, 0)),
         jax.ShapeDtypeStruct((nb, t_len // kbv, HEAD_DIM, kbv), BF16)),
        (pl.BlockSpec((nbt, IDX_HEADS, tt * reps), lambda i: (i // per, 0, i % per)),
         jax.ShapeDtypeStruct((nb, IDX_HEADS, t_len * reps), F32)),
        (row(D_MODEL), jax.ShapeDtypeStruct((n, D_MODEL), BF16)),
        (row(D_MODEL), jax.ShapeDtypeStruct((n, D_MODEL), BF16)),
    ]
    return pl.pallas_call(
        functools.partial(_proj_body, nbt=nbt, reps=reps, kbv=kbv),
        grid=(n // tm,),
        in_specs=[row(D_MODEL), _mod_spec(tm, t_len), const((D_MODEL, C_END)),
                  const((1, LANE)), const((1, LANE)), const((1, LANE))],
        out_specs=[s for s, _ in outs],
        out_shape=[s for _, s in outs],
        compiler_params=pltpu.CompilerParams(dimension_semantics=("parallel",),
                                             vmem_limit_bytes=VMEM_LIMIT),
        name="proj",
    )(xf, mod3, w_pad, qg, kg, ig)


def _s5_body(u_ref, h0r_ref, h0i_ref, bre_ref, bim_ref, cre_ref, cim_ref, lre_ref, lim_ref, d_ref,
             z_ref, sr_ref, si_ref, xr, xi, hr, hi, *, nb, tc, lane_chunk):
    @pl.when(pl.program_id(0) == 0)
    def _():
        hr[...] = h0r_ref[...]
        hi[...] = h0i_ref[...]

    ub = u_ref[...]
    for c in range(SSM_WIDTH // S5_IN_BLOCK):
        cin = slice(S5_IN_BLOCK * c, S5_IN_BLOCK * (c + 1))
        cst = slice(S5_IN_BLOCK * S5_EXPAND * c, S5_IN_BLOCK * S5_EXPAND * (c + 1))
        xr[:, cst] = _dot(ub[:, cin], bre_ref[cin, cst])
        xi[:, cst] = _dot(ub[:, cin], bim_ref[cin, cst])

    for lc in range(SSM_LANES // lane_chunk):
        cols = slice(lc * lane_chunk, (lc + 1) * lane_chunk)

        def step(t, carry, cols=cols):
            a, b = carry
            rows = pl.ds(pl.multiple_of(t * nb, nb), nb)
            lr = lre_ref[:, cols]
            li = lim_ref[:, cols]
            na = lr * a - li * b + xr[rows, cols]
            nb_ = lr * b + li * a + xi[rows, cols]
            xr[rows, cols] = na
            xi[rows, cols] = nb_
            return na, nb_

        a, b = lax.fori_loop(0, tc, step, (hr[:, cols], hi[:, cols]), unroll=2)
        hr[:, cols] = a
        hi[:, cols] = b

    for c in range(SSM_WIDTH // LANE):
        cy = slice(LANE * c, LANE * (c + 1))
        cst = slice(LANE * S5_EXPAND * c, LANE * S5_EXPAND * (c + 1))
        y = (_dot(xr[:, cst].astype(BF16), cre_ref[cst, cy])
             + _dot(xi[:, cst].astype(BF16), cim_ref[cst, cy]))
        y = y + d_ref[:, cy] * ub[:, cy].astype(F32)
        z_ref[:, cy] = jax.nn.gelu(y, approximate=True).astype(BF16)
    sr_ref[...] = hr[...]
    si_ref[...] = hi[...]


def _s5(u_tm, h0r, h0i, ssm, nb, t_len, tc):
    bre, bim, cre, cim, lre, lim, dsk = ssm
    rows = tc * nb
    const = lambda s: pl.BlockSpec(s, lambda c: (0, 0))
    return pl.pallas_call(
        functools.partial(_s5_body, nb=nb, tc=tc, lane_chunk=512),
        grid=(t_len // tc,),
        in_specs=[pl.BlockSpec((rows, SSM_WIDTH), lambda c: (c, 0)),
                  const((nb, SSM_LANES)), const((nb, SSM_LANES)),
                  const((SSM_WIDTH, SSM_LANES)), const((SSM_WIDTH, SSM_LANES)),
                  const((SSM_LANES, SSM_WIDTH)), const((SSM_LANES, SSM_WIDTH)),
                  const((nb, SSM_LANES)), const((nb, SSM_LANES)), const((1, SSM_WIDTH))],
        out_specs=[pl.BlockSpec((rows, SSM_WIDTH), lambda c: (c, 0)),
                   const((nb, SSM_LANES)), const((nb, SSM_LANES))],
        out_shape=[jax.ShapeDtypeStruct((t_len * nb, SSM_WIDTH), BF16),
                   jax.ShapeDtypeStruct((nb, SSM_LANES), F32),
                   jax.ShapeDtypeStruct((nb, SSM_LANES), F32)],
        scratch_shapes=[pltpu.VMEM((rows, SSM_LANES), F32), pltpu.VMEM((rows, SSM_LANES), F32),
                        pltpu.VMEM((nb, SSM_LANES), F32), pltpu.VMEM((nb, SSM_LANES), F32)],
        compiler_params=pltpu.CompilerParams(dimension_semantics=("arbitrary",),
                                             vmem_limit_bytes=VMEM_LIMIT),
        name="s5",
    )(u_tm, h0r, h0i, bre, bim, cre, cim, lre[:nb], lim[:nb], dsk)


def _dsa_body(q_ref, qc_ref, wt_ref, kc_ref, k_ref, vt_ref, wo_ref, o_ref,
              sc_ref, lg_ref, acc_ref, at_ref, st0_ref, st1_ref, bias_ref, hs_ref,
              *, tq, kb, s_real, s_pad, q_off, q_wrap, topk):
    j = pl.program_id(1)
    hq = N_HEADS * tq
    n_all = s_pad // kb
    cb = 2 * kb
    lane_q = lax.broadcasted_iota(I32, (1, tq), 1)
    qpos = q_off + j * tq + (lane_q & (q_wrap - 1))
    qchunk = qpos >> CHUNK_SHIFT
    last_end = (((q_off + j * tq + min(tq, q_wrap) - 1) >> CHUNK_SHIFT) + 1) << CHUNK_SHIFT
    npair = jnp.minimum(n_all // 2, lax.div(last_end + (cb - 1), cb))

    def blk(b):
        return pl.ds(pl.multiple_of(b * kb, kb), kb)

    def pair(i):
        return pl.ds(pl.multiple_of(i * cb, cb), cb)

    def admissible(b):
        kpos = b * kb + lax.broadcasted_iota(I32, (kb, tq), 0)
        return ((kpos >> CHUNK_SHIFT) <= qchunk) & (kpos < s_real), kpos

    def fold8(x, op):
        return op(x.reshape(x.shape[0] // 8, 8, x.shape[-1]), axis=0)

    hsl = lambda h: slice(h * tq, (h + 1) * tq)

    def staged(mm, consume, carry):
        st0_ref[...] = mm(0)

        def body(i, c):
            b0 = 2 * i
            st1_ref[...] = mm(b0 + 1)
            c = consume(b0, st0_ref, c)
            st0_ref[...] = mm(jnp.minimum(b0 + 2, n_all - 1))
            return consume(b0 + 1, st1_ref, c)

        return lax.fori_loop(0, npair, body, carry)

    qcs = qc_ref[0].reshape(hq, LANE)
    wt = wt_ref[0]

    rc = 64

    def sub(b, r):
        return pl.ds(pl.multiple_of(b * kb + r * rc, rc), rc)

    def admissible_rows(b, r):
        kpos = b * kb + r * rc + lax.broadcasted_iota(I32, (rc, tq), 0)
        return ((kpos >> CHUNK_SHIFT) <= qchunk) & (kpos < s_real), kpos

    def score_consume(b, st, carry):
        for r in range(kb // rc):
            sc = jnp.zeros((rc, tq), F32)
            for h in range(IDX_HEADS):
                sc = sc + wt[h:h + 1, :] * jnp.maximum(st[r * rc:(r + 1) * rc, hsl(h)], 0.0)
            sc = jnp.where(admissible_rows(b, r)[0], sc, NEG_INF)
            sc_ref[sub(b, r), :] = sc
            hs_ref[sub(b, r), :] = sc.astype(BF16)
        return carry

    staged(lambda b: _dot_nt(kc_ref[0, blk(b), :], qcs), score_consume, 0)

    def as_float(u):
        s = u ^ INT_MIN
        return lax.bitcast_convert_type(s ^ ((s >> 31) & 0x7FFFFFFF), F32)

    def count(pred):
        def body(i, acc):
            for r in range(cb // 32):
                row0 = i * cb + 32 * r
                kpos = lambda row0=row0: row0 + lax.broadcasted_iota(I32, (32, tq), 0)
                x = sc_ref[pl.ds(pl.multiple_of(row0, 32), 32), :]
                acc = acc + jnp.where(pred(x, kpos), 1.0, 0.0)
            return acc
        acc = lax.fori_loop(0, npair, body, jnp.zeros((32, tq), F32))
        return jnp.sum(acc, axis=0, keepdims=True)

    def count_rounded(cb16):
        def body(i, acc):
            for r in range(cb // 32):
                h = hs_ref[pl.ds(pl.multiple_of(i * cb + 32 * r, 32), 32), :]
                hit = jnp.where(h >= cb16, jnp.ones_like(h), jnp.zeros_like(h))
                acc = acc + hit
            return acc
        acc = lax.fori_loop(0, npair, body, jnp.zeros((32, tq), BF16))
        return jnp.sum(acc.astype(F32), axis=0, keepdims=True)

    def bf16_point(top):
        return top | jnp.where(top < 0, 0, 0xFFFF)

    def coarse_body(i, prefix):
        cand = prefix | (jnp.int32(1) << (31 - i))
        cnt = count_rounded(as_float(bf16_point(cand)).astype(BF16))
        return jnp.where(cnt >= topk, cand, prefix)

    coarse = bf16_point(lax.fori_loop(0, 16, coarse_body, jnp.zeros((1, tq), I32)))

    half_span = 2 ** 15 + 1

    def fine_body(i, lohi):
        lo, hi = lohi
        mid = lo + ((hi - lo + 1) >> 1)
        cf = as_float(mid)
        ok = count(lambda x, kpos: x >= cf) >= topk
        return jnp.where(ok, mid, lo), jnp.where(ok, hi, mid - 1)

    lo, _ = lax.fori_loop(0, 17, fine_body, (coarse - half_span, coarse + half_span))
    thr = as_float(lo)

    n_gt = count(lambda x, kpos: x > thr)
    n_eq = count(lambda x, kpos: x == thr)
    need = topk - n_gt
    any_over = jnp.max(n_eq - need) > 0.0
    idx_bits = int(s_pad).bit_length()

    def tie_search():
        def tbody(i, lim):
            cand = lim | (jnp.int32(1) << (idx_bits - 1 - i))
            n = count(lambda x, kpos: (x == thr) & (kpos() < cand))
            return jnp.where(n <= need, cand, lim)
        return lax.fori_loop(0, idx_bits, tbody, jnp.zeros((1, tq), I32))

    lim = lax.cond(any_over, tie_search, lambda: jnp.full((1, tq), 2 ** idx_bits - 1, I32))

    qs = q_ref[0].reshape(hq, LANE)

    def logits_pair(i, m):
        for half in range(2):
            b = 2 * i + half
            for r in range(kb // rc):
                x = sc_ref[sub(b, r), :]
                adm, kpos = admissible_rows(b, r)
                sel = ((x > thr) | ((x == thr) & (kpos < lim))) & adm
                bias_ref[half * kb + r * rc:half * kb + (r + 1) * rc, :] = jnp.where(sel, 0.0, NEG_INF)
        tops = [m[:, hsl(h)] for h in range(N_HEADS)]
        for half in range(2):
            b = 2 * i + half
            raw = _dot_nt(k_ref[0, blk(b), :], qs)
            for h in range(N_HEADS):
                lh = raw[:, hsl(h)] + bias_ref[half * kb:(half + 1) * kb, :]
                lg_ref[blk(b), hsl(h)] = lh
                tops[h] = jnp.maximum(tops[h], fold8(lh, jnp.max))
        return jnp.concatenate(tops, axis=1)

    m8 = lax.fori_loop(0, npair, logits_pair, jnp.full((8, hq), NEG_INF, F32))
    m = jnp.max(m8, axis=0, keepdims=True)

    acc_ref[...] = jnp.zeros_like(acc_ref)

    def pv_pair(i, ssum):
        p = jnp.exp2(lg_ref[pair(i), :] - m)
        acc_ref[...] += _dot(vt_ref[0, i], p.astype(BF16))
        return ssum + fold8(p, jnp.sum)

    s8 = lax.fori_loop(0, npair, pv_pair, jnp.zeros((8, hq), F32))
    out_t = acc_ref[...] / jnp.sum(s8, axis=0, keepdims=True)
    for h in range(N_HEADS):
        at_ref[HEAD_DIM * h:HEAD_DIM * (h + 1), :] = out_t[:, hsl(h)]
    o_ref[0] = _dot(at_ref[...].T.astype(BF16), wo_ref[...]).astype(BF16)


def _dsa(q, qc, wt, kc, kp, vt, wo, *, tq, kb, s_real, q_off, q_wrap):
    nb, _, t_len, _ = q.shape
    s_pad = kc.shape[1]
    cb = 2 * kb
    topk = min(TOPK_MAX, s_real // 4)
    hq = N_HEADS * tq
    qspec = pl.BlockSpec((1, N_HEADS, tq, LANE), lambda b, j: (b, 0, j, 0))
    return pl.pallas_call(
        functools.partial(_dsa_body, tq=tq, kb=kb, s_real=s_real, s_pad=s_pad, q_off=q_off,
                          q_wrap=q_wrap, topk=float(topk)),
        grid=(nb, t_len // tq),
        in_specs=[qspec, qspec,
                  pl.BlockSpec((1, IDX_HEADS, tq), lambda b, j: (b, 0, j)),
                  pl.BlockSpec((1, s_pad, LANE), lambda b, j: (b, 0, 0)),
                  pl.BlockSpec((1, s_pad, LANE), lambda b, j: (b, 0, 0)),
                  pl.BlockSpec((1, s_pad // cb, HEAD_DIM, cb), lambda b, j: (b, 0, 0, 0)),
                  pl.BlockSpec((N_HEADS * HEAD_DIM, D_MODEL), lambda b, j: (0, 0))],
        out_specs=pl.BlockSpec((1, tq, D_MODEL), lambda b, j: (b, j, 0)),
        out_shape=jax.ShapeDtypeStruct((nb, t_len, D_MODEL), BF16),
        scratch_shapes=[pltpu.VMEM((s_pad, tq), F32), pltpu.VMEM((s_pad, hq), F32),
                        pltpu.VMEM((HEAD_DIM, hq), F32), pltpu.VMEM((N_HEADS * HEAD_DIM, tq), F32),
                        pltpu.VMEM((kb, hq), F32), pltpu.VMEM((kb, hq), F32),
                        pltpu.VMEM((cb, tq), F32), pltpu.VMEM((s_pad, tq), BF16)],
        compiler_params=pltpu.CompilerParams(dimension_semantics=("parallel", "parallel"),
                                             vmem_limit_bytes=VMEM_LIMIT),
        name="dsa",
    )(q, qc, wt, kc, kp, vt, wo)


def _merge_body(x_ref, z_ref, ga_ref, gb_ref, ob_ref, mod_ref, wglu_ref, wout_ref, o_ref, *, nbt):
    z = z_ref[...]
    if nbt > 1:
        z = jnp.concatenate([z[:, SSM_WIDTH * b:SSM_WIDTH * (b + 1)] for b in range(nbt)], axis=0)
    glu = _dot(z, wglu_ref[...])
    out_a = glu[:, :D_MODEL] * jax.nn.sigmoid(glu[:, D_MODEL:])
    merged = ga_ref[...] * out_a + gb_ref[...] * ob_ref[...]
    upd = _dot(merged.astype(BF16), wout_ref[...])
    o_ref[...] = x_ref[...] + _gate_mul(upd, mod_ref[...], 2)


def _merge(xf, y_tm, ga, gb, ob, mod3, wglu, wout, t_len, tm):
    n = xf.shape[0]
    row = lambda w: pl.BlockSpec((tm, w), lambda i: (i, 0))
    const = lambda s: pl.BlockSpec(s, lambda i: (0, 0))
    return pl.pallas_call(
        functools.partial(_merge_body, nbt=_tiling(tm, t_len)[1]),
        grid=(n // tm,),
        in_specs=[row(D_MODEL), _tm_spec(tm, t_len), row(D_MODEL), row(D_MODEL), row(D_MODEL),
                  _mod_spec(tm, t_len), const((SSM_WIDTH, 2 * D_MODEL)), const((D_MODEL, D_MODEL))],
        out_specs=row(D_MODEL),
        out_shape=jax.ShapeDtypeStruct((n, D_MODEL), F32),
        compiler_params=pltpu.CompilerParams(dimension_semantics=("parallel",),
                                             vmem_limit_bytes=VMEM_LIMIT),
        name="merge",
    )(xf, y_tm, ga, gb, ob, mod3, wglu, wout)


def _moe_body(x_ref, mod_ref, wr_ref, br_ref, wg_ref, wu_ref, wd_ref, o_ref, hb_ref, comb_ref, acc_ref):
    e = pl.program_id(1)
    tm = x_ref.shape[0]
    lane = lax.broadcasted_iota(I32, (tm, LANE), 1).astype(F32)

    @pl.when(e == 0)
    def _():
        h2 = _modulate(_rms(x_ref[...], D_MODEL), mod_ref[...], 4, 3)
        hb_ref[...] = h2.astype(BF16)
        lg = _dot_hilo(h2, wr_ref[...]) + br_ref[...]
        neg = -jnp.inf
        gl = jnp.where(lane < N_GROUPS, lg, neg)
        gmax = jnp.max(gl, axis=-1, keepdims=True)
        g_idx = jnp.min(jnp.where(gl == gmax, lane, float(LANE)), axis=-1, keepdims=True)
        g_w = 1.0 / jnp.sum(jnp.where(lane < N_GROUPS, jnp.exp(lg - gmax), 0.0),
                            axis=-1, keepdims=True)
        lo = N_GROUPS + EXPERTS_PER_GROUP * g_idx
        el = jnp.where((lane >= lo) & (lane < lo + EXPERTS_PER_GROUP), lg, neg)
        v1 = jnp.max(el, axis=-1, keepdims=True)
        i1 = jnp.min(jnp.where(el == v1, lane, float(LANE)), axis=-1, keepdims=True)
        el2 = jnp.where(lane == i1, neg, el)
        v2 = jnp.max(el2, axis=-1, keepdims=True)
        i2 = jnp.min(jnp.where(el2 == v2, lane, float(LANE)), axis=-1, keepdims=True)
        e21 = jnp.exp(v2 - v1)
        w1 = g_w / (1.0 + e21)
        w2 = g_w * e21 / (1.0 + e21)
        comb_ref[...] = jnp.where(lane == i1, w1, 0.0) + jnp.where(lane == i2, w2, 0.0)
        acc_ref[...] = jnp.zeros_like(acc_ref)

    hb = hb_ref[...]
    comb = comb_ref[...]
    parts = []
    for k in range(EXPERTS_PER_GROUP):
        gate = _dot(hb, wg_ref[k])
        hid = gate * jax.nn.sigmoid(gate) * _dot(hb, wu_ref[k])
        lane_e = (N_GROUPS + EXPERTS_PER_GROUP * e + k).astype(F32)
        col = jnp.sum(jnp.where(lane == lane_e, comb, 0.0), axis=-1, keepdims=True)
        parts.append((hid * col).astype(BF16))
    acc_ref[...] += _dot(jnp.concatenate(parts, axis=1), wd_ref[0])

    @pl.when(e == N_GROUPS - 1)
    def _():
        o_ref[...] = x_ref[...] + _gate_mul(acc_ref[...], mod_ref[...], 5)


def _moe(x1, mod3, wr, br, wg, wu, wd, t_len, tm):
    n = x1.shape[0]
    gate_up = pl.BlockSpec((EXPERTS_PER_GROUP, D_MODEL, EXPERT_DIM), lambda i, e: (e, 0, 0))
    return pl.pallas_call(
        _moe_body,
        grid=(n // tm, N_GROUPS),
        in_specs=[pl.BlockSpec((tm, D_MODEL), lambda i, e: (i, 0)),
                  _mod_spec(tm, t_len),
                  pl.BlockSpec((D_MODEL, LANE), lambda i, e: (0, 0)),
                  pl.BlockSpec((1, LANE), lambda i, e: (0, 0)),
                  gate_up, gate_up,
                  pl.BlockSpec((1, EXPERTS_PER_GROUP * EXPERT_DIM, D_MODEL),
                               lambda i, e: (e, 0, 0))],
        out_specs=pl.BlockSpec((tm, D_MODEL), lambda i, e: (i, 0)),
        out_shape=jax.ShapeDtypeStruct((n, D_MODEL), F32),
        scratch_shapes=[pltpu.VMEM((tm, D_MODEL), BF16), pltpu.VMEM((tm, LANE), F32),
                        pltpu.VMEM((tm, D_MODEL), F32)],
        compiler_params=pltpu.CompilerParams(dimension_semantics=("parallel", "arbitrary"),
                                             vmem_limit_bytes=VMEM_LIMIT),
        name="moe",
    )(x1, mod3, wr, br, wg, wu, wd)


def _prep_params(w_in, ssm_a_re, ssm_a_im, ssm_log_dt, ssm_b_re, ssm_b_im, ssm_c_re, ssm_c_im, ssm_d,
                 w_glu, q_gain, k_gain, kidx_gain, w_attn_out, w_out, w_rg, b_rg, w_re, b_re,
                 w_gate, w_up, w_down, max_batch):
    offs = np.cumsum([0, SSM_WIDTH, N_HEADS * HEAD_DIM, HEAD_DIM, HEAD_DIM, IDX_HEADS * IDX_DIM,
                      IDX_DIM, IDX_HEADS, D_MODEL, D_MODEL])
    seg = lambda i: w_in[:, offs[i]:offs[i + 1]]
    padto = lambda a, w: jnp.pad(a, ((0, 0), (0, w - a.shape[-1])))
    wq = jnp.pad(seg(1).reshape(D_MODEL, N_HEADS, HEAD_DIM), ((0, 0), (0, 0), (0, LANE - HEAD_DIM)))
    wqi = jnp.tile(seg(4).reshape(D_MODEL, IDX_HEADS, IDX_DIM), (1, 1, LANE // IDX_DIM))
    w_pad = jnp.concatenate(
        [seg(0), wq.reshape(D_MODEL, N_HEADS * LANE), padto(seg(2), LANE), padto(seg(3), LANE),
         wqi.reshape(D_MODEL, IDX_HEADS * LANE), jnp.tile(seg(5), (1, LANE // IDX_DIM)),
         padto(seg(6), LANE), seg(7), seg(8)], axis=1).astype(BF16)
    qg = padto((q_gain * (HEAD_DIM ** -0.5 * np.log2(np.e)))[None, :], LANE)
    kg = padto(k_gain[None, :], LANE)
    ig = jnp.tile(kidx_gain[None, :], (1, LANE // IDX_DIM))

    dt = jnp.exp(ssm_log_dt)[:, None]
    decay = jnp.exp(ssm_a_re * dt)
    lam_re = decay * jnp.cos(ssm_a_im * dt)
    lam_im = decay * jnp.sin(ssm_a_im * dt)
    den = ssm_a_re * ssm_a_re + ssm_a_im * ssm_a_im
    num_re = lam_re - 1.0
    coef_re = (num_re * ssm_a_re + lam_im * ssm_a_im) / den
    coef_im = (lam_im * ssm_a_re - num_re * ssm_a_im) / den
    bb_re = coef_re[..., None] * ssm_b_re - coef_im[..., None] * ssm_b_im
    bb_im = coef_re[..., None] * ssm_b_im + coef_im[..., None] * ssm_b_re
    eye = jnp.eye(SSM_GROUPS, dtype=F32)

    def in_map(bb):
        return jnp.einsum('gpj,gh->gjhp', bb, eye).reshape(SSM_WIDTH, SSM_LANES).astype(BF16)

    def out_map(c):
        return jnp.einsum('gjp,gh->gphj', c, eye).reshape(SSM_LANES, SSM_WIDTH).astype(BF16)

    bcast = lambda a: jnp.broadcast_to(a.reshape(1, SSM_LANES), (max_batch, SSM_LANES))
    ssm = (in_map(bb_re), in_map(bb_im), out_map(ssm_c_re), out_map(-ssm_c_im),
           bcast(lam_re), bcast(lam_im), ssm_d[None, :])

    wr = jnp.pad(jnp.concatenate([w_rg, w_re], axis=1), ((0, 0), (0, LANE - N_GROUPS - N_EXPERTS)))
    br = jnp.pad(jnp.concatenate([b_rg, b_re]), (0, LANE - N_GROUPS - N_EXPERTS))[None, :]
    wd = w_down.astype(BF16).reshape(N_GROUPS, EXPERTS_PER_GROUP * EXPERT_DIM, D_MODEL)
    return dict(w_pad=w_pad, qg=qg, kg=kg, ig=ig, ssm=ssm,
                wglu=w_glu.astype(BF16), wo=w_attn_out.astype(BF16), wout=w_out.astype(BF16),
                wr=wr, br=br, wg=w_gate.astype(BF16), wu=w_up.astype(BF16), wd=wd)


def _layer(x, mod3, p, past, *, tm_tok, tm_moe, tc, tq, kb):
    nb, t_len, _ = x.shape
    n = nb * t_len
    cb = 2 * kb
    reps = max(tq // t_len, 1)
    kbv = min(cb, t_len)
    xf = x.reshape(n, D_MODEL)
    u_tm, q, qc, k, v, ki, kp, kc, vt, wt, ga, gb = _proj(
        xf, mod3, p["w_pad"], p["qg"], p["kg"], p["ig"], nb, t_len, tm_tok, reps, kbv)
    k3 = k.reshape(nb, t_len, HEAD_DIM)
    v3 = v.reshape(nb, t_len, HEAD_DIM)
    ki3 = ki.reshape(nb, t_len, IDX_DIM)
    kp = kp.reshape(nb, t_len, LANE)
    kc = kc.reshape(nb, t_len, LANE)

    if past is None:
        h0r = jnp.zeros((nb, SSM_LANES), F32)
        h0i = h0r
        s_real, q_off = t_len, 0
    else:
        past_k, past_v, past_ki, h0_re, h0_im = past
        h0r = h0_re.reshape(nb, SSM_LANES)
        h0i = h0_im.reshape(nb, SSM_LANES)
        q_off = past_k.shape[1]
        s_real = q_off + t_len
        s_pad = -(-s_real // cb) * cb
        fill = lambda a: jnp.pad(a, ((0, 0), (0, s_pad - s_real), (0, 0)))
        pk = jnp.pad(past_k.astype(BF16), ((0, 0), (0, 0), (0, LANE - HEAD_DIM)))
        kp = fill(jnp.concatenate([pk, kp], axis=1))
        pi_hi, pi_lo = _split_bf16(past_ki)
        kc = fill(jnp.concatenate([jnp.concatenate([pi_hi, pi_lo, pi_hi, pi_lo], axis=-1), kc], axis=1))
        vt_all = jnp.concatenate([past_v.astype(BF16).transpose(0, 2, 1),
                                  vt.transpose(0, 2, 1, 3).reshape(nb, HEAD_DIM, t_len)], axis=2)
        vt_all = jnp.pad(vt_all, ((0, 0), (0, 0), (0, s_pad - s_real)))
        vt = vt_all.reshape(nb, HEAD_DIM, s_pad // cb, cb).transpose(0, 2, 1, 3)

    y_tm, s_re, s_im = _s5(u_tm.reshape(n, SSM_WIDTH), h0r, h0i, p["ssm"], nb, t_len, tc)
    ob = _dsa(q, qc, wt, kc, kp, vt, p["wo"], tq=tq, kb=kb, s_real=s_real, q_off=q_off,
              q_wrap=t_len if reps > 1 else tq)
    ob = ob[:, :t_len].reshape(n, D_MODEL)
    x1 = _merge(xf, y_tm.reshape(t_len, nb * SSM_WIDTH), ga, gb, ob, mod3, p["wglu"], p["wout"],
                t_len, tm_tok)
    x2 = _moe(x1, mod3, p["wr"], p["br"], p["wg"], p["wu"], p["wd"], t_len, tm_moe)
    return (x2.reshape(nb, t_len, D_MODEL), k3, v3, ki3,
            s_re.reshape(nb, SSM_GROUPS, SSM_STATE), s_im.reshape(nb, SSM_GROUPS, SSM_STATE))


def kernel(x_prompt, x_sample, cache_k, cache_v, cache_kidx, state_ssm_re, state_ssm_im, c_prompt, c_sample, w_ada, b_ada, w_in, ssm_a_re, ssm_a_im, ssm_log_dt, ssm_b_re, ssm_b_im, ssm_c_re, ssm_c_im, ssm_d, w_glu, q_gain, k_gain, kidx_gain, w_attn_out, w_out, w_route_group, b_route_group, w_route_expert, b_route_expert, w_gate, w_up, w_down):
    depth = w_ada.shape[0]
    nbp = x_prompt.shape[0]
    nbs = x_sample.shape[0]
    xp, xs = x_prompt, x_sample
    outs = [[] for _ in range(10)]
    for l in range(depth):
        p = _prep_params(w_in[l], ssm_a_re[l], ssm_a_im[l], ssm_log_dt[l], ssm_b_re[l], ssm_b_im[l],
                         ssm_c_re[l], ssm_c_im[l], ssm_d[l], w_glu[l], q_gain[l], k_gain[l],
                         kidx_gain[l], w_attn_out[l], w_out[l], w_route_group[l], b_route_group[l],
                         w_route_expert[l], b_route_expert[l], w_gate[l], w_up[l], w_down[l],
                         max(nbp, nbs))
        mod = _adaln(jnp.concatenate([c_prompt, c_sample], axis=0), w_ada[l], b_ada[l][None, :])
        mod3 = mod[:, None, :]
        xp, kp, vp, kip, srp, sip = _layer(xp, mod3[:nbp], p, None,
                                           tm_tok=512, tm_moe=1024, tc=32, tq=256, kb=256)
        past = (cache_k[l], cache_v[l], cache_kidx[l], state_ssm_re[l], state_ssm_im[l])
        xs, ks, vs, kis, srs, sis = _layer(xs, mod3[nbp:], p, past,
                                           tm_tok=512, tm_moe=512, tc=64, tq=128, kb=384)
        for lst, val in zip(outs, (kp, vp, kip, srp, sip, ks, vs, kis, srs, sis)):
            lst.append(val)
    return (xp, xs) + tuple(jnp.stack(o) for o in outs)
```

```python
import functools

import jax
import jax.numpy as jnp
import numpy as np
from jax import lax
from jax.experimental import pallas as pl
from jax.experimental.pallas import tpu as pltpu

F32 = jnp.float32
BF16 = jnp.bfloat16
I32 = jnp.int32

D_MODEL = 1024
CHUNK_SHIFT = 6
EPS = 1e-6
NEG_INF = -1e30
SSM_WIDTH = 512
SSM_GROUP = 16
SSM_GROUPS = 32
SSM_STATE = 64
SSM_LANES = SSM_GROUPS * SSM_STATE
S5_EXPAND = SSM_STATE // SSM_GROUP
S5_IN_BLOCK = 256
N_HEADS = 8
HEAD_DIM = 64
IDX_HEADS = 8
IDX_DIM = 32
IDX_WEIGHT_SCALE = (IDX_HEADS * IDX_DIM) ** -0.5
TOPK_MAX = 256
N_GROUPS = 4
EXPERTS_PER_GROUP = 4
N_EXPERTS = 16
EXPERT_DIM = 256
LANE = 128
INT_MIN = -(2 ** 31)
VMEM_LIMIT = 56 * 1024 * 1024

C_U = 0
C_Q = C_U + SSM_WIDTH
C_K = C_Q + N_HEADS * LANE
C_V = C_K + LANE
C_QI = C_V + LANE
C_KI = C_QI + IDX_HEADS * LANE
C_WI = C_KI + LANE
C_GA = C_WI + LANE
C_GB = C_GA + D_MODEL
C_END = C_GB + D_MODEL


def _dot(a, b):
    return jnp.dot(a, b, preferred_element_type=F32)


def _dot_nt(a, b):
    return lax.dot_general(a, b, (((1,), (1,)), ((), ())), preferred_element_type=F32)


def _split_bf16(x):
    hi = x.astype(BF16)
    lo = (x - hi.astype(F32)).astype(BF16)
    return hi, lo


def _dot_hilo(a, b):
    ah, al = _split_bf16(a)
    bh, bl = _split_bf16(b)
    return _dot(ah, bh) + _dot(ah, bl) + _dot(al, bh)


def _tiling(tm, t_len):
    return (t_len // tm, 1) if tm < t_len else (1, tm // t_len)


def _mod_spec(tm, t_len):
    per, nbt = _tiling(tm, t_len)
    return pl.BlockSpec((nbt, 1, 6 * D_MODEL), lambda i, *_: (i // per, 0, 0))


def _modulate(x, mod, scale_seg, shift_seg):
    nb = mod.shape[0]
    tm = x.shape[0]
    sc = mod[:, :, scale_seg * D_MODEL:(scale_seg + 1) * D_MODEL]
    sh = mod[:, :, shift_seg * D_MODEL:(shift_seg + 1) * D_MODEL]
    x3 = x.reshape(nb, tm // nb, D_MODEL)
    return (x3 * (1.0 + sc) + sh).reshape(tm, D_MODEL)


def _gate_mul(x, mod, seg):
    nb = mod.shape[0]
    tm = x.shape[0]
    g = mod[:, :, seg * D_MODEL:(seg + 1) * D_MODEL]
    return (x.reshape(nb, tm // nb, D_MODEL) * g).reshape(tm, D_MODEL)


def _rms(x, n):
    return x * lax.rsqrt(jnp.sum(x * x, axis=-1, keepdims=True) * (1.0 / n) + EPS)


def _adaln_body(c_ref, w_ref, b_ref, o_ref):
    c = c_ref[...]
    s = c * jax.nn.sigmoid(c)
    o_ref[...] = _dot_hilo(s, w_ref[...]) + b_ref[...]


def _adaln(c, w, b):
    nb = c.shape[0]
    n = w.shape[1]
    bn = 512
    return pl.pallas_call(
        _adaln_body,
        grid=(n // bn,),
        in_specs=[pl.BlockSpec((nb, D_MODEL), lambda j: (0, 0)),
                  pl.BlockSpec((D_MODEL, bn), lambda j: (0, j)),
                  pl.BlockSpec((1, bn), lambda j: (0, j))],
        out_specs=pl.BlockSpec((nb, bn), lambda j: (0, j)),
        out_shape=jax.ShapeDtypeStruct((nb, n), F32),
        name="adaln",
    )(c, w, b)


def _proj_body(x_ref, mod_ref, w_ref, qg_ref, kg_ref, ig_ref,
               u_ref, q_ref, qc_ref, k_ref, v_ref, ki_ref, kp_ref, kc_ref, vt_ref, wt_ref,
               ga_ref, gb_ref, *, nbt, reps, kbv):
    tm = x_ref.shape[0]
    tt = tm // nbt
    rows = lambda b: slice(tt * b, tt * (b + 1))
    hsl = lambda h: slice(LANE * h, LANE * (h + 1))
    lane = lax.broadcasted_iota(I32, (tm, LANE), 1)
    hb = _modulate(_rms(x_ref[...], D_MODEL), mod_ref[...], 1, 0).astype(BF16)

    u_ref[...] = _dot(hb, w_ref[:, C_U:C_Q]).astype(BF16)

    def put_heads(ref, pieces):
        for b in range(nbt):
            for h in range(N_HEADS):
                for r in range(reps):
                    ref[b, h, tt * r:tt * (r + 1), :] = pieces[h][rows(b), :]

    q = _dot(hb, w_ref[:, C_Q:C_K])
    put_heads(q_ref, [(_rms(q[:, hsl(h)], HEAD_DIM) * qg_ref[...]).astype(BF16)
                      for h in range(N_HEADS)])

    qi_hi, qi_lo = _split_bf16(_dot(hb, w_ref[:, C_QI:C_KI]))
    put_heads(qc_ref, [jnp.where(lane < 2 * IDX_DIM, qi_hi[:, hsl(h)], qi_lo[:, hsl(h)])
                       for h in range(IDX_HEADS)])

    kn = _rms(_dot(hb, w_ref[:, C_K:C_V]), HEAD_DIM) * kg_ref[...]
    k_ref[...] = kn[:, :HEAD_DIM]
    kp_ref[...] = kn.astype(BF16)

    vfull = _dot(hb, w_ref[:, C_V:C_QI])
    v_ref[...] = vfull[:, :HEAD_DIM]
    vt = vfull.T[:HEAD_DIM].astype(BF16)
    for b in range(nbt):
        for c in range(tt // kbv):
            vt_ref[b, c] = vt[:, tt * b + kbv * c:tt * b + kbv * (c + 1)]

    kin = _rms(_dot(hb, w_ref[:, C_KI:C_WI]), LANE) * ig_ref[...]
    ki_ref[...] = kin[:, :IDX_DIM]
    ki_hi, ki_lo = _split_bf16(kin)
    kc_ref[...] = jnp.where(((lane >> 5) & 1) == 1, ki_lo, ki_hi)

    wt = (_dot(hb, w_ref[:, C_WI:C_GA]) * IDX_WEIGHT_SCALE).T[:IDX_HEADS]
    for b in range(nbt):
        for r in range(reps):
            wt_ref[b, :, tt * r:tt * (r + 1)] = wt[:, rows(b)]

    ga_ref[...] = jax.nn.sigmoid(_dot(hb, w_ref[:, C_GA:C_GB])).astype(BF16)
    gb_ref[...] = jax.nn.sigmoid(_dot(hb, w_ref[:, C_GB:C_END])).astype(BF16)


def _proj(xf, mod3, w_pad, qg, kg, ig, nb, t_len, tm, reps, kbv):
    n = xf.shape[0]
    per, nbt = _tiling(tm, t_len)
    tt = tm // nbt
    row = lambda w: pl.BlockSpec((tm, w), lambda i: (i, 0))
    const = lambda s: pl.BlockSpec(s, lambda i: (0, 0))
    head_spec = pl.BlockSpec((nbt, N_HEADS, tt * reps, LANE), lambda i: (i // per, 0, i % per, 0))
    head_shape = jax.ShapeDtypeStruct((nb, N_HEADS, t_len * reps, LANE), BF16)
    outs = [
        (row(SSM_WIDTH), jax.ShapeDtypeStruct((n, SSM_WIDTH), BF16)),
        (head_spec, head_shape), (head_spec, head_shape),
        (row(HEAD_DIM), jax.ShapeDtypeStruct((n, HEAD_DIM), F32)),
        (row(HEAD_DIM), jax.ShapeDtypeStruct((n, HEAD_DIM), F32)),
        (row(IDX_DIM), jax.ShapeDtypeStruct((n, IDX_DIM), F32)),
        (row(LANE), jax.ShapeDtypeStruct((n, LANE), BF16)),
        (row(LANE), jax.ShapeDtypeStruct((n, LANE), BF16)),
        (pl.BlockSpec((nbt, tt // kbv, HEAD_DIM, kbv), lambda i: (i // per, i % per, 0, 0)),
         jax.ShapeDtypeStruct((nb, t_len // kbv, HEAD_DIM, kbv), BF16)),
        (pl.BlockSpec((nbt, IDX_HEADS, tt * reps), lambda i: (i // per, 0, i % per)),
         jax.ShapeDtypeStruct((nb, IDX_HEADS, t_len * reps), F32)),
        (row(D_MODEL), jax.ShapeDtypeStruct((n, D_MODEL), BF16)),
        (row(D_MODEL), jax.ShapeDtypeStruct((n, D_MODEL), BF16)),
    ]
    return pl.pallas_call(
        functools.partial(_proj_body, nbt=nbt, reps=reps, kbv=kbv),
        grid=(n // tm,),
        in_specs=[row(D_MODEL), _mod_spec(tm, t_len), const((D_MODEL, C_END)),
                  const((1, LANE)), const((1, LANE)), const((1, LANE))],
        out_specs=[s for s, _ in outs],
        out_shape=[s for _, s in outs],
        compiler_params=pltpu.CompilerParams(dimension_semantics=("parallel",),
                                             vmem_limit_bytes=VMEM_LIMIT),
        name="proj",
    )(xf, mod3, w_pad, qg, kg, ig)


def _s5_body(u_ref, h0r_ref, h0i_ref, bre_ref, bim_ref, cre_ref, cim_ref, lre_ref, lim_ref, d_ref,
             z_ref, sr_ref, si_ref, xr, xi, hr, hi, *, nb, tc, pitch):
    @pl.when(pl.program_id(0) == 0)
    def _():
        hr[...] = h0r_ref[...]
        hi[...] = h0i_ref[...]

    ub = u_ref[...].reshape(nb * tc, SSM_WIDTH)
    brow = lambda b: slice(b * tc, (b + 1) * tc)
    prow = lambda b: slice(b * pitch, b * pitch + tc)
    in_slabs = S5_IN_BLOCK * S5_EXPAND // LANE

    for c in range(SSM_WIDTH // S5_IN_BLOCK):
        cin = slice(S5_IN_BLOCK * c, S5_IN_BLOCK * (c + 1))
        cst = slice(S5_IN_BLOCK * S5_EXPAND * c, S5_IN_BLOCK * S5_EXPAND * (c + 1))
        for dst, w_ref in ((xr, bre_ref), (xi, bim_ref)):
            x = _dot(ub[:, cin], w_ref[cin, cst])
            for k in range(in_slabs):
                for b in range(nb):
                    dst[in_slabs * c + k, prow(b), :] = x[brow(b), LANE * k:LANE * (k + 1)]

    chunk = 8
    for lc in range(SSM_LANES // LANE // chunk):
        slabs = range(chunk * lc, chunk * (lc + 1))
        lsl = lambda k: slice(LANE * k, LANE * (k + 1))

        def step(t, carry, slabs=slabs):
            rows = pl.ds(t, nb, stride=pitch)
            out_a, out_b = [], []
            for (a, b), k in zip(zip(*carry), slabs):
                lr = lre_ref[:, lsl(k)]
                li = lim_ref[:, lsl(k)]
                na = lr * a - li * b + xr[k, rows, :]
                nb_ = lr * b + li * a + xi[k, rows, :]
                xr[k, rows, :] = na
                xi[k, rows, :] = nb_
                out_a.append(na)
                out_b.append(nb_)
            return tuple(out_a), tuple(out_b)

        init = (tuple(hr[:, lsl(k)] for k in slabs), tuple(hi[:, lsl(k)] for k in slabs))
        fin_a, fin_b = lax.fori_loop(0, tc, step, init, unroll=2)
        for a, b, k in zip(fin_a, fin_b, slabs):
            hr[:, lsl(k)] = a
            hi[:, lsl(k)] = b

    out_slabs = S5_EXPAND

    def history(src, c):
        return jnp.concatenate(
            [jnp.concatenate([src[out_slabs * c + j, prow(b), :] for b in range(nb)], axis=0)
             for j in range(out_slabs)], axis=1).astype(BF16)

    for c in range(SSM_WIDTH // LANE):
        cy = slice(LANE * c, LANE * (c + 1))
        cst = slice(LANE * S5_EXPAND * c, LANE * S5_EXPAND * (c + 1))
        y = _dot(history(xr, c), cre_ref[cst, cy]) + _dot(history(xi, c), cim_ref[cst, cy])
        y = y + d_ref[:, cy] * ub[:, cy].astype(F32)
        z_ref[:, :, cy] = jax.nn.gelu(y, approximate=True).astype(BF16).reshape(nb, tc, LANE)
    sr_ref[...] = hr[...]
    si_ref[...] = hi[...]


def _s5(u, h0r, h0i, ssm, nb, t_len, tc):
    bre, bim, cre, cim, lre, lim, dsk = ssm
    pitch = tc + 8
    const = lambda s: pl.BlockSpec(s, lambda c: (0, 0))
    seq = pl.BlockSpec((nb, tc, SSM_WIDTH), lambda c: (0, c, 0))
    slab = pltpu.VMEM((SSM_LANES // LANE, nb * pitch, LANE), F32)
    return pl.pallas_call(
        functools.partial(_s5_body, nb=nb, tc=tc, pitch=pitch),
        grid=(t_len // tc,),
        in_specs=[seq, const((nb, SSM_LANES)), const((nb, SSM_LANES)),
                  const((SSM_WIDTH, SSM_LANES)), const((SSM_WIDTH, SSM_LANES)),
                  const((SSM_LANES, SSM_WIDTH)), const((SSM_LANES, SSM_WIDTH)),
                  const((nb, SSM_LANES)), const((nb, SSM_LANES)), const((1, SSM_WIDTH))],
        out_specs=[seq, const((nb, SSM_LANES)), const((nb, SSM_LANES))],
        out_shape=[jax.ShapeDtypeStruct((nb, t_len, SSM_WIDTH), BF16),
                   jax.ShapeDtypeStruct((nb, SSM_LANES), F32),
                   jax.ShapeDtypeStruct((nb, SSM_LANES), F32)],
        scratch_shapes=[slab, slab,
                        pltpu.VMEM((nb, SSM_LANES), F32), pltpu.VMEM((nb, SSM_LANES), F32)],
        compiler_params=pltpu.CompilerParams(dimension_semantics=("arbitrary",),
                                             vmem_limit_bytes=VMEM_LIMIT),
        name="s5",
    )(u, h0r, h0i, bre, bim, cre, cim, lre[:nb], lim[:nb], dsk)


def _dsa_body(q_ref, qc_ref, wt_ref, kc_ref, k_ref, vt_ref, wo_ref, o_ref,
              sc_ref, lg_ref, acc_ref, at_ref, st0_ref, st1_ref, bias_ref, hs_ref,
              *, tq, kb, s_real, s_pad, q_off, q_wrap, topk):
    j = pl.program_id(1)
    hq = N_HEADS * tq
    n_all = s_pad // kb
    cb = 2 * kb
    lane_q = lax.broadcasted_iota(I32, (1, tq), 1)
    qpos = q_off + j * tq + (lane_q & (q_wrap - 1))
    qchunk = qpos >> CHUNK_SHIFT
    last_end = (((q_off + j * tq + min(tq, q_wrap) - 1) >> CHUNK_SHIFT) + 1) << CHUNK_SHIFT
    npair = jnp.minimum(n_all // 2, lax.div(last_end + (cb - 1), cb))

    def blk(b):
        return pl.ds(pl.multiple_of(b * kb, kb), kb)

    def pair(i):
        return pl.ds(pl.multiple_of(i * cb, cb), cb)

    def admissible(b):
        kpos = b * kb + lax.broadcasted_iota(I32, (kb, tq), 0)
        return ((kpos >> CHUNK_SHIFT) <= qchunk) & (kpos < s_real), kpos

    def fold8(x, op):
        return op(x.reshape(x.shape[0] // 8, 8, x.shape[-1]), axis=0)

    hsl = lambda h: slice(h * tq, (h + 1) * tq)

    def staged(mm, consume, carry):
        st0_ref[...] = mm(0)

        def body(i, c):
            b0 = 2 * i
            st1_ref[...] = mm(b0 + 1)
            c = consume(b0, st0_ref, c)
            st0_ref[...] = mm(jnp.minimum(b0 + 2, n_all - 1))
            return consume(b0 + 1, st1_ref, c)

        return lax.fori_loop(0, npair, body, carry)

    qcs = qc_ref[0].reshape(hq, LANE)
    wt = wt_ref[0]

    rc = 64

    def sub(b, r):
        return pl.ds(pl.multiple_of(b * kb + r * rc, rc), rc)

    def admissible_rows(b, r):
        kpos = b * kb + r * rc + lax.broadcasted_iota(I32, (rc, tq), 0)
        return ((kpos >> CHUNK_SHIFT) <= qchunk) & (kpos < s_real), kpos

    def score_consume(b, st, carry):
        for r in range(kb // rc):
            sc = jnp.zeros((rc, tq), F32)
            for h in range(IDX_HEADS):
                sc = sc + wt[h:h + 1, :] * jnp.maximum(st[r * rc:(r + 1) * rc, hsl(h)], 0.0)
            sc = jnp.where(admissible_rows(b, r)[0], sc, NEG_INF)
            sc_ref[sub(b, r), :] = sc
            hs_ref[sub(b, r), :] = sc.astype(BF16)
        return carry

    staged(lambda b: _dot_nt(kc_ref[0, blk(b), :], qcs), score_consume, 0)

    def as_float(u):
        s = u ^ INT_MIN
        return lax.bitcast_convert_type(s ^ ((s >> 31) & 0x7FFFFFFF), F32)

    def count(pred):
        def body(i, acc):
            for r in range(cb // 32):
                row0 = i * cb + 32 * r
                kpos = lambda row0=row0: row0 + lax.broadcasted_iota(I32, (32, tq), 0)
                x = sc_ref[pl.ds(pl.multiple_of(row0, 32), 32), :]
                acc = acc + jnp.where(pred(x, kpos), 1.0, 0.0)
            return acc
        acc = lax.fori_loop(0, npair, body, jnp.zeros((32, tq), F32))
        return jnp.sum(acc, axis=0, keepdims=True)

    def count_rounded(cb16):
        def body(i, acc):
            for r in range(cb // 32):
                h = hs_ref[pl.ds(pl.multiple_of(i * cb + 32 * r, 32), 32), :]
                hit = jnp.where(h >= cb16, jnp.ones_like(h), jnp.zeros_like(h))
                acc = acc + hit
            return acc
        acc = lax.fori_loop(0, npair, body, jnp.zeros((32, tq), BF16))
        return jnp.sum(acc.astype(F32), axis=0, keepdims=True)

    def bf16_point(top):
        return top | jnp.where(top < 0, 0, 0xFFFF)

    def coarse_body(i, prefix):
        cand = prefix | (jnp.int32(1) << (31 - i))
        cnt = count_rounded(as_float(bf16_point(cand)).astype(BF16))
        return jnp.where(cnt >= topk, cand, prefix)

    coarse = bf16_point(lax.fori_loop(0, 16, coarse_body, jnp.zeros((1, tq), I32)))

    def fine_body(i, lohi):
        lo, hi = lohi
        mid = lo + ((hi - lo + 1) >> 1)
        cf = as_float(mid)
        ok = count(lambda x, kpos: x >= cf) >= topk
        return jnp.where(ok, mid, lo), jnp.where(ok, hi, mid - 1)

    lo, _ = lax.fori_loop(0, 17, fine_body, (coarse - (2 ** 15 + 1), coarse + (2 ** 16 + 1)))
    thr = as_float(lo)

    n_gt = count(lambda x, kpos: x > thr)
    n_eq = count(lambda x, kpos: x == thr)
    need = topk - n_gt
    any_over = jnp.max(n_eq - need) > 0.0
    idx_bits = int(s_pad).bit_length()

    def tie_search():
        def tbody(i, lim):
            cand = lim | (jnp.int32(1) << (idx_bits - 1 - i))
            n = count(lambda x, kpos: (x == thr) & (kpos() < cand))
            return jnp.where(n <= need, cand, lim)
        return lax.fori_loop(0, idx_bits, tbody, jnp.zeros((1, tq), I32))

    lim = lax.cond(any_over, tie_search, lambda: jnp.full((1, tq), 2 ** idx_bits - 1, I32))

    qs = q_ref[0].reshape(hq, LANE)

    def logits_pair(i, m):
        for half in range(2):
            b = 2 * i + half
            for r in range(kb // rc):
                x = sc_ref[sub(b, r), :]
                adm, kpos = admissible_rows(b, r)
                sel = ((x > thr) | ((x == thr) & (kpos < lim))) & adm
                bias_ref[half * kb + r * rc:half * kb + (r + 1) * rc, :] = jnp.where(sel, 0.0, NEG_INF)
        tops = [m[:, hsl(h)] for h in range(N_HEADS)]
        for half in range(2):
            b = 2 * i + half
            raw = _dot_nt(k_ref[0, blk(b), :], qs)
            for h in range(N_HEADS):
                lh = raw[:, hsl(h)] + bias_ref[half * kb:(half + 1) * kb, :]
                lg_ref[blk(b), hsl(h)] = lh
                tops[h] = jnp.maximum(tops[h], fold8(lh, jnp.max))
        return jnp.concatenate(tops, axis=1)

    m8 = lax.fori_loop(0, npair, logits_pair, jnp.full((8, hq), NEG_INF, F32))
    m = jnp.max(m8, axis=0, keepdims=True)

    acc_ref[...] = jnp.zeros_like(acc_ref)

    def pv_pair(i, ssum):
        p = jnp.exp2(lg_ref[pair(i), :] - m)
        acc_ref[...] += _dot(vt_ref[0, i], p.astype(BF16))
        return ssum + fold8(p, jnp.sum)

    s8 = lax.fori_loop(0, npair, pv_pair, jnp.zeros((8, hq), F32))
    out_t = acc_ref[...] / jnp.sum(s8, axis=0, keepdims=True)
    for h in range(N_HEADS):
        at_ref[HEAD_DIM * h:HEAD_DIM * (h + 1), :] = out_t[:, hsl(h)]
    o_ref[0] = _dot(at_ref[...].T.astype(BF16), wo_ref[...]).astype(BF16)


def _dsa(q, qc, wt, kc, kp, vt, wo, *, tq, kb, s_real, q_off, q_wrap):
    nb, _, t_len, _ = q.shape
    s_pad = kc.shape[1]
    cb = 2 * kb
    topk = min(TOPK_MAX, s_real // 4)
    hq = N_HEADS * tq
    qspec = pl.BlockSpec((1, N_HEADS, tq, LANE), lambda b, j: (b, 0, j, 0))
    return pl.pallas_call(
        functools.partial(_dsa_body, tq=tq, kb=kb, s_real=s_real, s_pad=s_pad, q_off=q_off,
                          q_wrap=q_wrap, topk=float(topk)),
        grid=(nb, t_len // tq),
        in_specs=[qspec, qspec,
                  pl.BlockSpec((1, IDX_HEADS, tq), lambda b, j: (b, 0, j)),
                  pl.BlockSpec((1, s_pad, LANE), lambda b, j: (b, 0, 0)),
                  pl.BlockSpec((1, s_pad, LANE), lambda b, j: (b, 0, 0)),
                  pl.BlockSpec((1, s_pad // cb, HEAD_DIM, cb), lambda b, j: (b, 0, 0, 0)),
                  pl.BlockSpec((N_HEADS * HEAD_DIM, D_MODEL), lambda b, j: (0, 0))],
        out_specs=pl.BlockSpec((1, tq, D_MODEL), lambda b, j: (b, j, 0)),
        out_shape=jax.ShapeDtypeStruct((nb, t_len, D_MODEL), BF16),
        scratch_shapes=[pltpu.VMEM((s_pad, tq), F32), pltpu.VMEM((s_pad, hq), F32),
                        pltpu.VMEM((HEAD_DIM, hq), F32), pltpu.VMEM((N_HEADS * HEAD_DIM, tq), F32),
                        pltpu.VMEM((kb, hq), F32), pltpu.VMEM((kb, hq), F32),
                        pltpu.VMEM((cb, tq), F32), pltpu.VMEM((s_pad, tq), BF16)],
        compiler_params=pltpu.CompilerParams(dimension_semantics=("parallel", "parallel"),
                                             vmem_limit_bytes=VMEM_LIMIT),
        name="dsa",
    )(q, qc, wt, kc, kp, vt, wo)


def _merge_body(x_ref, z_ref, ga_ref, gb_ref, ob_ref, mod_ref, wglu_ref, wout_ref, o_ref):
    glu = _dot(z_ref[...], wglu_ref[...])
    out_a = glu[:, :D_MODEL] * jax.nn.sigmoid(glu[:, D_MODEL:])
    merged = ga_ref[...] * out_a + gb_ref[...] * ob_ref[...]
    upd = _dot(merged.astype(BF16), wout_ref[...])
    o_ref[...] = x_ref[...] + _gate_mul(upd, mod_ref[...], 2)


def _merge(xf, z, ga, gb, ob, mod3, wglu, wout, t_len, tm):
    n = xf.shape[0]
    row = lambda w: pl.BlockSpec((tm, w), lambda i: (i, 0))
    const = lambda s: pl.BlockSpec(s, lambda i: (0, 0))
    return pl.pallas_call(
        _merge_body,
        grid=(n // tm,),
        in_specs=[row(D_MODEL), row(SSM_WIDTH), row(D_MODEL), row(D_MODEL), row(D_MODEL),
                  _mod_spec(tm, t_len), const((SSM_WIDTH, 2 * D_MODEL)), const((D_MODEL, D_MODEL))],
        out_specs=row(D_MODEL),
        out_shape=jax.ShapeDtypeStruct((n, D_MODEL), F32),
        compiler_params=pltpu.CompilerParams(dimension_semantics=("parallel",),
                                             vmem_limit_bytes=VMEM_LIMIT),
        name="merge",
    )(xf, z, ga, gb, ob, mod3, wglu, wout)


def _moe_body(x_ref, mod_ref, wr_ref, br_ref, wg_ref, wu_ref, wd_ref, o_ref, hb_ref, comb_ref, acc_ref):
    e = pl.program_id(1)
    tm = x_ref.shape[0]
    lane = lax.broadcasted_iota(I32, (tm, LANE), 1).astype(F32)

    @pl.when(e == 0)
    def _():
        h2 = _modulate(_rms(x_ref[...], D_MODEL), mod_ref[...], 4, 3)
        hb_ref[...] = h2.astype(BF16)
        lg = _dot_hilo(h2, wr_ref[...]) + br_ref[...]
        neg = -jnp.inf
        gl = jnp.where(lane < N_GROUPS, lg, neg)
        gmax = jnp.max(gl, axis=-1, keepdims=True)
        g_idx = jnp.min(jnp.where(gl == gmax, lane, float(LANE)), axis=-1, keepdims=True)
        g_w = 1.0 / jnp.sum(jnp.where(lane < N_GROUPS, jnp.exp(lg - gmax), 0.0),
                            axis=-1, keepdims=True)
        lo = N_GROUPS + EXPERTS_PER_GROUP * g_idx
        el = jnp.where((lane >= lo) & (lane < lo + EXPERTS_PER_GROUP), lg, neg)
        v1 = jnp.max(el, axis=-1, keepdims=True)
        i1 = jnp.min(jnp.where(el == v1, lane, float(LANE)), axis=-1, keepdims=True)
        el2 = jnp.where(lane == i1, neg, el)
        v2 = jnp.max(el2, axis=-1, keepdims=True)
        i2 = jnp.min(jnp.where(el2 == v2, lane, float(LANE)), axis=-1, keepdims=True)
        e21 = jnp.exp(v2 - v1)
        w1 = g_w / (1.0 + e21)
        w2 = g_w * e21 / (1.0 + e21)
        comb_ref[...] = jnp.where(lane == i1, w1, 0.0) + jnp.where(lane == i2, w2, 0.0)
        acc_ref[...] = jnp.zeros_like(acc_ref)

    hb = hb_ref[...]
    comb = comb_ref[...]
    parts = []
    for k in range(EXPERTS_PER_GROUP):
        gate = _dot(hb, wg_ref[k])
        hid = gate * jax.nn.sigmoid(gate) * _dot(hb, wu_ref[k])
        lane_e = (N_GROUPS + EXPERTS_PER_GROUP * e + k).astype(F32)
        col = jnp.sum(jnp.where(lane == lane_e, comb, 0.0), axis=-1, keepdims=True)
        parts.append((hid * col).astype(BF16))
    acc_ref[...] += _dot(jnp.concatenate(parts, axis=1), wd_ref[0])

    @pl.when(e == N_GROUPS - 1)
    def _():
        o_ref[...] = x_ref[...] + _gate_mul(acc_ref[...], mod_ref[...], 5)


def _moe(x1, mod3, wr, br, wg, wu, wd, t_len, tm):
    n = x1.shape[0]
    gate_up = pl.BlockSpec((EXPERTS_PER_GROUP, D_MODEL, EXPERT_DIM), lambda i, e: (e, 0, 0))
    return pl.pallas_call(
        _moe_body,
        grid=(n // tm, N_GROUPS),
        in_specs=[pl.BlockSpec((tm, D_MODEL), lambda i, e: (i, 0)),
                  _mod_spec(tm, t_len),
                  pl.BlockSpec((D_MODEL, LANE), lambda i, e: (0, 0)),
                  pl.BlockSpec((1, LANE), lambda i, e: (0, 0)),
                  gate_up, gate_up,
                  pl.BlockSpec((1, EXPERTS_PER_GROUP * EXPERT_DIM, D_MODEL),
                               lambda i, e: (e, 0, 0))],
        out_specs=pl.BlockSpec((tm, D_MODEL), lambda i, e: (i, 0)),
        out_shape=jax.ShapeDtypeStruct((n, D_MODEL), F32),
        scratch_shapes=[pltpu.VMEM((tm, D_MODEL), BF16), pltpu.VMEM((tm, LANE), F32),
                        pltpu.VMEM((tm, D_MODEL), F32)],
        compiler_params=pltpu.CompilerParams(dimension_semantics=("parallel", "arbitrary"),
                                             vmem_limit_bytes=VMEM_LIMIT),
        name="moe",
    )(x1, mod3, wr, br, wg, wu, wd)


def _prep_params(w_in, ssm_a_re, ssm_a_im, ssm_log_dt, ssm_b_re, ssm_b_im, ssm_c_re, ssm_c_im, ssm_d,
                 w_glu, q_gain, k_gain, kidx_gain, w_attn_out, w_out, w_rg, b_rg, w_re, b_re,
                 w_gate, w_up, w_down, max_batch):
    offs = np.cumsum([0, SSM_WIDTH, N_HEADS * HEAD_DIM, HEAD_DIM, HEAD_DIM, IDX_HEADS * IDX_DIM,
                      IDX_DIM, IDX_HEADS, D_MODEL, D_MODEL])
    seg = lambda i: w_in[:, offs[i]:offs[i + 1]]
    padto = lambda a, w: jnp.pad(a, ((0, 0), (0, w - a.shape[-1])))
    wq = jnp.pad(seg(1).reshape(D_MODEL, N_HEADS, HEAD_DIM), ((0, 0), (0, 0), (0, LANE - HEAD_DIM)))
    wqi = jnp.tile(seg(4).reshape(D_MODEL, IDX_HEADS, IDX_DIM), (1, 1, LANE // IDX_DIM))
    w_pad = jnp.concatenate(
        [seg(0), wq.reshape(D_MODEL, N_HEADS * LANE), padto(seg(2), LANE), padto(seg(3), LANE),
         wqi.reshape(D_MODEL, IDX_HEADS * LANE), jnp.tile(seg(5), (1, LANE // IDX_DIM)),
         padto(seg(6), LANE), seg(7), seg(8)], axis=1).astype(BF16)
    qg = padto((q_gain * (HEAD_DIM ** -0.5 * np.log2(np.e)))[None, :], LANE)
    kg = padto(k_gain[None, :], LANE)
    ig = jnp.tile(kidx_gain[None, :], (1, LANE // IDX_DIM))

    dt = jnp.exp(ssm_log_dt)[:, None]
    decay = jnp.exp(ssm_a_re * dt)
    lam_re = decay * jnp.cos(ssm_a_im * dt)
    lam_im = decay * jnp.sin(ssm_a_im * dt)
    den = ssm_a_re * ssm_a_re + ssm_a_im * ssm_a_im
    num_re = lam_re - 1.0
    coef_re = (num_re * ssm_a_re + lam_im * ssm_a_im) / den
    coef_im = (lam_im * ssm_a_re - num_re * ssm_a_im) / den
    bb_re = coef_re[..., None] * ssm_b_re - coef_im[..., None] * ssm_b_im
    bb_im = coef_re[..., None] * ssm_b_im + coef_im[..., None] * ssm_b_re
    eye = jnp.eye(SSM_GROUPS, dtype=F32)

    def in_map(bb):
        return jnp.einsum('gpj,gh->gjhp', bb, eye).reshape(SSM_WIDTH, SSM_LANES).astype(BF16)

    def out_map(c):
        return jnp.einsum('gjp,gh->gphj', c, eye).reshape(SSM_LANES, SSM_WIDTH).astype(BF16)

    bcast = lambda a: jnp.broadcast_to(a.reshape(1, SSM_LANES), (max_batch, SSM_LANES))
    ssm = (in_map(bb_re), in_map(bb_im), out_map(ssm_c_re), out_map(-ssm_c_im),
           bcast(lam_re), bcast(lam_im), ssm_d[None, :])

    wr = jnp.pad(jnp.concatenate([w_rg, w_re], axis=1), ((0, 0), (0, LANE - N_GROUPS - N_EXPERTS)))
    br = jnp.pad(jnp.concatenate([b_rg, b_re]), (0, LANE - N_GROUPS - N_EXPERTS))[None, :]
    wd = w_down.astype(BF16).reshape(N_GROUPS, EXPERTS_PER_GROUP * EXPERT_DIM, D_MODEL)
    return dict(w_pad=w_pad, qg=qg, kg=kg, ig=ig, ssm=ssm,
                wglu=w_glu.astype(BF16), wo=w_attn_out.astype(BF16), wout=w_out.astype(BF16),
                wr=wr, br=br, wg=w_gate.astype(BF16), wu=w_up.astype(BF16), wd=wd)


def _layer(x, mod3, p, past, *, tm_tok, tm_moe, tc, tq, kb):
    nb, t_len, _ = x.shape
    n = nb * t_len
    cb = 2 * kb
    reps = max(tq // t_len, 1)
    kbv = min(cb, t_len)
    xf = x.reshape(n, D_MODEL)
    u, q, qc, k, v, ki, kp, kc, vt, wt, ga, gb = _proj(
        xf, mod3, p["w_pad"], p["qg"], p["kg"], p["ig"], nb, t_len, tm_tok, reps, kbv)
    k3 = k.reshape(nb, t_len, HEAD_DIM)
    v3 = v.reshape(nb, t_len, HEAD_DIM)
    ki3 = ki.reshape(nb, t_len, IDX_DIM)
    kp = kp.reshape(nb, t_len, LANE)
    kc = kc.reshape(nb, t_len, LANE)

    if past is None:
        h0r = jnp.zeros((nb, SSM_LANES), F32)
        h0i = h0r
        s_real, q_off = t_len, 0
    else:
        past_k, past_v, past_ki, h0_re, h0_im = past
        h0r = h0_re.reshape(nb, SSM_LANES)
        h0i = h0_im.reshape(nb, SSM_LANES)
        q_off = past_k.shape[1]
        s_real = q_off + t_len
        s_pad = -(-s_real // cb) * cb
        fill = lambda a: jnp.pad(a, ((0, 0), (0, s_pad - s_real), (0, 0)))
        pk = jnp.pad(past_k.astype(BF16), ((0, 0), (0, 0), (0, LANE - HEAD_DIM)))
        kp = fill(jnp.concatenate([pk, kp], axis=1))
        pi_hi, pi_lo = _split_bf16(past_ki)
        kc = fill(jnp.concatenate([jnp.concatenate([pi_hi, pi_lo, pi_hi, pi_lo], axis=-1), kc], axis=1))
        vt_all = jnp.concatenate([past_v.astype(BF16).transpose(0, 2, 1),
                                  vt.transpose(0, 2, 1, 3).reshape(nb, HEAD_DIM, t_len)], axis=2)
        vt_all = jnp.pad(vt_all, ((0, 0), (0, 0), (0, s_pad - s_real)))
        vt = vt_all.reshape(nb, HEAD_DIM, s_pad // cb, cb).transpose(0, 2, 1, 3)

    z, s_re, s_im = _s5(u.reshape(nb, t_len, SSM_WIDTH), h0r, h0i, p["ssm"], nb, t_len, tc)
    ob = _dsa(q, qc, wt, kc, kp, vt, p["wo"], tq=tq, kb=kb, s_real=s_real, q_off=q_off,
              q_wrap=t_len if reps > 1 else tq)
    ob = ob[:, :t_len].reshape(n, D_MODEL)
    x1 = _merge(xf, z.reshape(n, SSM_WIDTH), ga, gb, ob, mod3, p["wglu"], p["wout"],
                t_len, tm_tok)
    x2 = _moe(x1, mod3, p["wr"], p["br"], p["wg"], p["wu"], p["wd"], t_len, tm_moe)
    return (x2.reshape(nb, t_len, D_MODEL), k3, v3, ki3,
            s_re.reshape(nb, SSM_GROUPS, SSM_STATE), s_im.reshape(nb, SSM_GROUPS, SSM_STATE))


def kernel(x_prompt, x_sample, cache_k, cache_v, cache_kidx, state_ssm_re, state_ssm_im, c_prompt, c_sample, w_ada, b_ada, w_in, ssm_a_re, ssm_a_im, ssm_log_dt, ssm_b_re, ssm_b_im, ssm_c_re, ssm_c_im, ssm_d, w_glu, q_gain, k_gain, kidx_gain, w_attn_out, w_out, w_route_group, b_route_group, w_route_expert, b_route_expert, w_gate, w_up, w_down):
    depth = w_ada.shape[0]
    nbp = x_prompt.shape[0]
    nbs = x_sample.shape[0]
    xp, xs = x_prompt, x_sample
    outs = [[] for _ in range(10)]
    for l in range(depth):
        p = _prep_params(w_in[l], ssm_a_re[l], ssm_a_im[l], ssm_log_dt[l], ssm_b_re[l], ssm_b_im[l],
                         ssm_c_re[l], ssm_c_im[l], ssm_d[l], w_glu[l], q_gain[l], k_gain[l],
                         kidx_gain[l], w_attn_out[l], w_out[l], w_route_group[l], b_route_group[l],
                         w_route_expert[l], b_route_expert[l], w_gate[l], w_up[l], w_down[l],
                         max(nbp, nbs))
        mod = _adaln(jnp.concatenate([c_prompt, c_sample], axis=0), w_ada[l], b_ada[l][None, :])
        mod3 = mod[:, None, :]
        xp, kp, vp, kip, srp, sip = _layer(xp, mod3[:nbp], p, None,
                                           tm_tok=512, tm_moe=1024, tc=32, tq=256, kb=256)
        past = (cache_k[l], cache_v[l], cache_kidx[l], state_ssm_re[l], state_ssm_im[l])
        xs, ks, vs, kis, srs, sis = _layer(xs, mod3[nbp:], p, past,
                                           tm_tok=512, tm_moe=512, tc=64, tq=128, kb=384)
        for lst, val in zip(outs, (kp, vp, kip, srp, sip, ks, vs, kis, srs, sis)):
            lst.append(val)
    return (xp, xs) + tuple(jnp.stack(o) for o in outs)
```

```python
import functools

import jax
import jax.numpy as jnp
import numpy as np
from jax import lax
from jax.experimental import pallas as pl
from jax.experimental.pallas import tpu as pltpu

F32 = jnp.float32
BF16 = jnp.bfloat16
I32 = jnp.int32

D_MODEL = 1024
CHUNK_SHIFT = 6
EPS = 1e-6
NEG_INF = -1e30
SSM_WIDTH = 512
SSM_GROUP = 16
SSM_GROUPS = 32
SSM_STATE = 64
SSM_LANES = SSM_GROUPS * SSM_STATE
S5_EXPAND = SSM_STATE // SSM_GROUP
S5_IN_BLOCK = 256
N_HEADS = 8
HEAD_DIM = 64
IDX_HEADS = 8
IDX_DIM = 32
IDX_WEIGHT_SCALE = (IDX_HEADS * IDX_DIM) ** -0.5
TOPK_MAX = 256
N_GROUPS = 4
EXPERTS_PER_GROUP = 4
N_EXPERTS = 16
EXPERT_DIM = 256
LANE = 128
INT_MIN = -(2 ** 31)
VMEM_LIMIT = 56 * 1024 * 1024

C_U = 0
C_Q = C_U + SSM_WIDTH
C_K = C_Q + N_HEADS * HEAD_DIM
C_V = C_K + LANE
C_QI = C_V + LANE
C_KI = C_QI + IDX_HEADS * LANE
C_WI = C_KI + LANE
C_GA = C_WI + LANE
C_GB = C_GA + D_MODEL
C_END = C_GB + D_MODEL


def _dot(a, b):
    return jnp.dot(a, b, preferred_element_type=F32)


def _dot_nt(a, b):
    return lax.dot_general(a, b, (((1,), (1,)), ((), ())), preferred_element_type=F32)


def _split_bf16(x):
    hi = x.astype(BF16)
    lo = (x - hi.astype(F32)).astype(BF16)
    return hi, lo


def _dot_hilo(a, b):
    ah, al = _split_bf16(a)
    bh, bl = _split_bf16(b)
    return _dot(ah, bh) + _dot(ah, bl) + _dot(al, bh)


def _tiling(tm, t_len):
    return (t_len // tm, 1) if tm < t_len else (1, tm // t_len)


def _mod_spec(tm, t_len):
    per, nbt = _tiling(tm, t_len)
    return pl.BlockSpec((nbt, 1, 6 * D_MODEL), lambda i, *_: (i // per, 0, 0))


def _modulate(x, mod, scale_seg, shift_seg):
    nb = mod.shape[0]
    tm = x.shape[0]
    sc = mod[:, :, scale_seg * D_MODEL:(scale_seg + 1) * D_MODEL]
    sh = mod[:, :, shift_seg * D_MODEL:(shift_seg + 1) * D_MODEL]
    x3 = x.reshape(nb, tm // nb, D_MODEL)
    return (x3 * (1.0 + sc) + sh).reshape(tm, D_MODEL)


def _gate_mul(x, mod, seg):
    nb = mod.shape[0]
    tm = x.shape[0]
    g = mod[:, :, seg * D_MODEL:(seg + 1) * D_MODEL]
    return (x.reshape(nb, tm // nb, D_MODEL) * g).reshape(tm, D_MODEL)


def _rms(x, n):
    return x * lax.rsqrt(jnp.sum(x * x, axis=-1, keepdims=True) * (1.0 / n) + EPS)


def _adaln_body(c_ref, w_ref, b_ref, o_ref):
    c = c_ref[...]
    s = c * jax.nn.sigmoid(c)
    o_ref[...] = _dot_hilo(s, w_ref[...]) + b_ref[...]


def _adaln(c, w, b):
    nb = c.shape[0]
    n = w.shape[1]
    bn = 512
    return pl.pallas_call(
        _adaln_body,
        grid=(n // bn,),
        in_specs=[pl.BlockSpec((nb, D_MODEL), lambda j: (0, 0)),
                  pl.BlockSpec((D_MODEL, bn), lambda j: (0, j)),
                  pl.BlockSpec((1, bn), lambda j: (0, j))],
        out_specs=pl.BlockSpec((nb, bn), lambda j: (0, j)),
        out_shape=jax.ShapeDtypeStruct((nb, n), F32),
        name="adaln",
    )(c, w, b)


def _proj_body(x_ref, mod_ref, w_ref, qg_ref, kg_ref, ig_ref,
               u_ref, q_ref, qc_ref, k_ref, v_ref, ki_ref, kp_ref, kc_ref, vt_ref, wt_ref,
               ga_ref, gb_ref, *, nbt, reps, kbv):
    tm = x_ref.shape[0]
    tt = tm // nbt
    rows = lambda b: slice(tt * b, tt * (b + 1))
    hsl = lambda h: slice(LANE * h, LANE * (h + 1))
    lane = lax.broadcasted_iota(I32, (tm, LANE), 1)
    hb = _modulate(_rms(x_ref[...], D_MODEL), mod_ref[...], 1, 0).astype(BF16)

    u_ref[...] = _dot(hb, w_ref[:, C_U:C_Q]).astype(BF16)

    def put_heads(ref, pieces):
        for b in range(nbt):
            for h in range(N_HEADS):
                for r in range(reps):
                    ref[b, h, tt * r:tt * (r + 1), :] = pieces[h][rows(b), :]

    q = _dot(hb, w_ref[:, C_Q:C_K])
    zpad = jnp.zeros((tm, LANE - HEAD_DIM), BF16)
    heads = []
    for h in range(N_HEADS):
        qh = _rms(q[:, HEAD_DIM * h:HEAD_DIM * (h + 1)], HEAD_DIM) * qg_ref[...]
        heads.append(jnp.concatenate([qh.astype(BF16), zpad], axis=-1))
    put_heads(q_ref, heads)

    qi_hi, qi_lo = _split_bf16(_dot(hb, w_ref[:, C_QI:C_KI]))
    put_heads(qc_ref, [jnp.where(lane < 2 * IDX_DIM, qi_hi[:, hsl(h)], qi_lo[:, hsl(h)])
                       for h in range(IDX_HEADS)])

    kn = _rms(_dot(hb, w_ref[:, C_K:C_V]), HEAD_DIM) * kg_ref[...]
    k_ref[...] = kn[:, :HEAD_DIM]
    kp_ref[...] = kn.astype(BF16)

    vfull = _dot(hb, w_ref[:, C_V:C_QI])
    v_ref[...] = vfull[:, :HEAD_DIM]
    vt = vfull.T[:HEAD_DIM].astype(BF16)
    for b in range(nbt):
        for c in range(tt // kbv):
            vt_ref[b, c] = vt[:, tt * b + kbv * c:tt * b + kbv * (c + 1)]

    kin = _rms(_dot(hb, w_ref[:, C_KI:C_WI]), LANE) * ig_ref[...]
    ki_ref[...] = kin[:, :IDX_DIM]
    ki_hi, ki_lo = _split_bf16(kin)
    kc_ref[...] = jnp.where(((lane >> 5) & 1) == 1, ki_lo, ki_hi)

    wt = (_dot(hb, w_ref[:, C_WI:C_GA]) * IDX_WEIGHT_SCALE).T[:IDX_HEADS]
    for b in range(nbt):
        for r in range(reps):
            wt_ref[b, :, tt * r:tt * (r + 1)] = wt[:, rows(b)]

    ga_ref[...] = jax.nn.sigmoid(_dot(hb, w_ref[:, C_GA:C_GB])).astype(BF16)
    gb_ref[...] = jax.nn.sigmoid(_dot(hb, w_ref[:, C_GB:C_END])).astype(BF16)


def _proj(xf, mod3, w_pad, qg, kg, ig, nb, t_len, tm, reps, kbv):
    n = xf.shape[0]
    per, nbt = _tiling(tm, t_len)
    tt = tm // nbt
    row = lambda w: pl.BlockSpec((tm, w), lambda i: (i, 0))
    const = lambda s: pl.BlockSpec(s, lambda i: (0, 0))
    head_spec = pl.BlockSpec((nbt, N_HEADS, tt * reps, LANE), lambda i: (i // per, 0, i % per, 0))
    head_shape = jax.ShapeDtypeStruct((nb, N_HEADS, t_len * reps, LANE), BF16)
    outs = [
        (row(SSM_WIDTH), jax.ShapeDtypeStruct((n, SSM_WIDTH), BF16)),
        (head_spec, head_shape), (head_spec, head_shape),
        (row(HEAD_DIM), jax.ShapeDtypeStruct((n, HEAD_DIM), F32)),
        (row(HEAD_DIM), jax.ShapeDtypeStruct((n, HEAD_DIM), F32)),
        (row(IDX_DIM), jax.ShapeDtypeStruct((n, IDX_DIM), F32)),
        (row(LANE), jax.ShapeDtypeStruct((n, LANE), BF16)),
        (row(LANE), jax.ShapeDtypeStruct((n, LANE), BF16)),
        (pl.BlockSpec((nbt, tt // kbv, HEAD_DIM, kbv), lambda i: (i // per, i % per, 0, 0)),
         jax.ShapeDtypeStruct((nb, t_len // kbv, HEAD_DIM, kbv), BF16)),
        (pl.BlockSpec((nbt, IDX_HEADS, tt * reps), lambda i: (i // per, 0, i % per)),
         jax.ShapeDtypeStruct((nb, IDX_HEADS, t_len * reps), F32)),
        (row(D_MODEL), jax.ShapeDtypeStruct((n, D_MODEL), BF16)),
        (row(D_MODEL), jax.ShapeDtypeStruct((n, D_MODEL), BF16)),
    ]
    return pl.pallas_call(
        functools.partial(_proj_body, nbt=nbt, reps=reps, kbv=kbv),
        grid=(n // tm,),
        in_specs=[row(D_MODEL), _mod_spec(tm, t_len), const((D_MODEL, C_END)),
                  const((1, HEAD_DIM)), const((1, LANE)), const((1, LANE))],
        out_specs=[s for s, _ in outs],
        out_shape=[s for _, s in outs],
        compiler_params=pltpu.CompilerParams(dimension_semantics=("parallel",),
                                             vmem_limit_bytes=VMEM_LIMIT),
        name="proj",
    )(xf, mod3, w_pad, qg, kg, ig)


def _s5_body(u_ref, h0r_ref, h0i_ref, bre_ref, bim_ref, cre_ref, cim_ref, lre_ref, lim_ref, d_ref,
             z_ref, sr_ref, si_ref, xr, xi, hr, hi, *, nb, tc, pitch):
    @pl.when(pl.program_id(0) == 0)
    def _():
        hr[...] = h0r_ref[...]
        hi[...] = h0i_ref[...]

    ub = u_ref[...].reshape(nb * tc, SSM_WIDTH)
    brow = lambda b: slice(b * tc, (b + 1) * tc)
    prow = lambda b: slice(b * pitch, b * pitch + tc)
    in_slabs = S5_IN_BLOCK * S5_EXPAND // LANE

    for c in range(SSM_WIDTH // S5_IN_BLOCK):
        cin = slice(S5_IN_BLOCK * c, S5_IN_BLOCK * (c + 1))
        cst = slice(S5_IN_BLOCK * S5_EXPAND * c, S5_IN_BLOCK * S5_EXPAND * (c + 1))
        for dst, w_ref in ((xr, bre_ref), (xi, bim_ref)):
            x = _dot(ub[:, cin], w_ref[cin, cst])
            for k in range(in_slabs):
                for b in range(nb):
                    dst[in_slabs * c + k, prow(b), :] = x[brow(b), LANE * k:LANE * (k + 1)]

    chunk = 8
    for lc in range(SSM_LANES // LANE // chunk):
        slabs = range(chunk * lc, chunk * (lc + 1))
        lsl = lambda k: slice(LANE * k, LANE * (k + 1))

        def step(t, carry, slabs=slabs):
            rows = pl.ds(t, nb, stride=pitch)
            out_a, out_b = [], []
            for (a, b), k in zip(zip(*carry), slabs):
                lr = lre_ref[:, lsl(k)]
                li = lim_ref[:, lsl(k)]
                na = lr * a - li * b + xr[k, rows, :]
                nb_ = lr * b + li * a + xi[k, rows, :]
                xr[k, rows, :] = na
                xi[k, rows, :] = nb_
                out_a.append(na)
                out_b.append(nb_)
            return tuple(out_a), tuple(out_b)

        init = (tuple(hr[:, lsl(k)] for k in slabs), tuple(hi[:, lsl(k)] for k in slabs))
        fin_a, fin_b = lax.fori_loop(0, tc, step, init, unroll=2)
        for a, b, k in zip(fin_a, fin_b, slabs):
            hr[:, lsl(k)] = a
            hi[:, lsl(k)] = b

    out_slabs = S5_EXPAND

    def history(src, c):
        return jnp.concatenate(
            [jnp.concatenate([src[out_slabs * c + j, prow(b), :] for b in range(nb)], axis=0)
             for j in range(out_slabs)], axis=1).astype(BF16)

    for c in range(SSM_WIDTH // LANE):
        cy = slice(LANE * c, LANE * (c + 1))
        cst = slice(LANE * S5_EXPAND * c, LANE * S5_EXPAND * (c + 1))
        y = _dot(history(xr, c), cre_ref[cst, cy]) + _dot(history(xi, c), cim_ref[cst, cy])
        y = y + d_ref[:, cy] * ub[:, cy].astype(F32)
        z_ref[:, :, cy] = jax.nn.gelu(y, approximate=True).astype(BF16).reshape(nb, tc, LANE)
    sr_ref[...] = hr[...]
    si_ref[...] = hi[...]


def _s5(u, h0r, h0i, ssm, nb, t_len, tc):
    bre, bim, cre, cim, lre, lim, dsk = ssm
    pitch = tc + 8
    const = lambda s: pl.BlockSpec(s, lambda c: (0, 0))
    seq = pl.BlockSpec((nb, tc, SSM_WIDTH), lambda c: (0, c, 0))
    slab = pltpu.VMEM((SSM_LANES // LANE, nb * pitch, LANE), F32)
    return pl.pallas_call(
        functools.partial(_s5_body, nb=nb, tc=tc, pitch=pitch),
        grid=(t_len // tc,),
        in_specs=[seq, const((nb, SSM_LANES)), const((nb, SSM_LANES)),
                  const((SSM_WIDTH, SSM_LANES)), const((SSM_WIDTH, SSM_LANES)),
                  const((SSM_LANES, SSM_WIDTH)), const((SSM_LANES, SSM_WIDTH)),
                  const((nb, SSM_LANES)), const((nb, SSM_LANES)), const((1, SSM_WIDTH))],
        out_specs=[seq, const((nb, SSM_LANES)), const((nb, SSM_LANES))],
        out_shape=[jax.ShapeDtypeStruct((nb, t_len, SSM_WIDTH), BF16),
                   jax.ShapeDtypeStruct((nb, SSM_LANES), F32),
                   jax.ShapeDtypeStruct((nb, SSM_LANES), F32)],
        scratch_shapes=[slab, slab,
                        pltpu.VMEM((nb, SSM_LANES), F32), pltpu.VMEM((nb, SSM_LANES), F32)],
        compiler_params=pltpu.CompilerParams(dimension_semantics=("arbitrary",),
                                             vmem_limit_bytes=VMEM_LIMIT),
        name="s5",
    )(u, h0r, h0i, bre, bim, cre, cim, lre[:nb], lim[:nb], dsk)


def _dsa_body(q_ref, qc_ref, wt_ref, kc_ref, k_ref, vt_ref, wo_ref, o_ref,
              sc_ref, lg_ref, acc_ref, at_ref, st0_ref, st1_ref, bias_ref, hs_ref,
              *, tq, kb, s_real, s_pad, q_off, q_wrap, topk):
    j = pl.program_id(1)
    hq = N_HEADS * tq
    n_all = s_pad // kb
    cb = 2 * kb
    lane_q = lax.broadcasted_iota(I32, (1, tq), 1)
    qpos = q_off + j * tq + (lane_q & (q_wrap - 1))
    qchunk = qpos >> CHUNK_SHIFT
    last_end = (((q_off + j * tq + min(tq, q_wrap) - 1) >> CHUNK_SHIFT) + 1) << CHUNK_SHIFT
    npair = jnp.minimum(n_all // 2, lax.div(last_end + (cb - 1), cb))

    def blk(b):
        return pl.ds(pl.multiple_of(b * kb, kb), kb)

    def pair(i):
        return pl.ds(pl.multiple_of(i * cb, cb), cb)

    def fold8(x, op):
        return op(x.reshape(x.shape[0] // 8, 8, x.shape[-1]), axis=0)

    hsl = lambda h: slice(h * tq, (h + 1) * tq)

    def staged(mm, consume, carry):
        st0_ref[...] = mm(0)

        def body(i, c):
            b0 = 2 * i
            st1_ref[...] = mm(b0 + 1)
            c = consume(b0, st0_ref, c)
            st0_ref[...] = mm(jnp.minimum(b0 + 2, n_all - 1))
            return consume(b0 + 1, st1_ref, c)

        return lax.fori_loop(0, npair, body, carry)

    qcs = qc_ref[0].reshape(hq, LANE)
    wt = wt_ref[0]

    rc = 64

    def sub(b, r):
        return pl.ds(pl.multiple_of(b * kb + r * rc, rc), rc)

    def admissible_rows(b, r):
        kpos = b * kb + r * rc + lax.broadcasted_iota(I32, (rc, tq), 0)
        return ((kpos >> CHUNK_SHIFT) <= qchunk) & (kpos < s_real), kpos

    def score_consume(b, st, carry):
        for r in range(kb // rc):
            sc = jnp.zeros((rc, tq), F32)
            for h in range(IDX_HEADS):
                sc = sc + wt[h:h + 1, :] * jnp.maximum(st[r * rc:(r + 1) * rc, hsl(h)], 0.0)
            sc = jnp.where(admissible_rows(b, r)[0], sc, NEG_INF)
            sc_ref[sub(b, r), :] = sc
            hs_ref[sub(b, r), :] = sc.astype(BF16)
        return carry

    staged(lambda b: _dot_nt(kc_ref[0, blk(b), :], qcs), score_consume, 0)

    def as_float(u):
        s = u ^ INT_MIN
        return lax.bitcast_convert_type(s ^ ((s >> 31) & 0x7FFFFFFF), F32)

    def count(pred):
        def body(i, acc):
            for r in range(cb // 32):
                row0 = i * cb + 32 * r
                kpos = lambda row0=row0: row0 + lax.broadcasted_iota(I32, (32, tq), 0)
                x = sc_ref[pl.ds(pl.multiple_of(row0, 32), 32), :]
                acc = acc + jnp.where(pred(x, kpos), 1.0, 0.0)
            return acc
        acc = lax.fori_loop(0, npair, body, jnp.zeros((32, tq), F32))
        return jnp.sum(acc, axis=0, keepdims=True)

    def count_rounded(cb16):
        def body(i, acc):
            for r in range(cb // 32):
                h = hs_ref[pl.ds(pl.multiple_of(i * cb + 32 * r, 32), 32), :]
                hit = jnp.where(h >= cb16, jnp.ones_like(h), jnp.zeros_like(h))
                acc = acc + hit
            return acc
        acc = lax.fori_loop(0, npair, body, jnp.zeros((32, tq), BF16))
        return jnp.sum(acc.astype(F32), axis=0, keepdims=True)

    def bf16_point(top):
        return top | jnp.where(top < 0, 0, 0xFFFF)

    def coarse_body(i, prefix):
        cand = prefix | (jnp.int32(1) << (31 - i))
        cnt = count_rounded(as_float(bf16_point(cand)).astype(BF16))
        return jnp.where(cnt >= topk, cand, prefix)

    coarse = bf16_point(lax.fori_loop(0, 16, coarse_body, jnp.zeros((1, tq), I32)))

    def fine_body(i, lohi):
        lo, hi = lohi
        mid = lo + ((hi - lo + 1) >> 1)
        cf = as_float(mid)
        ok = count(lambda x, kpos: x >= cf) >= topk
        return jnp.where(ok, mid, lo), jnp.where(ok, hi, mid - 1)

    lo, _ = lax.fori_loop(0, 17, fine_body, (coarse - (2 ** 15 + 1), coarse + (2 ** 16 + 1)))
    thr = as_float(lo)

    n_gt = count(lambda x, kpos: x > thr)
    n_eq = count(lambda x, kpos: x == thr)
    need = topk - n_gt
    any_over = jnp.max(n_eq - need) > 0.0
    idx_bits = int(s_pad).bit_length()

    def tie_search():
        def tbody(i, lim):
            cand = lim | (jnp.int32(1) << (idx_bits - 1 - i))
            n = count(lambda x, kpos: (x == thr) & (kpos() < cand))
            return jnp.where(n <= need, cand, lim)
        return lax.fori_loop(0, idx_bits, tbody, jnp.zeros((1, tq), I32))

    lim = lax.cond(any_over, tie_search, lambda: jnp.full((1, tq), 2 ** idx_bits - 1, I32))

    qs = q_ref[0].reshape(hq, LANE)

    def logits_pair(i, m):
        for half in range(2):
            b = 2 * i + half
            for r in range(kb // rc):
                x = sc_ref[sub(b, r), :]
                adm, kpos = admissible_rows(b, r)
                sel = ((x > thr) | ((x == thr) & (kpos < lim))) & adm
                bias_ref[half * kb + r * rc:half * kb + (r + 1) * rc, :] = jnp.where(sel, 0.0, NEG_INF)
        tops = [m[:, hsl(h)] for h in range(N_HEADS)]
        for half in range(2):
            b = 2 * i + half
            raw = _dot_nt(k_ref[0, blk(b), :], qs)
            for h in range(N_HEADS):
                lh = raw[:, hsl(h)] + bias_ref[half * kb:(half + 1) * kb, :]
                lg_ref[blk(b), hsl(h)] = lh
                tops[h] = jnp.maximum(tops[h], fold8(lh, jnp.max))
        return jnp.concatenate(tops, axis=1)

    m8 = lax.fori_loop(0, npair, logits_pair, jnp.full((8, hq), NEG_INF, F32))
    m = jnp.max(m8, axis=0, keepdims=True)

    acc_ref[...] = jnp.zeros_like(acc_ref)

    def pv_pair(i, ssum):
        p = jnp.exp2(lg_ref[pair(i), :] - m)
        acc_ref[...] += _dot(vt_ref[0, i], p.astype(BF16))
        return ssum + fold8(p, jnp.sum)

    s8 = lax.fori_loop(0, npair, pv_pair, jnp.zeros((8, hq), F32))
    out_t = acc_ref[...] / jnp.sum(s8, axis=0, keepdims=True)
    for h in range(N_HEADS):
        at_ref[HEAD_DIM * h:HEAD_DIM * (h + 1), :] = out_t[:, hsl(h)]
    o_ref[0] = _dot(at_ref[...].T.astype(BF16), wo_ref[...]).astype(BF16)


def _dsa(q, qc, wt, kc, kp, vt, wo, *, tq, kb, s_real, q_off, q_wrap):
    nb, _, t_len, _ = q.shape
    s_pad = kc.shape[1]
    cb = 2 * kb
    topk = min(TOPK_MAX, s_real // 4)
    hq = N_HEADS * tq
    qspec = pl.BlockSpec((1, N_HEADS, tq, LANE), lambda b, j: (b, 0, j, 0))
    return pl.pallas_call(
        functools.partial(_dsa_body, tq=tq, kb=kb, s_real=s_real, s_pad=s_pad, q_off=q_off,
                          q_wrap=q_wrap, topk=float(topk)),
        grid=(nb, t_len // tq),
        in_specs=[qspec, qspec,
                  pl.BlockSpec((1, IDX_HEADS, tq), lambda b, j: (b, 0, j)),
                  pl.BlockSpec((1, s_pad, LANE), lambda b, j: (b, 0, 0)),
                  pl.BlockSpec((1, s_pad, LANE), lambda b, j: (b, 0, 0)),
                  pl.BlockSpec((1, s_pad // cb, HEAD_DIM, cb), lambda b, j: (b, 0, 0, 0)),
                  pl.BlockSpec((N_HEADS * HEAD_DIM, D_MODEL), lambda b, j: (0, 0))],
        out_specs=pl.BlockSpec((1, tq, D_MODEL), lambda b, j: (b, j, 0)),
        out_shape=jax.ShapeDtypeStruct((nb, t_len, D_MODEL), BF16),
        scratch_shapes=[pltpu.VMEM((s_pad, tq), F32), pltpu.VMEM((s_pad, hq), F32),
                        pltpu.VMEM((HEAD_DIM, hq), F32), pltpu.VMEM((N_HEADS * HEAD_DIM, tq), F32),
                        pltpu.VMEM((kb, hq), F32), pltpu.VMEM((kb, hq), F32),
                        pltpu.VMEM((cb, tq), F32), pltpu.VMEM((s_pad, tq), BF16)],
        compiler_params=pltpu.CompilerParams(dimension_semantics=("parallel", "parallel"),
                                             vmem_limit_bytes=VMEM_LIMIT),
        name="dsa",
    )(q, qc, wt, kc, kp, vt, wo)


def _merge_body(x_ref, z_ref, ga_ref, gb_ref, ob_ref, mod_ref, wglu_ref, wout_ref, o_ref):
    glu = _dot(z_ref[...], wglu_ref[...])
    out_a = glu[:, :D_MODEL] * jax.nn.sigmoid(glu[:, D_MODEL:])
    merged = ga_ref[...] * out_a + gb_ref[...] * ob_ref[...]
    upd = _dot(merged.astype(BF16), wout_ref[...])
    o_ref[...] = x_ref[...] + _gate_mul(upd, mod_ref[...], 2)


def _merge(xf, z, ga, gb, ob, mod3, wglu, wout, t_len, tm):
    n = xf.shape[0]
    row = lambda w: pl.BlockSpec((tm, w), lambda i: (i, 0))
    const = lambda s: pl.BlockSpec(s, lambda i: (0, 0))
    return pl.pallas_call(
        _merge_body,
        grid=(n // tm,),
        in_specs=[row(D_MODEL), row(SSM_WIDTH), row(D_MODEL), row(D_MODEL), row(D_MODEL),
                  _mod_spec(tm, t_len), const((SSM_WIDTH, 2 * D_MODEL)), const((D_MODEL, D_MODEL))],
        out_specs=row(D_MODEL),
        out_shape=jax.ShapeDtypeStruct((n, D_MODEL), F32),
        compiler_params=pltpu.CompilerParams(dimension_semantics=("parallel",),
                                             vmem_limit_bytes=VMEM_LIMIT),
        name="merge",
    )(xf, z, ga, gb, ob, mod3, wglu, wout)


def _moe_body(x_ref, mod_ref, wr_ref, br_ref, wgu_ref, wd_ref, o_ref, hb_ref, comb_ref, acc_ref):
    e = pl.program_id(1)
    tm = x_ref.shape[0]
    lane = lax.broadcasted_iota(I32, (tm, LANE), 1).astype(F32)

    @pl.when(e == 0)
    def _():
        h2 = _modulate(_rms(x_ref[...], D_MODEL), mod_ref[...], 4, 3)
        hb_ref[...] = h2.astype(BF16)
        lg = _dot_hilo(h2, wr_ref[...]) + br_ref[...]
        neg = -jnp.inf
        gl = jnp.where(lane < N_GROUPS, lg, neg)
        gmax = jnp.max(gl, axis=-1, keepdims=True)
        g_idx = jnp.min(jnp.where(gl == gmax, lane, float(LANE)), axis=-1, keepdims=True)
        g_w = 1.0 / jnp.sum(jnp.where(lane < N_GROUPS, jnp.exp(lg - gmax), 0.0),
                            axis=-1, keepdims=True)
        lo = N_GROUPS + EXPERTS_PER_GROUP * g_idx
        el = jnp.where((lane >= lo) & (lane < lo + EXPERTS_PER_GROUP), lg, neg)
        v1 = jnp.max(el, axis=-1, keepdims=True)
        i1 = jnp.min(jnp.where(el == v1, lane, float(LANE)), axis=-1, keepdims=True)
        el2 = jnp.where(lane == i1, neg, el)
        v2 = jnp.max(el2, axis=-1, keepdims=True)
        i2 = jnp.min(jnp.where(el2 == v2, lane, float(LANE)), axis=-1, keepdims=True)
        e21 = jnp.exp(v2 - v1)
        w1 = g_w / (1.0 + e21)
        w2 = g_w * e21 / (1.0 + e21)
        comb_ref[...] = jnp.where(lane == i1, w1, 0.0) + jnp.where(lane == i2, w2, 0.0)
        acc_ref[...] = jnp.zeros_like(acc_ref)

    hb = hb_ref[...]
    comb = comb_ref[...]
    parts = []
    for k in range(EXPERTS_PER_GROUP):
        gu = _dot(hb, wgu_ref[k])
        gate = gu[:, :EXPERT_DIM]
        hid = gate * jax.nn.sigmoid(gate) * gu[:, EXPERT_DIM:]
        lane_e = (N_GROUPS + EXPERTS_PER_GROUP * e + k).astype(F32)
        col = jnp.sum(jnp.where(lane == lane_e, comb, 0.0), axis=-1, keepdims=True)
        parts.append((hid * col).astype(BF16))
    acc_ref[...] += _dot(jnp.concatenate(parts, axis=1), wd_ref[0])

    @pl.when(e == N_GROUPS - 1)
    def _():
        o_ref[...] = x_ref[...] + _gate_mul(acc_ref[...], mod_ref[...], 5)


def _moe(x1, mod3, wr, br, wgu, wd, t_len, tm):
    n = x1.shape[0]
    gate_up = pl.BlockSpec((EXPERTS_PER_GROUP, D_MODEL, 2 * EXPERT_DIM), lambda i, e: (e, 0, 0))
    return pl.pallas_call(
        _moe_body,
        grid=(n // tm, N_GROUPS),
        in_specs=[pl.BlockSpec((tm, D_MODEL), lambda i, e: (i, 0)),
                  _mod_spec(tm, t_len),
                  pl.BlockSpec((D_MODEL, LANE), lambda i, e: (0, 0)),
                  pl.BlockSpec((1, LANE), lambda i, e: (0, 0)),
                  gate_up,
                  pl.BlockSpec((1, EXPERTS_PER_GROUP * EXPERT_DIM, D_MODEL),
                               lambda i, e: (e, 0, 0))],
        out_specs=pl.BlockSpec((tm, D_MODEL), lambda i, e: (i, 0)),
        out_shape=jax.ShapeDtypeStruct((n, D_MODEL), F32),
        scratch_shapes=[pltpu.VMEM((tm, D_MODEL), BF16), pltpu.VMEM((tm, LANE), F32),
                        pltpu.VMEM((tm, D_MODEL), F32)],
        compiler_params=pltpu.CompilerParams(dimension_semantics=("parallel", "arbitrary"),
                                             vmem_limit_bytes=VMEM_LIMIT),
        name="moe",
    )(x1, mod3, wr, br, wgu, wd)


def _prep_params(w_in, ssm_a_re, ssm_a_im, ssm_log_dt, ssm_b_re, ssm_b_im, ssm_c_re, ssm_c_im, ssm_d,
                 w_glu, q_gain, k_gain, kidx_gain, w_attn_out, w_out, w_rg, b_rg, w_re, b_re,
                 w_gate, w_up, w_down, max_batch):
    offs = np.cumsum([0, SSM_WIDTH, N_HEADS * HEAD_DIM, HEAD_DIM, HEAD_DIM, IDX_HEADS * IDX_DIM,
                      IDX_DIM, IDX_HEADS, D_MODEL, D_MODEL])
    seg = lambda i: w_in[:, offs[i]:offs[i + 1]]
    padto = lambda a, w: jnp.pad(a, ((0, 0), (0, w - a.shape[-1])))
    wqi = jnp.tile(seg(4).reshape(D_MODEL, IDX_HEADS, IDX_DIM), (1, 1, LANE // IDX_DIM))
    w_pad = jnp.concatenate(
        [seg(0), seg(1), padto(seg(2), LANE), padto(seg(3), LANE),
         wqi.reshape(D_MODEL, IDX_HEADS * LANE), jnp.tile(seg(5), (1, LANE // IDX_DIM)),
         padto(seg(6), LANE), seg(7), seg(8)], axis=1).astype(BF16)
    qg = (q_gain * (HEAD_DIM ** -0.5 * np.log2(np.e)))[None, :]
    kg = padto(k_gain[None, :], LANE)
    ig = jnp.tile(kidx_gain[None, :], (1, LANE // IDX_DIM))

    dt = jnp.exp(ssm_log_dt)[:, None]
    decay = jnp.exp(ssm_a_re * dt)
    lam_re = decay * jnp.cos(ssm_a_im * dt)
    lam_im = decay * jnp.sin(ssm_a_im * dt)
    den = ssm_a_re * ssm_a_re + ssm_a_im * ssm_a_im
    num_re = lam_re - 1.0
    coef_re = (num_re * ssm_a_re + lam_im * ssm_a_im) / den
    coef_im = (lam_im * ssm_a_re - num_re * ssm_a_im) / den
    bb_re = coef_re[..., None] * ssm_b_re - coef_im[..., None] * ssm_b_im
    bb_im = coef_re[..., None] * ssm_b_im + coef_im[..., None] * ssm_b_re
    eye = jnp.eye(SSM_GROUPS, dtype=F32)

    def in_map(bb):
        return jnp.einsum('gpj,gh->gjhp', bb, eye).reshape(SSM_WIDTH, SSM_LANES).astype(BF16)

    def out_map(c):
        return jnp.einsum('gjp,gh->gphj', c, eye).reshape(SSM_LANES, SSM_WIDTH).astype(BF16)

    bcast = lambda a: jnp.broadcast_to(a.reshape(1, SSM_LANES), (max_batch, SSM_LANES))
    ssm = (in_map(bb_re), in_map(bb_im), out_map(ssm_c_re), out_map(-ssm_c_im),
           bcast(lam_re), bcast(lam_im), ssm_d[None, :])

    wr = jnp.pad(jnp.concatenate([w_rg, w_re], axis=1), ((0, 0), (0, LANE - N_GROUPS - N_EXPERTS)))
    br = jnp.pad(jnp.concatenate([b_rg, b_re]), (0, LANE - N_GROUPS - N_EXPERTS))[None, :]
    wd = w_down.astype(BF16).reshape(N_GROUPS, EXPERTS_PER_GROUP * EXPERT_DIM, D_MODEL)
    return dict(w_pad=w_pad, qg=qg, kg=kg, ig=ig, ssm=ssm,
                wglu=w_glu.astype(BF16), wo=w_attn_out.astype(BF16), wout=w_out.astype(BF16),
                wr=wr, br=br, wgu=jnp.concatenate([w_gate, w_up], axis=-1).astype(BF16), wd=wd)


def _layer(x, mod3, p, past, *, tm_tok, tm_moe, tc, tq, kb):
    nb, t_len, _ = x.shape
    n = nb * t_len
    cb = 2 * kb
    reps = max(tq // t_len, 1)
    kbv = min(cb, t_len)
    xf = x.reshape(n, D_MODEL)
    u, q, qc, k, v, ki, kp, kc, vt, wt, ga, gb = _proj(
        xf, mod3, p["w_pad"], p["qg"], p["kg"], p["ig"], nb, t_len, tm_tok, reps, kbv)
    k3 = k.reshape(nb, t_len, HEAD_DIM)
    v3 = v.reshape(nb, t_len, HEAD_DIM)
    ki3 = ki.reshape(nb, t_len, IDX_DIM)
    kp = kp.reshape(nb, t_len, LANE)
    kc = kc.reshape(nb, t_len, LANE)

    if past is None:
        h0r = jnp.zeros((nb, SSM_LANES), F32)
        h0i = h0r
        s_real, q_off = t_len, 0
    else:
        past_k, past_v, past_ki, h0_re, h0_im = past
        h0r = h0_re.reshape(nb, SSM_LANES)
        h0i = h0_im.reshape(nb, SSM_LANES)
        q_off = past_k.shape[1]
        s_real = q_off + t_len
        s_pad = -(-s_real // cb) * cb
        fill = lambda a: jnp.pad(a, ((0, 0), (0, s_pad - s_real), (0, 0)))
        pk = jnp.pad(past_k.astype(BF16), ((0, 0), (0, 0), (0, LANE - HEAD_DIM)))
        kp = fill(jnp.concatenate([pk, kp], axis=1))
        pi_hi, pi_lo = _split_bf16(past_ki)
        kc = fill(jnp.concatenate([jnp.concatenate([pi_hi, pi_lo, pi_hi, pi_lo], axis=-1), kc], axis=1))
        vt_all = jnp.concatenate([past_v.astype(BF16).transpose(0, 2, 1),
                                  vt.transpose(0, 2, 1, 3).reshape(nb, HEAD_DIM, t_len)], axis=2)
        vt_all = jnp.pad(vt_all, ((0, 0), (0, 0), (0, s_pad - s_real)))
        vt = vt_all.reshape(nb, HEAD_DIM, s_pad // cb, cb).transpose(0, 2, 1, 3)

    z, s_re, s_im = _s5(u.reshape(nb, t_len, SSM_WIDTH), h0r, h0i, p["ssm"], nb, t_len, tc)
    ob = _dsa(q, qc, wt, kc, kp, vt, p["wo"], tq=tq, kb=kb, s_real=s_real, q_off=q_off,
              q_wrap=t_len if reps > 1 else tq)
    ob = ob[:, :t_len].reshape(n, D_MODEL)
    x1 = _merge(xf, z.reshape(n, SSM_WIDTH), ga, gb, ob, mod3, p["wglu"], p["wout"],
                t_len, tm_tok)
    x2 = _moe(x1, mod3, p["wr"], p["br"], p["wgu"], p["wd"], t_len, tm_moe)
    return (x2.reshape(nb, t_len, D_MODEL), k3, v3, ki3,
            s_re.reshape(nb, SSM_GROUPS, SSM_STATE), s_im.reshape(nb, SSM_GROUPS, SSM_STATE))


def kernel(x_prompt, x_sample, cache_k, cache_v, cache_kidx, state_ssm_re, state_ssm_im, c_prompt, c_sample, w_ada, b_ada, w_in, ssm_a_re, ssm_a_im, ssm_log_dt, ssm_b_re, ssm_b_im, ssm_c_re, ssm_c_im, ssm_d, w_glu, q_gain, k_gain, kidx_gain, w_attn_out, w_out, w_route_group, b_route_group, w_route_expert, b_route_expert, w_gate, w_up, w_down):
    depth = w_ada.shape[0]
    nbp = x_prompt.shape[0]
    nbs = x_sample.shape[0]
    xp, xs = x_prompt, x_sample
    outs = [[] for _ in range(10)]
    for l in range(depth):
        p = _prep_params(w_in[l], ssm_a_re[l], ssm_a_im[l], ssm_log_dt[l], ssm_b_re[l], ssm_b_im[l],
                         ssm_c_re[l], ssm_c_im[l], ssm_d[l], w_glu[l], q_gain[l], k_gain[l],
                         kidx_gain[l], w_attn_out[l], w_out[l], w_route_group[l], b_route_group[l],
                         w_route_expert[l], b_route_expert[l], w_gate[l], w_up[l], w_down[l],
                         max(nbp, nbs))
        mod = _adaln(jnp.concatenate([c_prompt, c_sample], axis=0), w_ada[l], b_ada[l][None, :])
        mod3 = mod[:, None, :]
        xp, kp, vp, kip, srp, sip = _layer(xp, mod3[:nbp], p, None,
                                           tm_tok=512, tm_moe=1024, tc=64, tq=256, kb=256)
        past = (cache_k[l], cache_v[l], cache_kidx[l], state_ssm_re[l], state_ssm_im[l])
        xs, ks, vs, kis, srs, sis = _layer(xs, mod3[nbp:], p, past,
                                           tm_tok=512, tm_moe=512, tc=64, tq=128, kb=384)
        for lst, val in zip(outs, (kp, vp, kip, srp, sip, ks, vs, kis, srs, sis)):
            lst.append(val)
    return (xp, xs) + tuple(jnp.stack(o) for o in outs)
```

```python
import functools

import jax
import jax.numpy as jnp
import numpy as np
from jax import lax
from jax.experimental import pallas as pl
from jax.experimental.pallas import tpu as pltpu

F32 = jnp.float32
BF16 = jnp.bfloat16
I32 = jnp.int32

D_MODEL = 1024
CHUNK_SHIFT = 6
EPS = 1e-6
NEG_INF = -1e30
SSM_WIDTH = 512
SSM_GROUP = 16
SSM_GROUPS = 32
SSM_STATE = 64
SSM_LANES = SSM_GROUPS * SSM_STATE
S5_EXPAND = SSM_STATE // SSM_GROUP
S5_IN_BLOCK = 256
N_HEADS = 8
HEAD_DIM = 64
IDX_HEADS = 8
IDX_DIM = 32
IDX_WEIGHT_SCALE = (IDX_HEADS * IDX_DIM) ** -0.5
TOPK_MAX = 256
N_GROUPS = 4
EXPERTS_PER_GROUP = 4
N_EXPERTS = 16
EXPERT_DIM = 256
LANE = 128
INT_MIN = -(2 ** 31)
VMEM_LIMIT = 58 * 1024 * 1024

C_U = 0
C_Q = C_U + SSM_WIDTH
C_K = C_Q + N_HEADS * HEAD_DIM
C_V = C_K + LANE
C_QI = C_V + LANE
C_KI = C_QI + IDX_HEADS * LANE
C_WI = C_KI + LANE
C_GA = C_WI + LANE
C_GB = C_GA + D_MODEL
C_END = C_GB + D_MODEL


def _dot(a, b):
    return jnp.dot(a, b, preferred_element_type=F32)


def _dot_nt(a, b):
    return lax.dot_general(a, b, (((1,), (1,)), ((), ())), preferred_element_type=F32)


def _split_bf16(x):
    hi = x.astype(BF16)
    lo = (x - hi.astype(F32)).astype(BF16)
    return hi, lo


def _dot_hilo(a, b):
    ah, al = _split_bf16(a)
    bh, bl = _split_bf16(b)
    return _dot(ah, bh) + _dot(ah, bl) + _dot(al, bh)


def _tiling(tm, t_len):
    return (t_len // tm, 1) if tm < t_len else (1, tm // t_len)


def _mod_spec(tm, t_len):
    per, nbt = _tiling(tm, t_len)
    return pl.BlockSpec((nbt, 1, 6 * D_MODEL), lambda i, *_: (i // per, 0, 0))


def _modulate(x, mod, scale_seg, shift_seg):
    nb = mod.shape[0]
    tm = x.shape[0]
    sc = mod[:, :, scale_seg * D_MODEL:(scale_seg + 1) * D_MODEL]
    sh = mod[:, :, shift_seg * D_MODEL:(shift_seg + 1) * D_MODEL]
    x3 = x.reshape(nb, tm // nb, D_MODEL)
    return (x3 * (1.0 + sc) + sh).reshape(tm, D_MODEL)


def _gate_mul(x, mod, seg):
    nb = mod.shape[0]
    tm = x.shape[0]
    g = mod[:, :, seg * D_MODEL:(seg + 1) * D_MODEL]
    return (x.reshape(nb, tm // nb, D_MODEL) * g).reshape(tm, D_MODEL)


def _rms(x, n):
    return x * lax.rsqrt(jnp.sum(x * x, axis=-1, keepdims=True) * (1.0 / n) + EPS)


def _adaln_body(c_ref, w_ref, b_ref, o_ref):
    c = c_ref[...]
    s = c * jax.nn.sigmoid(c)
    o_ref[...] = _dot_hilo(s, w_ref[...]) + b_ref[...]


def _adaln(c, w, b):
    nb = c.shape[0]
    n = w.shape[1]
    bn = 512
    return pl.pallas_call(
        _adaln_body,
        grid=(n // bn,),
        in_specs=[pl.BlockSpec((nb, D_MODEL), lambda j: (0, 0)),
                  pl.BlockSpec((D_MODEL, bn), lambda j: (0, j)),
                  pl.BlockSpec((1, bn), lambda j: (0, j))],
        out_specs=pl.BlockSpec((nb, bn), lambda j: (0, j)),
        out_shape=jax.ShapeDtypeStruct((nb, n), F32),
        name="adaln",
    )(c, w, b)


def _proj_body(x_ref, mod_ref, w_ref, qg_ref, kg_ref, ig_ref,
               u_ref, q_ref, qc_ref, k_ref, v_ref, ki_ref, kp_ref, kc_ref, vt_ref, wt_ref,
               ga_ref, gb_ref, *, nbt, reps, kbv):
    tm = x_ref.shape[0]
    tt = tm // nbt
    rows = lambda b: slice(tt * b, tt * (b + 1))
    hsl = lambda h: slice(LANE * h, LANE * (h + 1))
    lane = lax.broadcasted_iota(I32, (tm, LANE), 1)
    hb = _modulate(_rms(x_ref[...], D_MODEL), mod_ref[...], 1, 0).astype(BF16)

    u_ref[...] = _dot(hb, w_ref[:, C_U:C_Q]).astype(BF16)

    def put_heads(ref, pieces):
        for b in range(nbt):
            for h in range(N_HEADS):
                for r in range(reps):
                    ref[b, h, tt * r:tt * (r + 1), :] = pieces[h][rows(b), :]

    q = _dot(hb, w_ref[:, C_Q:C_K])
    zpad = jnp.zeros((tm, LANE - HEAD_DIM), BF16)
    heads = []
    for h in range(N_HEADS):
        qh = _rms(q[:, HEAD_DIM * h:HEAD_DIM * (h + 1)], HEAD_DIM) * qg_ref[...]
        heads.append(jnp.concatenate([qh.astype(BF16), zpad], axis=-1))
    put_heads(q_ref, heads)

    qi_hi, qi_lo = _split_bf16(_dot(hb, w_ref[:, C_QI:C_KI]))
    put_heads(qc_ref, [jnp.where(lane < 2 * IDX_DIM, qi_hi[:, hsl(h)], qi_lo[:, hsl(h)])
                       for h in range(IDX_HEADS)])

    kn = _rms(_dot(hb, w_ref[:, C_K:C_V]), HEAD_DIM) * kg_ref[...]
    k_ref[...] = kn[:, :HEAD_DIM]
    kp_ref[...] = kn.astype(BF16)

    vfull = _dot(hb, w_ref[:, C_V:C_QI])
    v_ref[...] = vfull[:, :HEAD_DIM]
    vt = vfull.T[:HEAD_DIM].astype(BF16)
    for b in range(nbt):
        for c in range(tt // kbv):
            vt_ref[b, c] = vt[:, tt * b + kbv * c:tt * b + kbv * (c + 1)]

    kin = _rms(_dot(hb, w_ref[:, C_KI:C_WI]), LANE) * ig_ref[...]
    ki_ref[...] = kin[:, :IDX_DIM]
    ki_hi, ki_lo = _split_bf16(kin)
    kc_ref[...] = jnp.where(((lane >> 5) & 1) == 1, ki_lo, ki_hi)

    wt = (_dot(hb, w_ref[:, C_WI:C_GA]) * IDX_WEIGHT_SCALE).T[:IDX_HEADS]
    for b in range(nbt):
        for r in range(reps):
            wt_ref[b, :, tt * r:tt * (r + 1)] = wt[:, rows(b)]

    ga_ref[...] = jax.nn.sigmoid(_dot(hb, w_ref[:, C_GA:C_GB])).astype(BF16)
    gb_ref[...] = jax.nn.sigmoid(_dot(hb, w_ref[:, C_GB:C_END])).astype(BF16)


def _proj(xf, mod3, w_pad, qg, kg, ig, nb, t_len, tm, reps, kbv):
    n = xf.shape[0]
    per, nbt = _tiling(tm, t_len)
    tt = tm // nbt
    row = lambda w: pl.BlockSpec((tm, w), lambda i: (i, 0))
    const = lambda s: pl.BlockSpec(s, lambda i: (0, 0))
    head_spec = pl.BlockSpec((nbt, N_HEADS, tt * reps, LANE), lambda i: (i // per, 0, i % per, 0))
    head_shape = jax.ShapeDtypeStruct((nb, N_HEADS, t_len * reps, LANE), BF16)
    outs = [
        (row(SSM_WIDTH), jax.ShapeDtypeStruct((n, SSM_WIDTH), BF16)),
        (head_spec, head_shape), (head_spec, head_shape),
        (row(HEAD_DIM), jax.ShapeDtypeStruct((n, HEAD_DIM), F32)),
        (row(HEAD_DIM), jax.ShapeDtypeStruct((n, HEAD_DIM), F32)),
        (row(IDX_DIM), jax.ShapeDtypeStruct((n, IDX_DIM), F32)),
        (row(LANE), jax.ShapeDtypeStruct((n, LANE), BF16)),
        (row(LANE), jax.ShapeDtypeStruct((n, LANE), BF16)),
        (pl.BlockSpec((nbt, tt // kbv, HEAD_DIM, kbv), lambda i: (i // per, i % per, 0, 0)),
         jax.ShapeDtypeStruct((nb, t_len // kbv, HEAD_DIM, kbv), BF16)),
        (pl.BlockSpec((nbt, IDX_HEADS, tt * reps), lambda i: (i // per, 0, i % per)),
         jax.ShapeDtypeStruct((nb, IDX_HEADS, t_len * reps), F32)),
        (row(D_MODEL), jax.ShapeDtypeStruct((n, D_MODEL), BF16)),
        (row(D_MODEL), jax.ShapeDtypeStruct((n, D_MODEL), BF16)),
    ]
    return pl.pallas_call(
        functools.partial(_proj_body, nbt=nbt, reps=reps, kbv=kbv),
        grid=(n // tm,),
        in_specs=[row(D_MODEL), _mod_spec(tm, t_len), const((D_MODEL, C_END)),
                  const((1, HEAD_DIM)), const((1, LANE)), const((1, LANE))],
        out_specs=[s for s, _ in outs],
        out_shape=[s for _, s in outs],
        compiler_params=pltpu.CompilerParams(dimension_semantics=("parallel",),
                                             vmem_limit_bytes=VMEM_LIMIT),
        name="proj",
    )(xf, mod3, w_pad, qg, kg, ig)


def _s5_body(u_ref, h0r_ref, h0i_ref, bre_ref, bim_ref, cre_ref, cim_ref, lre_ref, lim_ref, d_ref,
             z_ref, sr_ref, si_ref, xr, xi, hr, hi, *, nb, tc, pitch):
    @pl.when(pl.program_id(0) == 0)
    def _():
        hr[...] = h0r_ref[...]
        hi[...] = h0i_ref[...]

    ub = u_ref[...].reshape(nb * tc, SSM_WIDTH)
    brow = lambda b: slice(b * tc, (b + 1) * tc)
    prow = lambda b: slice(b * pitch, b * pitch + tc)
    in_slabs = S5_IN_BLOCK * S5_EXPAND // LANE

    for c in range(SSM_WIDTH // S5_IN_BLOCK):
        cin = slice(S5_IN_BLOCK * c, S5_IN_BLOCK * (c + 1))
        cst = slice(S5_IN_BLOCK * S5_EXPAND * c, S5_IN_BLOCK * S5_EXPAND * (c + 1))
        for dst, w_ref in ((xr, bre_ref), (xi, bim_ref)):
            x = _dot(ub[:, cin], w_ref[cin, cst])
            for k in range(in_slabs):
                for b in range(nb):
                    dst[in_slabs * c + k, prow(b), :] = x[brow(b), LANE * k:LANE * (k + 1)]

    chunk = 8
    for lc in range(SSM_LANES // LANE // chunk):
        slabs = range(chunk * lc, chunk * (lc + 1))
        lsl = lambda k: slice(LANE * k, LANE * (k + 1))

        def step(t, carry, slabs=slabs):
            rows = pl.ds(t, nb, stride=pitch)
            out_a, out_b = [], []
            for (a, b), k in zip(zip(*carry), slabs):
                lr = lre_ref[:, lsl(k)]
                li = lim_ref[:, lsl(k)]
                na = lr * a - li * b + xr[k, rows, :]
                nb_ = lr * b + li * a + xi[k, rows, :]
                xr[k, rows, :] = na
                xi[k, rows, :] = nb_
                out_a.append(na)
                out_b.append(nb_)
            return tuple(out_a), tuple(out_b)

        init = (tuple(hr[:, lsl(k)] for k in slabs), tuple(hi[:, lsl(k)] for k in slabs))
        fin_a, fin_b = lax.fori_loop(0, tc, step, init, unroll=2)
        for a, b, k in zip(fin_a, fin_b, slabs):
            hr[:, lsl(k)] = a
            hi[:, lsl(k)] = b

    out_slabs = S5_EXPAND

    def history(src, c):
        return jnp.concatenate(
            [jnp.concatenate([src[out_slabs * c + j, prow(b), :] for b in range(nb)], axis=0)
             for j in range(out_slabs)], axis=1).astype(BF16)

    for c in range(SSM_WIDTH // LANE):
        cy = slice(LANE * c, LANE * (c + 1))
        cst = slice(LANE * S5_EXPAND * c, LANE * S5_EXPAND * (c + 1))
        y = _dot(history(xr, c), cre_ref[cst, cy]) + _dot(history(xi, c), cim_ref[cst, cy])
        y = y + d_ref[:, cy] * ub[:, cy].astype(F32)
        z_ref[:, :, cy] = jax.nn.gelu(y, approximate=True).astype(BF16).reshape(nb, tc, LANE)
    sr_ref[...] = hr[...]
    si_ref[...] = hi[...]


def _s5(u, h0r, h0i, ssm, nb, t_len, tc):
    bre, bim, cre, cim, lre, lim, dsk = ssm
    pitch = tc + 8
    const = lambda s: pl.BlockSpec(s, lambda c: (0, 0))
    seq = pl.BlockSpec((nb, tc, SSM_WIDTH), lambda c: (0, c, 0))
    slab = pltpu.VMEM((SSM_LANES // LANE, nb * pitch, LANE), F32)
    return pl.pallas_call(
        functools.partial(_s5_body, nb=nb, tc=tc, pitch=pitch),
        grid=(t_len // tc,),
        in_specs=[seq, const((nb, SSM_LANES)), const((nb, SSM_LANES)),
                  const((SSM_WIDTH, SSM_LANES)), const((SSM_WIDTH, SSM_LANES)),
                  const((SSM_LANES, SSM_WIDTH)), const((SSM_LANES, SSM_WIDTH)),
                  const((nb, SSM_LANES)), const((nb, SSM_LANES)), const((1, SSM_WIDTH))],
        out_specs=[seq, const((nb, SSM_LANES)), const((nb, SSM_LANES))],
        out_shape=[jax.ShapeDtypeStruct((nb, t_len, SSM_WIDTH), BF16),
                   jax.ShapeDtypeStruct((nb, SSM_LANES), F32),
                   jax.ShapeDtypeStruct((nb, SSM_LANES), F32)],
        scratch_shapes=[slab, slab,
                        pltpu.VMEM((nb, SSM_LANES), F32), pltpu.VMEM((nb, SSM_LANES), F32)],
        compiler_params=pltpu.CompilerParams(dimension_semantics=("arbitrary",),
                                             vmem_limit_bytes=VMEM_LIMIT),
        name="s5",
    )(u, h0r, h0i, bre, bim, cre, cim, lre[:nb], lim[:nb], dsk)


def _dsa_body(q_ref, qc_ref, wt_ref, kc_ref, k_ref, vt_ref, wo_ref, o_ref,
              sc_ref, lg_ref, acc_ref, at_ref, st0_ref, st1_ref, bias_ref, hs_ref,
              *, tq, kb, s_real, s_pad, q_off, q_wrap, topk):
    j = pl.program_id(1)
    hq = N_HEADS * tq
    n_all = s_pad // kb
    cb = 2 * kb
    lane_q = lax.broadcasted_iota(I32, (1, tq), 1)
    qpos = q_off + j * tq + (lane_q & (q_wrap - 1))
    qchunk = qpos >> CHUNK_SHIFT
    last_end = (((q_off + j * tq + min(tq, q_wrap) - 1) >> CHUNK_SHIFT) + 1) << CHUNK_SHIFT
    npair = jnp.minimum(n_all // 2, lax.div(last_end + (cb - 1), cb))

    def blk(b):
        return pl.ds(pl.multiple_of(b * kb, kb), kb)

    def pair(i):
        return pl.ds(pl.multiple_of(i * cb, cb), cb)

    def fold8(x, op):
        return op(x.reshape(x.shape[0] // 8, 8, x.shape[-1]), axis=0)

    hsl = lambda h: slice(h * tq, (h + 1) * tq)

    def staged(mm, consume, carry):
        st0_ref[...] = mm(0)

        def body(i, c):
            b0 = 2 * i
            st1_ref[...] = mm(b0 + 1)
            c = consume(b0, st0_ref, c)
            st0_ref[...] = mm(jnp.minimum(b0 + 2, n_all - 1))
            return consume(b0 + 1, st1_ref, c)

        return lax.fori_loop(0, npair, body, carry)

    qcs = qc_ref[0].reshape(hq, LANE)
    wt = wt_ref[0]

    rc = 64

    def sub(b, r):
        return pl.ds(pl.multiple_of(b * kb + r * rc, rc), rc)

    def admissible_rows(b, r):
        kpos = b * kb + r * rc + lax.broadcasted_iota(I32, (rc, tq), 0)
        return ((kpos >> CHUNK_SHIFT) <= qchunk) & (kpos < s_real), kpos

    def score_consume(b, st, carry):
        for r in range(kb // rc):
            sc = jnp.zeros((rc, tq), F32)
            for h in range(IDX_HEADS):
                sc = sc + wt[h:h + 1, :] * jnp.maximum(st[r * rc:(r + 1) * rc, hsl(h)], 0.0)
            sc = jnp.where(admissible_rows(b, r)[0], sc, NEG_INF)
            sc_ref[sub(b, r), :] = sc
            hs_ref[sub(b, r), :] = sc.astype(BF16)
        return carry

    staged(lambda b: _dot_nt(kc_ref[0, blk(b), :], qcs), score_consume, 0)

    def as_float(u):
        s = u ^ INT_MIN
        return lax.bitcast_convert_type(s ^ ((s >> 31) & 0x7FFFFFFF), F32)

    def count(pred):
        def body(i, acc):
            for r in range(cb // 32):
                row0 = i * cb + 32 * r
                kpos = lambda row0=row0: row0 + lax.broadcasted_iota(I32, (32, tq), 0)
                x = sc_ref[pl.ds(pl.multiple_of(row0, 32), 32), :]
                acc = acc + jnp.where(pred(x, kpos), 1.0, 0.0)
            return acc
        acc = lax.fori_loop(0, npair, body, jnp.zeros((32, tq), F32))
        return jnp.sum(acc, axis=0, keepdims=True)

    def count_rounded(cb16):
        def body(i, acc):
            for r in range(cb // 32):
                h = hs_ref[pl.ds(pl.multiple_of(i * cb + 32 * r, 32), 32), :]
                hit = jnp.where(h >= cb16, jnp.ones_like(h), jnp.zeros_like(h))
                acc = acc + hit
            return acc
        acc = lax.fori_loop(0, npair, body, jnp.zeros((32, tq), BF16))
        return jnp.sum(acc.astype(F32), axis=0, keepdims=True)

    def bf16_point(top):
        return top | jnp.where(top < 0, 0, 0xFFFF)

    def coarse_body(i, prefix):
        cand = prefix | (jnp.int32(1) << (31 - i))
        cnt = count_rounded(as_float(bf16_point(cand)).astype(BF16))
        return jnp.where(cnt >= topk, cand, prefix)

    coarse = bf16_point(lax.fori_loop(0, 16, coarse_body, jnp.zeros((1, tq), I32)))

    def fine_body(i, lohi):
        lo, hi = lohi
        mid = lo + ((hi - lo + 1) >> 1)
        cf = as_float(mid)
        ok = count(lambda x, kpos: x >= cf) >= topk
        return jnp.where(ok, mid, lo), jnp.where(ok, hi, mid - 1)

    lo, _ = lax.fori_loop(0, 17, fine_body, (coarse - (2 ** 15 + 1), coarse + (2 ** 16 + 1)))
    thr = as_float(lo)

    n_gt = count(lambda x, kpos: x > thr)
    n_eq = count(lambda x, kpos: x == thr)
    need = topk - n_gt
    any_over = jnp.max(n_eq - need) > 0.0
    idx_bits = int(s_pad).bit_length()

    def tie_search():
        def tbody(i, lim):
            cand = lim | (jnp.int32(1) << (idx_bits - 1 - i))
            n = count(lambda x, kpos: (x == thr) & (kpos() < cand))
            return jnp.where(n <= need, cand, lim)
        return lax.fori_loop(0, idx_bits, tbody, jnp.zeros((1, tq), I32))

    lim = lax.cond(any_over, tie_search, lambda: jnp.full((1, tq), 2 ** idx_bits - 1, I32))

    qs = q_ref[0].reshape(hq, LANE)

    def logits_pair(i, m):
        for half in range(2):
            b = 2 * i + half
            for r in range(kb // rc):
                x = sc_ref[sub(b, r), :]
                adm, kpos = admissible_rows(b, r)
                sel = ((x > thr) | ((x == thr) & (kpos < lim))) & adm
                bias_ref[half * kb + r * rc:half * kb + (r + 1) * rc, :] = jnp.where(sel, 0.0, NEG_INF)
        tops = [m[:, hsl(h)] for h in range(N_HEADS)]
        for half in range(2):
            b = 2 * i + half
            raw = _dot_nt(k_ref[0, blk(b), :], qs)
            for h in range(N_HEADS):
                lh = raw[:, hsl(h)] + bias_ref[half * kb:(half + 1) * kb, :]
                lg_ref[blk(b), hsl(h)] = lh
                tops[h] = jnp.maximum(tops[h], fold8(lh, jnp.max))
        return jnp.concatenate(tops, axis=1)

    m8 = lax.fori_loop(0, npair, logits_pair, jnp.full((8, hq), NEG_INF, F32))
    m = jnp.max(m8, axis=0, keepdims=True)

    acc_ref[...] = jnp.zeros_like(acc_ref)

    def pv_pair(i, ssum):
        p = jnp.exp2(lg_ref[pair(i), :] - m)
        acc_ref[...] += _dot(vt_ref[0, i], p.astype(BF16))
        return ssum + fold8(p, jnp.sum)

    s8 = lax.fori_loop(0, npair, pv_pair, jnp.zeros((8, hq), F32))
    out_t = acc_ref[...] / jnp.sum(s8, axis=0, keepdims=True)
    for h in range(N_HEADS):
        at_ref[HEAD_DIM * h:HEAD_DIM * (h + 1), :] = out_t[:, hsl(h)]
    o_ref[0] = _dot(at_ref[...].T.astype(BF16), wo_ref[...]).astype(BF16)


def _dsa(q, qc, wt, kc, kp, vt, wo, *, tq, kb, s_real, q_off, q_wrap):
    nb, _, t_len, _ = q.shape
    s_pad = kc.shape[1]
    cb = 2 * kb
    topk = min(TOPK_MAX, s_real // 4)
    hq = N_HEADS * tq
    qspec = pl.BlockSpec((1, N_HEADS, tq, LANE), lambda b, j: (b, 0, j, 0))
    return pl.pallas_call(
        functools.partial(_dsa_body, tq=tq, kb=kb, s_real=s_real, s_pad=s_pad, q_off=q_off,
                          q_wrap=q_wrap, topk=float(topk)),
        grid=(nb, t_len // tq),
        in_specs=[qspec, qspec,
                  pl.BlockSpec((1, IDX_HEADS, tq), lambda b, j: (b, 0, j)),
                  pl.BlockSpec((1, s_pad, LANE), lambda b, j: (b, 0, 0)),
                  pl.BlockSpec((1, s_pad, LANE), lambda b, j: (b, 0, 0)),
                  pl.BlockSpec((1, s_pad // cb, HEAD_DIM, cb), lambda b, j: (b, 0, 0, 0)),
                  pl.BlockSpec((N_HEADS * HEAD_DIM, D_MODEL), lambda b, j: (0, 0))],
        out_specs=pl.BlockSpec((1, tq, D_MODEL), lambda b, j: (b, j, 0)),
        out_shape=jax.ShapeDtypeStruct((nb, t_len, D_MODEL), BF16),
        scratch_shapes=[pltpu.VMEM((s_pad, tq), F32), pltpu.VMEM((s_pad, hq), F32),
                        pltpu.VMEM((HEAD_DIM, hq), F32), pltpu.VMEM((N_HEADS * HEAD_DIM, tq), F32),
                        pltpu.VMEM((kb, hq), F32), pltpu.VMEM((kb, hq), F32),
                        pltpu.VMEM((cb, tq), F32), pltpu.VMEM((s_pad, tq), BF16)],
        compiler_params=pltpu.CompilerParams(dimension_semantics=("parallel", "parallel"),
                                             vmem_limit_bytes=VMEM_LIMIT),
        name="dsa",
    )(q, qc, wt, kc, kp, vt, wo)


def _merge_body(x_ref, z_ref, ga_ref, gb_ref, ob_ref, mod_ref, wglu_ref, wout_ref, o_ref):
    glu = _dot(z_ref[...], wglu_ref[...])
    out_a = glu[:, :D_MODEL] * jax.nn.sigmoid(glu[:, D_MODEL:])
    merged = ga_ref[...] * out_a + gb_ref[...] * ob_ref[...]
    upd = _dot(merged.astype(BF16), wout_ref[...])
    o_ref[...] = x_ref[...] + _gate_mul(upd, mod_ref[...], 2)


def _merge(xf, z, ga, gb, ob, mod3, wglu, wout, t_len, tm):
    n = xf.shape[0]
    row = lambda w: pl.BlockSpec((tm, w), lambda i: (i, 0))
    const = lambda s: pl.BlockSpec(s, lambda i: (0, 0))
    return pl.pallas_call(
        _merge_body,
        grid=(n // tm,),
        in_specs=[row(D_MODEL), row(SSM_WIDTH), row(D_MODEL), row(D_MODEL), row(D_MODEL),
                  _mod_spec(tm, t_len), const((SSM_WIDTH, 2 * D_MODEL)), const((D_MODEL, D_MODEL))],
        out_specs=row(D_MODEL),
        out_shape=jax.ShapeDtypeStruct((n, D_MODEL), F32),
        compiler_params=pltpu.CompilerParams(dimension_semantics=("parallel",),
                                             vmem_limit_bytes=VMEM_LIMIT),
        name="merge",
    )(xf, z, ga, gb, ob, mod3, wglu, wout)


def _moe_body(x_ref, mod_ref, wr_ref, br_ref, wgu_ref, wd_ref, o_ref, hb_ref, comb_ref, acc_ref):
    e = pl.program_id(1)
    tm = x_ref.shape[0]
    lane = lax.broadcasted_iota(I32, (tm, LANE), 1).astype(F32)

    @pl.when(e == 0)
    def _():
        h2 = _modulate(_rms(x_ref[...], D_MODEL), mod_ref[...], 4, 3)
        hb_ref[...] = h2.astype(BF16)
        lg = _dot_hilo(h2, wr_ref[...]) + br_ref[...]
        neg = -jnp.inf
        gl = jnp.where(lane < N_GROUPS, lg, neg)
        gmax = jnp.max(gl, axis=-1, keepdims=True)
        g_idx = jnp.min(jnp.where(gl == gmax, lane, float(LANE)), axis=-1, keepdims=True)
        g_w = 1.0 / jnp.sum(jnp.where(lane < N_GROUPS, jnp.exp(lg - gmax), 0.0),
                            axis=-1, keepdims=True)
        lo = N_GROUPS + EXPERTS_PER_GROUP * g_idx
        el = jnp.where((lane >= lo) & (lane < lo + EXPERTS_PER_GROUP), lg, neg)
        v1 = jnp.max(el, axis=-1, keepdims=True)
        i1 = jnp.min(jnp.where(el == v1, lane, float(LANE)), axis=-1, keepdims=True)
        el2 = jnp.where(lane == i1, neg, el)
        v2 = jnp.max(el2, axis=-1, keepdims=True)
        i2 = jnp.min(jnp.where(el2 == v2, lane, float(LANE)), axis=-1, keepdims=True)
        e21 = jnp.exp(v2 - v1)
        w1 = g_w / (1.0 + e21)
        w2 = g_w * e21 / (1.0 + e21)
        comb_ref[...] = jnp.where(lane == i1, w1, 0.0) + jnp.where(lane == i2, w2, 0.0)
        acc_ref[...] = jnp.zeros_like(acc_ref)

    hb = hb_ref[...]
    comb = comb_ref[...]
    parts = []
    for k in range(EXPERTS_PER_GROUP):
        gu = _dot(hb, wgu_ref[k])
        gate = gu[:, :EXPERT_DIM]
        hid = gate * jax.nn.sigmoid(gate) * gu[:, EXPERT_DIM:]
        lane_e = (N_GROUPS + EXPERTS_PER_GROUP * e + k).astype(F32)
        col = jnp.sum(jnp.where(lane == lane_e, comb, 0.0), axis=-1, keepdims=True)
        parts.append((hid * col).astype(BF16))
    acc_ref[...] += _dot(jnp.concatenate(parts, axis=1), wd_ref[0])

    @pl.when(e == N_GROUPS - 1)
    def _():
        o_ref[...] = x_ref[...] + _gate_mul(acc_ref[...], mod_ref[...], 5)


def _moe(x1, mod3, wr, br, wgu, wd, t_len, tm):
    n = x1.shape[0]
    gate_up = pl.BlockSpec((EXPERTS_PER_GROUP, D_MODEL, 2 * EXPERT_DIM), lambda i, e: (e, 0, 0))
    return pl.pallas_call(
        _moe_body,
        grid=(n // tm, N_GROUPS),
        in_specs=[pl.BlockSpec((tm, D_MODEL), lambda i, e: (i, 0)),
                  _mod_spec(tm, t_len),
                  pl.BlockSpec((D_MODEL, LANE), lambda i, e: (0, 0)),
                  pl.BlockSpec((1, LANE), lambda i, e: (0, 0)),
                  gate_up,
                  pl.BlockSpec((1, EXPERTS_PER_GROUP * EXPERT_DIM, D_MODEL),
                               lambda i, e: (e, 0, 0))],
        out_specs=pl.BlockSpec((tm, D_MODEL), lambda i, e: (i, 0)),
        out_shape=jax.ShapeDtypeStruct((n, D_MODEL), F32),
        scratch_shapes=[pltpu.VMEM((tm, D_MODEL), BF16), pltpu.VMEM((tm, LANE), F32),
                        pltpu.VMEM((tm, D_MODEL), F32)],
        compiler_params=pltpu.CompilerParams(dimension_semantics=("parallel", "arbitrary"),
                                             vmem_limit_bytes=VMEM_LIMIT),
        name="moe",
    )(x1, mod3, wr, br, wgu, wd)


def _prep_params(w_in, ssm_a_re, ssm_a_im, ssm_log_dt, ssm_b_re, ssm_b_im, ssm_c_re, ssm_c_im, ssm_d,
                 w_glu, q_gain, k_gain, kidx_gain, w_attn_out, w_out, w_rg, b_rg, w_re, b_re,
                 w_gate, w_up, w_down, max_batch):
    offs = np.cumsum([0, SSM_WIDTH, N_HEADS * HEAD_DIM, HEAD_DIM, HEAD_DIM, IDX_HEADS * IDX_DIM,
                      IDX_DIM, IDX_HEADS, D_MODEL, D_MODEL])
    seg = lambda i: w_in[:, offs[i]:offs[i + 1]]
    padto = lambda a, w: jnp.pad(a, ((0, 0), (0, w - a.shape[-1])))
    wqi = jnp.tile(seg(4).reshape(D_MODEL, IDX_HEADS, IDX_DIM), (1, 1, LANE // IDX_DIM))
    w_pad = jnp.concatenate(
        [seg(0), seg(1), padto(seg(2), LANE), padto(seg(3), LANE),
         wqi.reshape(D_MODEL, IDX_HEADS * LANE), jnp.tile(seg(5), (1, LANE // IDX_DIM)),
         padto(seg(6), LANE), seg(7), seg(8)], axis=1).astype(BF16)
    qg = (q_gain * (HEAD_DIM ** -0.5 * np.log2(np.e)))[None, :]
    kg = padto(k_gain[None, :], LANE)
    ig = jnp.tile(kidx_gain[None, :], (1, LANE // IDX_DIM))

    dt = jnp.exp(ssm_log_dt)[:, None]
    decay = jnp.exp(ssm_a_re * dt)
    lam_re = decay * jnp.cos(ssm_a_im * dt)
    lam_im = decay * jnp.sin(ssm_a_im * dt)
    den = ssm_a_re * ssm_a_re + ssm_a_im * ssm_a_im
    num_re = lam_re - 1.0
    coef_re = (num_re * ssm_a_re + lam_im * ssm_a_im) / den
    coef_im = (lam_im * ssm_a_re - num_re * ssm_a_im) / den
    bb_re = coef_re[..., None] * ssm_b_re - coef_im[..., None] * ssm_b_im
    bb_im = coef_re[..., None] * ssm_b_im + coef_im[..., None] * ssm_b_re
    eye = jnp.eye(SSM_GROUPS, dtype=F32)

    def in_map(bb):
        return jnp.einsum('gpj,gh->gjhp', bb, eye).reshape(SSM_WIDTH, SSM_LANES).astype(BF16)

    def out_map(c):
        return jnp.einsum('gjp,gh->gphj', c, eye).reshape(SSM_LANES, SSM_WIDTH).astype(BF16)

    bcast = lambda a: jnp.broadcast_to(a.reshape(1, SSM_LANES), (max_batch, SSM_LANES))
    ssm = (in_map(bb_re), in_map(bb_im), out_map(ssm_c_re), out_map(-ssm_c_im),
           bcast(lam_re), bcast(lam_im), ssm_d[None, :])

    wr = jnp.pad(jnp.concatenate([w_rg, w_re], axis=1), ((0, 0), (0, LANE - N_GROUPS - N_EXPERTS)))
    br = jnp.pad(jnp.concatenate([b_rg, b_re]), (0, LANE - N_GROUPS - N_EXPERTS))[None, :]
    wd = w_down.astype(BF16).reshape(N_GROUPS, EXPERTS_PER_GROUP * EXPERT_DIM, D_MODEL)
    return dict(w_pad=w_pad, qg=qg, kg=kg, ig=ig, ssm=ssm,
                wglu=w_glu.astype(BF16), wo=w_attn_out.astype(BF16), wout=w_out.astype(BF16),
                wr=wr, br=br, wgu=jnp.concatenate([w_gate, w_up], axis=-1).astype(BF16), wd=wd)


def _layer(x, mod3, p, past, *, tm_tok, tm_moe, tc, tq, kb):
    nb, t_len, _ = x.shape
    n = nb * t_len
    cb = 2 * kb
    reps = max(tq // t_len, 1)
    kbv = min(cb, t_len)
    xf = x.reshape(n, D_MODEL)
    u, q, qc, k, v, ki, kp, kc, vt, wt, ga, gb = _proj(
        xf, mod3, p["w_pad"], p["qg"], p["kg"], p["ig"], nb, t_len, tm_tok, reps, kbv)
    k3 = k.reshape(nb, t_len, HEAD_DIM)
    v3 = v.reshape(nb, t_len, HEAD_DIM)
    ki3 = ki.reshape(nb, t_len, IDX_DIM)
    kp = kp.reshape(nb, t_len, LANE)
    kc = kc.reshape(nb, t_len, LANE)

    if past is None:
        h0r = jnp.zeros((nb, SSM_LANES), F32)
        h0i = h0r
        s_real, q_off = t_len, 0
    else:
        past_k, past_v, past_ki, h0_re, h0_im = past
        h0r = h0_re.reshape(nb, SSM_LANES)
        h0i = h0_im.reshape(nb, SSM_LANES)
        q_off = past_k.shape[1]
        s_real = q_off + t_len
        s_pad = -(-s_real // cb) * cb
        fill = lambda a: jnp.pad(a, ((0, 0), (0, s_pad - s_real), (0, 0)))
        pk = jnp.pad(past_k.astype(BF16), ((0, 0), (0, 0), (0, LANE - HEAD_DIM)))
        kp = fill(jnp.concatenate([pk, kp], axis=1))
        pi_hi, pi_lo = _split_bf16(past_ki)
        kc = fill(jnp.concatenate([jnp.concatenate([pi_hi, pi_lo, pi_hi, pi_lo], axis=-1), kc], axis=1))
        vt_all = jnp.concatenate([past_v.astype(BF16).transpose(0, 2, 1),
                                  vt.transpose(0, 2, 1, 3).reshape(nb, HEAD_DIM, t_len)], axis=2)
        vt_all = jnp.pad(vt_all, ((0, 0), (0, 0), (0, s_pad - s_real)))
        vt = vt_all.reshape(nb, HEAD_DIM, s_pad // cb, cb).transpose(0, 2, 1, 3)

    z, s_re, s_im = _s5(u.reshape(nb, t_len, SSM_WIDTH), h0r, h0i, p["ssm"], nb, t_len, tc)
    ob = _dsa(q, qc, wt, kc, kp, vt, p["wo"], tq=tq, kb=kb, s_real=s_real, q_off=q_off,
              q_wrap=t_len if reps > 1 else tq)
    ob = ob[:, :t_len].reshape(n, D_MODEL)
    x1 = _merge(xf, z.reshape(n, SSM_WIDTH), ga, gb, ob, mod3, p["wglu"], p["wout"],
                t_len, tm_tok)
    x2 = _moe(x1, mod3, p["wr"], p["br"], p["wgu"], p["wd"], t_len, tm_moe)
    return (x2.reshape(nb, t_len, D_MODEL), k3, v3, ki3,
            s_re.reshape(nb, SSM_GROUPS, SSM_STATE), s_im.reshape(nb, SSM_GROUPS, SSM_STATE))


def kernel(x_prompt, x_sample, cache_k, cache_v, cache_kidx, state_ssm_re, state_ssm_im, c_prompt, c_sample, w_ada, b_ada, w_in, ssm_a_re, ssm_a_im, ssm_log_dt, ssm_b_re, ssm_b_im, ssm_c_re, ssm_c_im, ssm_d, w_glu, q_gain, k_gain, kidx_gain, w_attn_out, w_out, w_route_group, b_route_group, w_route_expert, b_route_expert, w_gate, w_up, w_down):
    depth = w_ada.shape[0]
    nbp = x_prompt.shape[0]
    nbs = x_sample.shape[0]
    xp, xs = x_prompt, x_sample
    outs = [[] for _ in range(10)]
    for l in range(depth):
        p = _prep_params(w_in[l], ssm_a_re[l], ssm_a_im[l], ssm_log_dt[l], ssm_b_re[l], ssm_b_im[l],
                         ssm_c_re[l], ssm_c_im[l], ssm_d[l], w_glu[l], q_gain[l], k_gain[l],
                         kidx_gain[l], w_attn_out[l], w_out[l], w_route_group[l], b_route_group[l],
                         w_route_expert[l], b_route_expert[l], w_gate[l], w_up[l], w_down[l],
                         max(nbp, nbs))
        mod = _adaln(jnp.concatenate([c_prompt, c_sample], axis=0), w_ada[l], b_ada[l][None, :])
        mod3 = mod[:, None, :]
        xp, kp, vp, kip, srp, sip = _layer(xp, mod3[:nbp], p, None,
                                           tm_tok=512, tm_moe=1024, tc=64, tq=512, kb=256)
        past = (cache_k[l], cache_v[l], cache_kidx[l], state_ssm_re[l], state_ssm_im[l])
        xs, ks, vs, kis, srs, sis = _layer(xs, mod3[nbp:], p, past,
                                           tm_tok=512, tm_moe=512, tc=64, tq=128, kb=384)
        for lst, val in zip(outs, (kp, vp, kip, srp, sip, ks, vs, kis, srs, sis)):
            lst.append(val)
    return (xp, xs) + tuple(jnp.stack(o) for o in outs)
```

```python
import functools

import jax
import jax.numpy as jnp
import numpy as np
from jax import lax
from jax.experimental import pallas as pl
from jax.experimental.pallas import tpu as pltpu

F32 = jnp.float32
BF16 = jnp.bfloat16
I32 = jnp.int32

D_MODEL = 1024
CHUNK_SHIFT = 6
EPS = 1e-6
NEG_INF = -1e30
SSM_WIDTH = 512
SSM_GROUP = 16
SSM_GROUPS = 32
SSM_STATE = 64
SSM_LANES = SSM_GROUPS * SSM_STATE
S5_EXPAND = SSM_STATE // SSM_GROUP
S5_IN_BLOCK = 256
N_HEADS = 8
HEAD_DIM = 64
IDX_HEADS = 8
IDX_DIM = 32
IDX_WEIGHT_SCALE = (IDX_HEADS * IDX_DIM) ** -0.5
TOPK_MAX = 256
N_GROUPS = 4
EXPERTS_PER_GROUP = 4
N_EXPERTS = 16
EXPERT_DIM = 256
LANE = 128
INT_MIN = -(2 ** 31)
VMEM_LIMIT = 58 * 1024 * 1024

C_U = 0
C_Q = C_U + SSM_WIDTH
C_K = C_Q + N_HEADS * HEAD_DIM
C_V = C_K + LANE
C_QI = C_V + LANE
C_KI = C_QI + IDX_HEADS * LANE
C_WI = C_KI + LANE
C_GA = C_WI + LANE
C_GB = C_GA + D_MODEL
C_END = C_GB + D_MODEL


def _dot(a, b):
    return jnp.dot(a, b, preferred_element_type=F32)


def _dot_nt(a, b):
    return lax.dot_general(a, b, (((1,), (1,)), ((), ())), preferred_element_type=F32)


def _split_bf16(x):
    hi = x.astype(BF16)
    lo = (x - hi.astype(F32)).astype(BF16)
    return hi, lo


def _dot_hilo(a, b):
    ah, al = _split_bf16(a)
    bh, bl = _split_bf16(b)
    return _dot(ah, bh) + _dot(ah, bl) + _dot(al, bh)


def _tiling(tm, t_len):
    return (t_len // tm, 1) if tm < t_len else (1, tm // t_len)


def _mod_spec(tm, t_len):
    per, nbt = _tiling(tm, t_len)
    return pl.BlockSpec((nbt, 1, 6 * D_MODEL), lambda i, *_: (i // per, 0, 0))


def _modulate(x, mod, scale_seg, shift_seg):
    nb = mod.shape[0]
    tm = x.shape[0]
    sc = mod[:, :, scale_seg * D_MODEL:(scale_seg + 1) * D_MODEL]
    sh = mod[:, :, shift_seg * D_MODEL:(shift_seg + 1) * D_MODEL]
    x3 = x.reshape(nb, tm // nb, D_MODEL)
    return (x3 * (1.0 + sc) + sh).reshape(tm, D_MODEL)


def _gate_mul(x, mod, seg):
    nb = mod.shape[0]
    tm = x.shape[0]
    g = mod[:, :, seg * D_MODEL:(seg + 1) * D_MODEL]
    return (x.reshape(nb, tm // nb, D_MODEL) * g).reshape(tm, D_MODEL)


def _rms(x, n):
    return x * lax.rsqrt(jnp.sum(x * x, axis=-1, keepdims=True) * (1.0 / n) + EPS)


def _adaln_body(c_ref, w_ref, b_ref, o_ref):
    c = c_ref[...]
    s = c * jax.nn.sigmoid(c)
    o_ref[...] = _dot_hilo(s, w_ref[...]) + b_ref[...]


def _adaln(c, w, b):
    nb = c.shape[0]
    n = w.shape[1]
    bn = 512
    return pl.pallas_call(
        _adaln_body,
        grid=(n // bn,),
        in_specs=[pl.BlockSpec((nb, D_MODEL), lambda j: (0, 0)),
                  pl.BlockSpec((D_MODEL, bn), lambda j: (0, j)),
                  pl.BlockSpec((1, bn), lambda j: (0, j))],
        out_specs=pl.BlockSpec((nb, bn), lambda j: (0, j)),
        out_shape=jax.ShapeDtypeStruct((nb, n), F32),
        name="adaln",
    )(c, w, b)


def _proj_body(x_ref, mod_ref, w_ref, qg_ref, kg_ref, ig_ref,
               u_ref, q_ref, qc_ref, k_ref, v_ref, ki_ref, kp_ref, kc_ref, vt_ref, wt_ref,
               ga_ref, gb_ref, *, nbt, reps, kbv):
    tm = x_ref.shape[0]
    tt = tm // nbt
    rows = lambda b: slice(tt * b, tt * (b + 1))
    hsl = lambda h: slice(LANE * h, LANE * (h + 1))
    lane = lax.broadcasted_iota(I32, (tm, LANE), 1)
    hb = _modulate(_rms(x_ref[...], D_MODEL), mod_ref[...], 1, 0).astype(BF16)

    u_ref[...] = _dot(hb, w_ref[:, C_U:C_Q]).astype(BF16)

    def put_heads(ref, pieces):
        for b in range(nbt):
            for h in range(N_HEADS):
                for r in range(reps):
                    ref[b, h, tt * r:tt * (r + 1), :] = pieces[h][rows(b), :]

    q = _dot(hb, w_ref[:, C_Q:C_K])
    zpad = jnp.zeros((tm, LANE - HEAD_DIM), BF16)
    heads = []
    for h in range(N_HEADS):
        qh = _rms(q[:, HEAD_DIM * h:HEAD_DIM * (h + 1)], HEAD_DIM) * qg_ref[...]
        heads.append(jnp.concatenate([qh.astype(BF16), zpad], axis=-1))
    put_heads(q_ref, heads)

    qi_hi, qi_lo = _split_bf16(_dot(hb, w_ref[:, C_QI:C_KI]))
    put_heads(qc_ref, [jnp.where(lane < 2 * IDX_DIM, qi_hi[:, hsl(h)], qi_lo[:, hsl(h)])
                       for h in range(IDX_HEADS)])

    kn = _rms(_dot(hb, w_ref[:, C_K:C_V]), HEAD_DIM) * kg_ref[...]
    k_ref[...] = kn[:, :HEAD_DIM]
    kp_ref[...] = kn.astype(BF16)

    vfull = _dot(hb, w_ref[:, C_V:C_QI])
    v_ref[...] = vfull[:, :HEAD_DIM]
    vt = vfull.T[:HEAD_DIM].astype(BF16)
    for b in range(nbt):
        for c in range(tt // kbv):
            vt_ref[b, c] = vt[:, tt * b + kbv * c:tt * b + kbv * (c + 1)]

    kin = _rms(_dot(hb, w_ref[:, C_KI:C_WI]), LANE) * ig_ref[...]
    ki_ref[...] = kin[:, :IDX_DIM]
    ki_hi, ki_lo = _split_bf16(kin)
    kc_ref[...] = jnp.where(((lane >> 5) & 1) == 1, ki_lo, ki_hi)

    wt = (_dot(hb, w_ref[:, C_WI:C_GA]) * IDX_WEIGHT_SCALE).T[:IDX_HEADS]
    for b in range(nbt):
        for r in range(reps):
            wt_ref[b, :, tt * r:tt * (r + 1)] = wt[:, rows(b)]

    ga_ref[...] = jax.nn.sigmoid(_dot(hb, w_ref[:, C_GA:C_GB])).astype(BF16)
    gb_ref[...] = jax.nn.sigmoid(_dot(hb, w_ref[:, C_GB:C_END])).astype(BF16)


def _proj(xf, mod3, w_pad, qg, kg, ig, nb, t_len, tm, reps, kbv):
    n = xf.shape[0]
    per, nbt = _tiling(tm, t_len)
    tt = tm // nbt
    row = lambda w: pl.BlockSpec((tm, w), lambda i: (i, 0))
    const = lambda s: pl.BlockSpec(s, lambda i: (0, 0))
    head_spec = pl.BlockSpec((nbt, N_HEADS, tt * reps, LANE), lambda i: (i // per, 0, i % per, 0))
    head_shape = jax.ShapeDtypeStruct((nb, N_HEADS, t_len * reps, LANE), BF16)
    outs = [
        (row(SSM_WIDTH), jax.ShapeDtypeStruct((n, SSM_WIDTH), BF16)),
        (head_spec, head_shape), (head_spec, head_shape),
        (row(HEAD_DIM), jax.ShapeDtypeStruct((n, HEAD_DIM), F32)),
        (row(HEAD_DIM), jax.ShapeDtypeStruct((n, HEAD_DIM), F32)),
        (row(IDX_DIM), jax.ShapeDtypeStruct((n, IDX_DIM), F32)),
        (row(LANE), jax.ShapeDtypeStruct((n, LANE), BF16)),
        (row(LANE), jax.ShapeDtypeStruct((n, LANE), BF16)),
        (pl.BlockSpec((nbt, tt // kbv, HEAD_DIM, kbv), lambda i: (i // per, i % per, 0, 0)),
         jax.ShapeDtypeStruct((nb, t_len // kbv, HEAD_DIM, kbv), BF16)),
        (pl.BlockSpec((nbt, IDX_HEADS, tt * reps), lambda i: (i // per, 0, i % per)),
         jax.ShapeDtypeStruct((nb, IDX_HEADS, t_len * reps), F32)),
        (row(D_MODEL), jax.ShapeDtypeStruct((n, D_MODEL), BF16)),
        (row(D_MODEL), jax.ShapeDtypeStruct((n, D_MODEL), BF16)),
    ]
    return pl.pallas_call(
        functools.partial(_proj_body, nbt=nbt, reps=reps, kbv=kbv),
        grid=(n // tm,),
        in_specs=[row(D_MODEL), _mod_spec(tm, t_len),
                  pl.BlockSpec((D_MODEL, C_END), lambda i: (0, 0), pipeline_mode=pl.Buffered(1)),
                  const((1, HEAD_DIM)), const((1, LANE)), const((1, LANE))],
        out_specs=[s for s, _ in outs],
        out_shape=[s for _, s in outs],
        compiler_params=pltpu.CompilerParams(dimension_semantics=("parallel",),
                                             vmem_limit_bytes=VMEM_LIMIT),
        name="proj",
    )(xf, mod3, w_pad, qg, kg, ig)


def _s5_body(u_ref, h0r_ref, h0i_ref, bre_ref, bim_ref, cre_ref, cim_ref, lre_ref, lim_ref, d_ref,
             z_ref, sr_ref, si_ref, xr, xi, hr, hi, *, nb, tc, pitch):
    @pl.when(pl.program_id(0) == 0)
    def _():
        hr[...] = h0r_ref[...]
        hi[...] = h0i_ref[...]

    ub = u_ref[...].reshape(nb * tc, SSM_WIDTH)
    brow = lambda b: slice(b * tc, (b + 1) * tc)
    prow = lambda b: slice(b * pitch, b * pitch + tc)
    in_slabs = S5_IN_BLOCK * S5_EXPAND // LANE

    for c in range(SSM_WIDTH // S5_IN_BLOCK):
        cin = slice(S5_IN_BLOCK * c, S5_IN_BLOCK * (c + 1))
        cst = slice(S5_IN_BLOCK * S5_EXPAND * c, S5_IN_BLOCK * S5_EXPAND * (c + 1))
        for dst, w_ref in ((xr, bre_ref), (xi, bim_ref)):
            x = _dot(ub[:, cin], w_ref[cin, cst])
            for k in range(in_slabs):
                for b in range(nb):
                    dst[in_slabs * c + k, prow(b), :] = x[brow(b), LANE * k:LANE * (k + 1)]

    chunk = 8
    for lc in range(SSM_LANES // LANE // chunk):
        slabs = range(chunk * lc, chunk * (lc + 1))
        lsl = lambda k: slice(LANE * k, LANE * (k + 1))

        def step(t, carry, slabs=slabs):
            rows = pl.ds(t, nb, stride=pitch)
            out_a, out_b = [], []
            for (a, b), k in zip(zip(*carry), slabs):
                lr = lre_ref[:, lsl(k)]
                li = lim_ref[:, lsl(k)]
                na = lr * a - li * b + xr[k, rows, :]
                nb_ = lr * b + li * a + xi[k, rows, :]
                xr[k, rows, :] = na
                xi[k, rows, :] = nb_
                out_a.append(na)
                out_b.append(nb_)
            return tuple(out_a), tuple(out_b)

        init = (tuple(hr[:, lsl(k)] for k in slabs), tuple(hi[:, lsl(k)] for k in slabs))
        fin_a, fin_b = lax.fori_loop(0, tc, step, init, unroll=2)
        for a, b, k in zip(fin_a, fin_b, slabs):
            hr[:, lsl(k)] = a
            hi[:, lsl(k)] = b

    out_slabs = S5_EXPAND

    def history(src, c):
        return jnp.concatenate(
            [jnp.concatenate([src[out_slabs * c + j, prow(b), :] for b in range(nb)], axis=0)
             for j in range(out_slabs)], axis=1).astype(BF16)

    for c in range(SSM_WIDTH // LANE):
        cy = slice(LANE * c, LANE * (c + 1))
        cst = slice(LANE * S5_EXPAND * c, LANE * S5_EXPAND * (c + 1))
        y = _dot(history(xr, c), cre_ref[cst, cy]) + _dot(history(xi, c), cim_ref[cst, cy])
        y = y + d_ref[:, cy] * ub[:, cy].astype(F32)
        z_ref[:, :, cy] = jax.nn.gelu(y, approximate=True).astype(BF16).reshape(nb, tc, LANE)
    sr_ref[...] = hr[...]
    si_ref[...] = hi[...]


def _s5(u, h0r, h0i, ssm, nb, t_len, tc):
    bre, bim, cre, cim, lre, lim, dsk = ssm
    pitch = tc + 8
    const = lambda s: pl.BlockSpec(s, lambda c: (0, 0))
    seq = pl.BlockSpec((nb, tc, SSM_WIDTH), lambda c: (0, c, 0))
    slab = pltpu.VMEM((SSM_LANES // LANE, nb * pitch, LANE), F32)
    return pl.pallas_call(
        functools.partial(_s5_body, nb=nb, tc=tc, pitch=pitch),
        grid=(t_len // tc,),
        in_specs=[seq, const((nb, SSM_LANES)), const((nb, SSM_LANES)),
                  const((SSM_WIDTH, SSM_LANES)), const((SSM_WIDTH, SSM_LANES)),
                  const((SSM_LANES, SSM_WIDTH)), const((SSM_LANES, SSM_WIDTH)),
                  const((nb, SSM_LANES)), const((nb, SSM_LANES)), const((1, SSM_WIDTH))],
        out_specs=[seq, const((nb, SSM_LANES)), const((nb, SSM_LANES))],
        out_shape=[jax.ShapeDtypeStruct((nb, t_len, SSM_WIDTH), BF16),
                   jax.ShapeDtypeStruct((nb, SSM_LANES), F32),
                   jax.ShapeDtypeStruct((nb, SSM_LANES), F32)],
        scratch_shapes=[slab, slab,
                        pltpu.VMEM((nb, SSM_LANES), F32), pltpu.VMEM((nb, SSM_LANES), F32)],
        compiler_params=pltpu.CompilerParams(dimension_semantics=("arbitrary",),
                                             vmem_limit_bytes=VMEM_LIMIT),
        name="s5",
    )(u, h0r, h0i, bre, bim, cre, cim, lre[:nb], lim[:nb], dsk)


def _dsa_body(q_ref, qc_ref, wt_ref, kc_ref, k_ref, vt_ref, wo_ref, o_ref,
              sc_ref, lg_ref, acc_ref, at_ref, st0_ref, st1_ref, bias_ref, hs_ref,
              *, tq, kb, s_real, s_pad, q_off, q_wrap, topk):
    j = pl.program_id(1)
    hq = N_HEADS * tq
    n_all = s_pad // kb
    cb = 2 * kb
    lane_q = lax.broadcasted_iota(I32, (1, tq), 1)
    qpos = q_off + j * tq + (lane_q & (q_wrap - 1))
    qchunk = qpos >> CHUNK_SHIFT
    last_end = (((q_off + j * tq + min(tq, q_wrap) - 1) >> CHUNK_SHIFT) + 1) << CHUNK_SHIFT
    npair = jnp.minimum(n_all // 2, lax.div(last_end + (cb - 1), cb))

    def blk(b):
        return pl.ds(pl.multiple_of(b * kb, kb), kb)

    def pair(i):
        return pl.ds(pl.multiple_of(i * cb, cb), cb)

    def fold8(x, op):
        return op(x.reshape(x.shape[0] // 8, 8, x.shape[-1]), axis=0)

    hsl = lambda h: slice(h * tq, (h + 1) * tq)

    def staged(mm, consume, carry):
        st0_ref[...] = mm(0)

        def body(i, c):
            b0 = 2 * i
            st1_ref[...] = mm(b0 + 1)
            c = consume(b0, st0_ref, c)
            st0_ref[...] = mm(jnp.minimum(b0 + 2, n_all - 1))
            return consume(b0 + 1, st1_ref, c)

        return lax.fori_loop(0, npair, body, carry)

    qcs = qc_ref[0].reshape(hq, LANE)
    wt = wt_ref[0]

    rc = 64

    def sub(b, r):
        return pl.ds(pl.multiple_of(b * kb + r * rc, rc), rc)

    def admissible_rows(b, r):
        kpos = b * kb + r * rc + lax.broadcasted_iota(I32, (rc, tq), 0)
        return ((kpos >> CHUNK_SHIFT) <= qchunk) & (kpos < s_real), kpos

    def score_consume(b, st, carry):
        for r in range(kb // rc):
            sc = jnp.zeros((rc, tq), F32)
            for h in range(IDX_HEADS):
                sc = sc + wt[h:h + 1, :] * jnp.maximum(st[r * rc:(r + 1) * rc, hsl(h)], 0.0)
            sc = jnp.where(admissible_rows(b, r)[0], sc, NEG_INF)
            sc_ref[sub(b, r), :] = sc
            hs_ref[sub(b, r), :] = sc.astype(BF16)
        return carry

    staged(lambda b: _dot_nt(kc_ref[0, blk(b), :], qcs), score_consume, 0)

    def as_float(u):
        s = u ^ INT_MIN
        return lax.bitcast_convert_type(s ^ ((s >> 31) & 0x7FFFFFFF), F32)

    def count(pred):
        def body(i, acc):
            for r in range(cb // 32):
                row0 = i * cb + 32 * r
                kpos = lambda row0=row0: row0 + lax.broadcasted_iota(I32, (32, tq), 0)
                x = sc_ref[pl.ds(pl.multiple_of(row0, 32), 32), :]
                acc = acc + jnp.where(pred(x, kpos), 1.0, 0.0)
            return acc
        acc = lax.fori_loop(0, npair, body, jnp.zeros((32, tq), F32))
        return jnp.sum(acc, axis=0, keepdims=True)

    def count_rounded(cb16):
        def body(i, acc):
            for r in range(cb // 32):
                h = hs_ref[pl.ds(pl.multiple_of(i * cb + 32 * r, 32), 32), :]
                hit = jnp.where(h >= cb16, jnp.ones_like(h), jnp.zeros_like(h))
                acc = acc + hit
            return acc
        acc = lax.fori_loop(0, npair, body, jnp.zeros((32, tq), BF16))
        return jnp.sum(acc.astype(F32), axis=0, keepdims=True)

    def bf16_point(top):
        return top | jnp.where(top < 0, 0, 0xFFFF)

    def coarse_body(i, prefix):
        cand = prefix | (jnp.int32(1) << (31 - i))
        cnt = count_rounded(as_float(bf16_point(cand)).astype(BF16))
        return jnp.where(cnt >= topk, cand, prefix)

    coarse = bf16_point(lax.fori_loop(0, 16, coarse_body, jnp.zeros((1, tq), I32)))

    def fine_body(i, lohi):
        lo, hi = lohi
        mid = lo + ((hi - lo + 1) >> 1)
        cf = as_float(mid)
        ok = count(lambda x, kpos: x >= cf) >= topk
        return jnp.where(ok, mid, lo), jnp.where(ok, hi, mid - 1)

    lo, _ = lax.fori_loop(0, 17, fine_body, (coarse - (2 ** 15 + 1), coarse + (2 ** 16 + 1)))
    thr = as_float(lo)

    n_gt = count(lambda x, kpos: x > thr)
    n_eq = count(lambda x, kpos: x == thr)
    need = topk - n_gt
    any_over = jnp.max(n_eq - need) > 0.0
    idx_bits = int(s_pad).bit_length()

    def tie_search():
        def tbody(i, lim):
            cand = lim | (jnp.int32(1) << (idx_bits - 1 - i))
            n = count(lambda x, kpos: (x == thr) & (kpos() < cand))
            return jnp.where(n <= need, cand, lim)
        return lax.fori_loop(0, idx_bits, tbody, jnp.zeros((1, tq), I32))

    lim = lax.cond(any_over, tie_search, lambda: jnp.full((1, tq), 2 ** idx_bits - 1, I32))

    qs = q_ref[0].reshape(hq, LANE)

    def logits_pair(i, m):
        for half in range(2):
            b = 2 * i + half
            for r in range(kb // rc):
                x = sc_ref[sub(b, r), :]
                adm, kpos = admissible_rows(b, r)
                sel = ((x > thr) | ((x == thr) & (kpos < lim))) & adm
                bias_ref[half * kb + r * rc:half * kb + (r + 1) * rc, :] = jnp.where(sel, 0.0, NEG_INF)
        tops = [m[:, hsl(h)] for h in range(N_HEADS)]
        for half in range(2):
            b = 2 * i + half
            raw = _dot_nt(k_ref[0, blk(b), :], qs)
            for h in range(N_HEADS):
                lh = raw[:, hsl(h)] + bias_ref[half * kb:(half + 1) * kb, :]
                lg_ref[blk(b), hsl(h)] = lh
                tops[h] = jnp.maximum(tops[h], fold8(lh, jnp.max))
        return jnp.concatenate(tops, axis=1)

    m8 = lax.fori_loop(0, npair, logits_pair, jnp.full((8, hq), NEG_INF, F32))
    m = jnp.max(m8, axis=0, keepdims=True)

    acc_ref[...] = jnp.zeros_like(acc_ref)

    def pv_pair(i, ssum):
        p = jnp.exp2(lg_ref[pair(i), :] - m)
        acc_ref[...] += _dot(vt_ref[0, i], p.astype(BF16))
        return ssum + fold8(p, jnp.sum)

    s8 = lax.fori_loop(0, npair, pv_pair, jnp.zeros((8, hq), F32))
    out_t = acc_ref[...] / jnp.sum(s8, axis=0, keepdims=True)
    for h in range(N_HEADS):
        at_ref[HEAD_DIM * h:HEAD_DIM * (h + 1), :] = out_t[:, hsl(h)]
    o_ref[0] = _dot(at_ref[...].T.astype(BF16), wo_ref[...]).astype(BF16)


def _dsa(q, qc, wt, kc, kp, vt, wo, *, tq, kb, s_real, q_off, q_wrap):
    nb, _, t_len, _ = q.shape
    s_pad = kc.shape[1]
    cb = 2 * kb
    topk = min(TOPK_MAX, s_real // 4)
    hq = N_HEADS * tq
    qspec = pl.BlockSpec((1, N_HEADS, tq, LANE), lambda b, j: (b, 0, j, 0))
    return pl.pallas_call(
        functools.partial(_dsa_body, tq=tq, kb=kb, s_real=s_real, s_pad=s_pad, q_off=q_off,
                          q_wrap=q_wrap, topk=float(topk)),
        grid=(nb, t_len // tq),
        in_specs=[qspec, qspec,
                  pl.BlockSpec((1, IDX_HEADS, tq), lambda b, j: (b, 0, j)),
                  pl.BlockSpec((1, s_pad, LANE), lambda b, j: (b, 0, 0)),
                  pl.BlockSpec((1, s_pad, LANE), lambda b, j: (b, 0, 0)),
                  pl.BlockSpec((1, s_pad // cb, HEAD_DIM, cb), lambda b, j: (b, 0, 0, 0)),
                  pl.BlockSpec((N_HEADS * HEAD_DIM, D_MODEL), lambda b, j: (0, 0))],
        out_specs=pl.BlockSpec((1, tq, D_MODEL), lambda b, j: (b, j, 0)),
        out_shape=jax.ShapeDtypeStruct((nb, t_len, D_MODEL), BF16),
        scratch_shapes=[pltpu.VMEM((s_pad, tq), F32), pltpu.VMEM((s_pad, hq), F32),
                        pltpu.VMEM((HEAD_DIM, hq), F32), pltpu.VMEM((N_HEADS * HEAD_DIM, tq), F32),
                        pltpu.VMEM((kb, hq), F32), pltpu.VMEM((kb, hq), F32),
                        pltpu.VMEM((cb, tq), F32), pltpu.VMEM((s_pad, tq), BF16)],
        compiler_params=pltpu.CompilerParams(dimension_semantics=("parallel", "parallel"),
                                             vmem_limit_bytes=VMEM_LIMIT),
        name="dsa",
    )(q, qc, wt, kc, kp, vt, wo)


def _merge_body(x_ref, z_ref, ga_ref, gb_ref, ob_ref, mod_ref, wglu_ref, wout_ref, o_ref):
    glu = _dot(z_ref[...], wglu_ref[...])
    out_a = glu[:, :D_MODEL] * jax.nn.sigmoid(glu[:, D_MODEL:])
    merged = ga_ref[...] * out_a + gb_ref[...] * ob_ref[...]
    upd = _dot(merged.astype(BF16), wout_ref[...])
    o_ref[...] = x_ref[...] + _gate_mul(upd, mod_ref[...], 2)


def _merge(xf, z, ga, gb, ob, mod3, wglu, wout, t_len, tm):
    n = xf.shape[0]
    row = lambda w: pl.BlockSpec((tm, w), lambda i: (i, 0))
    const = lambda s: pl.BlockSpec(s, lambda i: (0, 0))
    return pl.pallas_call(
        _merge_body,
        grid=(n // tm,),
        in_specs=[row(D_MODEL), row(SSM_WIDTH), row(D_MODEL), row(D_MODEL), row(D_MODEL),
                  _mod_spec(tm, t_len), const((SSM_WIDTH, 2 * D_MODEL)), const((D_MODEL, D_MODEL))],
        out_specs=row(D_MODEL),
        out_shape=jax.ShapeDtypeStruct((n, D_MODEL), F32),
        compiler_params=pltpu.CompilerParams(dimension_semantics=("parallel",),
                                             vmem_limit_bytes=VMEM_LIMIT),
        name="merge",
    )(xf, z, ga, gb, ob, mod3, wglu, wout)


def _moe_body(x_ref, mod_ref, wr_ref, br_ref, wgu_ref, wd_ref, o_ref, hb_ref, comb_ref, acc_ref):
    e = pl.program_id(1)
    tm = x_ref.shape[0]
    lane = lax.broadcasted_iota(I32, (tm, LANE), 1).astype(F32)

    @pl.when(e == 0)
    def _():
        h2 = _modulate(_rms(x_ref[...], D_MODEL), mod_ref[...], 4, 3)
        hb_ref[...] = h2.astype(BF16)
        h_hi, h_lo = _split_bf16(h2)
        w_hi, w_lo = _split_bf16(wr_ref[...])
        rhs = jnp.concatenate([jnp.concatenate([w_hi, w_lo], axis=1),
                               jnp.concatenate([w_hi, jnp.zeros_like(w_lo)], axis=1)], axis=0)
        both = _dot(jnp.concatenate([h_hi, h_lo], axis=1), rhs)
        lg = both[:, :LANE] + both[:, LANE:] + br_ref[...]
        neg = -jnp.inf
        gl = jnp.where(lane < N_GROUPS, lg, neg)
        gmax = jnp.max(gl, axis=-1, keepdims=True)
        g_idx = jnp.min(jnp.where(gl == gmax, lane, float(LANE)), axis=-1, keepdims=True)
        g_w = 1.0 / jnp.sum(jnp.where(lane < N_GROUPS, jnp.exp(lg - gmax), 0.0),
                            axis=-1, keepdims=True)
        lo = N_GROUPS + EXPERTS_PER_GROUP * g_idx
        el = jnp.where((lane >= lo) & (lane < lo + EXPERTS_PER_GROUP), lg, neg)
        v1 = jnp.max(el, axis=-1, keepdims=True)
        i1 = jnp.min(jnp.where(el == v1, lane, float(LANE)), axis=-1, keepdims=True)
        el2 = jnp.where(lane == i1, neg, el)
        v2 = jnp.max(el2, axis=-1, keepdims=True)
        i2 = jnp.min(jnp.where(el2 == v2, lane, float(LANE)), axis=-1, keepdims=True)
        e21 = jnp.exp(v2 - v1)
        w1 = g_w / (1.0 + e21)
        w2 = g_w * e21 / (1.0 + e21)
        comb_ref[...] = jnp.where(lane == i1, w1, 0.0) + jnp.where(lane == i2, w2, 0.0)
        acc_ref[...] = jnp.zeros_like(acc_ref)

    hb = hb_ref[...]
    comb = comb_ref[...]
    parts = []
    for k in range(EXPERTS_PER_GROUP):
        gu = _dot(hb, wgu_ref[k])
        gate = gu[:, :EXPERT_DIM]
        hid = gate * jax.nn.sigmoid(gate) * gu[:, EXPERT_DIM:]
        lane_e = (N_GROUPS + EXPERTS_PER_GROUP * e + k).astype(F32)
        col = jnp.sum(jnp.where(lane == lane_e, comb, 0.0), axis=-1, keepdims=True)
        parts.append((hid * col).astype(BF16))
    acc_ref[...] += _dot(jnp.concatenate(parts, axis=1), wd_ref[0])

    @pl.when(e == N_GROUPS - 1)
    def _():
        o_ref[...] = x_ref[...] + _gate_mul(acc_ref[...], mod_ref[...], 5)


def _moe(x1, mod3, wr, br, wgu, wd, t_len, tm):
    n = x1.shape[0]
    gate_up = pl.BlockSpec((EXPERTS_PER_GROUP, D_MODEL, 2 * EXPERT_DIM), lambda i, e: (e, 0, 0))
    return pl.pallas_call(
        _moe_body,
        grid=(n // tm, N_GROUPS),
        in_specs=[pl.BlockSpec((tm, D_MODEL), lambda i, e: (i, 0)),
                  _mod_spec(tm, t_len),
                  pl.BlockSpec((D_MODEL, LANE), lambda i, e: (0, 0)),
                  pl.BlockSpec((1, LANE), lambda i, e: (0, 0)),
                  gate_up,
                  pl.BlockSpec((1, EXPERTS_PER_GROUP * EXPERT_DIM, D_MODEL),
                               lambda i, e: (e, 0, 0))],
        out_specs=pl.BlockSpec((tm, D_MODEL), lambda i, e: (i, 0)),
        out_shape=jax.ShapeDtypeStruct((n, D_MODEL), F32),
        scratch_shapes=[pltpu.VMEM((tm, D_MODEL), BF16), pltpu.VMEM((tm, LANE), F32),
                        pltpu.VMEM((tm, D_MODEL), F32)],
        compiler_params=pltpu.CompilerParams(dimension_semantics=("parallel", "arbitrary"),
                                             vmem_limit_bytes=VMEM_LIMIT),
        name="moe",
    )(x1, mod3, wr, br, wgu, wd)


def _prep_params(w_in, ssm_a_re, ssm_a_im, ssm_log_dt, ssm_b_re, ssm_b_im, ssm_c_re, ssm_c_im, ssm_d,
                 w_glu, q_gain, k_gain, kidx_gain, w_attn_out, w_out, w_rg, b_rg, w_re, b_re,
                 w_gate, w_up, w_down, max_batch):
    offs = np.cumsum([0, SSM_WIDTH, N_HEADS * HEAD_DIM, HEAD_DIM, HEAD_DIM, IDX_HEADS * IDX_DIM,
                      IDX_DIM, IDX_HEADS, D_MODEL, D_MODEL])
    seg = lambda i: w_in[:, offs[i]:offs[i + 1]]
    padto = lambda a, w: jnp.pad(a, ((0, 0), (0, w - a.shape[-1])))
    wqi = jnp.tile(seg(4).reshape(D_MODEL, IDX_HEADS, IDX_DIM), (1, 1, LANE // IDX_DIM))
    w_pad = jnp.concatenate(
        [seg(0), seg(1), padto(seg(2), LANE), padto(seg(3), LANE),
         wqi.reshape(D_MODEL, IDX_HEADS * LANE), jnp.tile(seg(5), (1, LANE // IDX_DIM)),
         padto(seg(6), LANE), seg(7), seg(8)], axis=1).astype(BF16)
    qg = (q_gain * (HEAD_DIM ** -0.5 * np.log2(np.e)))[None, :]
    kg = padto(k_gain[None, :], LANE)
    ig = jnp.tile(kidx_gain[None, :], (1, LANE // IDX_DIM))

    dt = jnp.exp(ssm_log_dt)[:, None]
    decay = jnp.exp(ssm_a_re * dt)
    lam_re = decay * jnp.cos(ssm_a_im * dt)
    lam_im = decay * jnp.sin(ssm_a_im * dt)
    den = ssm_a_re * ssm_a_re + ssm_a_im * ssm_a_im
    num_re = lam_re - 1.0
    coef_re = (num_re * ssm_a_re + lam_im * ssm_a_im) / den
    coef_im = (lam_im * ssm_a_re - num_re * ssm_a_im) / den
    bb_re = coef_re[..., None] * ssm_b_re - coef_im[..., None] * ssm_b_im
    bb_im = coef_re[..., None] * ssm_b_im + coef_im[..., None] * ssm_b_re
    eye = jnp.eye(SSM_GROUPS, dtype=F32)

    def in_map(bb):
        return jnp.einsum('gpj,gh->gjhp', bb, eye).reshape(SSM_WIDTH, SSM_LANES).astype(BF16)

    def out_map(c):
        return jnp.einsum('gjp,gh->gphj', c, eye).reshape(SSM_LANES, SSM_WIDTH).astype(BF16)

    bcast = lambda a: jnp.broadcast_to(a.reshape(1, SSM_LANES), (max_batch, SSM_LANES))
    ssm = (in_map(bb_re), in_map(bb_im), out_map(ssm_c_re), out_map(-ssm_c_im),
           bcast(lam_re), bcast(lam_im), ssm_d[None, :])

    wr = jnp.pad(jnp.concatenate([w_rg, w_re], axis=1), ((0, 0), (0, LANE - N_GROUPS - N_EXPERTS)))
    br = jnp.pad(jnp.concatenate([b_rg, b_re]), (0, LANE - N_GROUPS - N_EXPERTS))[None, :]
    wd = w_down.astype(BF16).reshape(N_GROUPS, EXPERTS_PER_GROUP * EXPERT_DIM, D_MODEL)
    return dict(w_pad=w_pad, qg=qg, kg=kg, ig=ig, ssm=ssm,
                wglu=w_glu.astype(BF16), wo=w_attn_out.astype(BF16), wout=w_out.astype(BF16),
                wr=wr, br=br, wgu=jnp.concatenate([w_gate, w_up], axis=-1).astype(BF16), wd=wd)


def _layer(x, mod3, p, past, *, tm_tok, tm_moe, tc, tq, kb):
    nb, t_len, _ = x.shape
    n = nb * t_len
    cb = 2 * kb
    reps = max(tq // t_len, 1)
    kbv = min(cb, t_len)
    xf = x.reshape(n, D_MODEL)
    u, q, qc, k, v, ki, kp, kc, vt, wt, ga, gb = _proj(
        xf, mod3, p["w_pad"], p["qg"], p["kg"], p["ig"], nb, t_len, tm_tok, reps, kbv)
    k3 = k.reshape(nb, t_len, HEAD_DIM)
    v3 = v.reshape(nb, t_len, HEAD_DIM)
    ki3 = ki.reshape(nb, t_len, IDX_DIM)
    kp = kp.reshape(nb, t_len, LANE)
    kc = kc.reshape(nb, t_len, LANE)

    if past is None:
        h0r = jnp.zeros((nb, SSM_LANES), F32)
        h0i = h0r
        s_real, q_off = t_len, 0
    else:
        past_k, past_v, past_ki, h0_re, h0_im = past
        h0r = h0_re.reshape(nb, SSM_LANES)
        h0i = h0_im.reshape(nb, SSM_LANES)
        q_off = past_k.shape[1]
        s_real = q_off + t_len
        s_pad = -(-s_real // cb) * cb
        fill = lambda a: jnp.pad(a, ((0, 0), (0, s_pad - s_real), (0, 0)))
        pk = jnp.pad(past_k.astype(BF16), ((0, 0), (0, 0), (0, LANE - HEAD_DIM)))
        kp = fill(jnp.concatenate([pk, kp], axis=1))
        pi_hi, pi_lo = _split_bf16(past_ki)
        kc = fill(jnp.concatenate([jnp.concatenate([pi_hi, pi_lo, pi_hi, pi_lo], axis=-1), kc], axis=1))
        vt_all = jnp.concatenate([past_v.astype(BF16).transpose(0, 2, 1),
                                  vt.transpose(0, 2, 1, 3).reshape(nb, HEAD_DIM, t_len)], axis=2)
        vt_all = jnp.pad(vt_all, ((0, 0), (0, 0), (0, s_pad - s_real)))
        vt = vt_all.reshape(nb, HEAD_DIM, s_pad // cb, cb).transpose(0, 2, 1, 3)

    z, s_re, s_im = _s5(u.reshape(nb, t_len, SSM_WIDTH), h0r, h0i, p["ssm"], nb, t_len, tc)
    ob = _dsa(q, qc, wt, kc, kp, vt, p["wo"], tq=tq, kb=kb, s_real=s_real, q_off=q_off,
              q_wrap=t_len if reps > 1 else tq)
    ob = ob[:, :t_len].reshape(n, D_MODEL)
    x1 = _merge(xf, z.reshape(n, SSM_WIDTH), ga, gb, ob, mod3, p["wglu"], p["wout"],
                t_len, tm_tok)
    x2 = _moe(x1, mod3, p["wr"], p["br"], p["wgu"], p["wd"], t_len, tm_moe)
    return (x2.reshape(nb, t_len, D_MODEL), k3, v3, ki3,
            s_re.reshape(nb, SSM_GROUPS, SSM_STATE), s_im.reshape(nb, SSM_GROUPS, SSM_STATE))


def kernel(x_prompt, x_sample, cache_k, cache_v, cache_kidx, state_ssm_re, state_ssm_im, c_prompt, c_sample, w_ada, b_ada, w_in, ssm_a_re, ssm_a_im, ssm_log_dt, ssm_b_re, ssm_b_im, ssm_c_re, ssm_c_im, ssm_d, w_glu, q_gain, k_gain, kidx_gain, w_attn_out, w_out, w_route_group, b_route_group, w_route_expert, b_route_expert, w_gate, w_up, w_down):
    depth = w_ada.shape[0]
    nbp = x_prompt.shape[0]
    nbs = x_sample.shape[0]
    xp, xs = x_prompt, x_sample
    outs = [[] for _ in range(10)]
    for l in range(depth):
        p = _prep_params(w_in[l], ssm_a_re[l], ssm_a_im[l], ssm_log_dt[l], ssm_b_re[l], ssm_b_im[l],
                         ssm_c_re[l], ssm_c_im[l], ssm_d[l], w_glu[l], q_gain[l], k_gain[l],
                         kidx_gain[l], w_attn_out[l], w_out[l], w_route_group[l], b_route_group[l],
                         w_route_expert[l], b_route_expert[l], w_gate[l], w_up[l], w_down[l],
                         max(nbp, nbs))
        mod = _adaln(jnp.concatenate([c_prompt, c_sample], axis=0), w_ada[l], b_ada[l][None, :])
        mod3 = mod[:, None, :]
        xp, kp, vp, kip, srp, sip = _layer(xp, mod3[:nbp], p, None,
                                           tm_tok=1024, tm_moe=1024, tc=64, tq=512, kb=256)
        past = (cache_k[l], cache_v[l], cache_kidx[l], state_ssm_re[l], state_ssm_im[l])
        xs, ks, vs, kis, srs, sis = _layer(xs, mod3[nbp:], p, past,
                                           tm_tok=512, tm_moe=512, tc=64, tq=128, kb=768)
        for lst, val in zip(outs, (kp, vp, kip, srp, sip, ks, vs, kis, srs, sis)):
            lst.append(val)
    return (xp, xs) + tuple(jnp.stack(o) for o in outs)
```
